```python
import math
import jax, jax.numpy as jnp
from jax import lax
import numpy as np

D_MODEL = 4096
BATCH = 2
SEQ = 4096
DEPTH = 2

HEAD_DIM = 128
NORM_EPS = 1e-6
L2_EPS = 1e-6
GRID_W = 64
Q_BLOCK = 128
N_EVEN = (DEPTH + 1) // 2
N_ODD = DEPTH // 2

A_HEADS = 16
A_DK = 128
A_DV = 128
A_QK = A_HEADS * A_DK
A_V = A_HEADS * A_DV
A_QKV_CH = 2 * A_QK + A_V
A_CONV_W = 5
A_CHUNK = 64

B_PATTERNS = ((128, 1), (512, 4), (2048, 16))
B_GROUPS = len(B_PATTERNS)
B_HEADS_PER_GROUP = 8
B_HEADS = B_GROUPS * B_HEADS_PER_GROUP
B_WIDTH = B_HEADS * HEAD_DIM
B_OUT = B_HEADS_PER_GROUP * HEAD_DIM
B_ROT_DIM = HEAD_DIM // 4
B_ROPE_THETA = 500000.0

AB_IN = A_QKV_CH + A_V + 4 * A_HEADS + 3 * B_WIDTH
AB_OUT = A_V + B_OUT

C_Q_HEADS = 32
C_KV_HEADS = 8
C_REP = C_Q_HEADS // C_KV_HEADS
C_QKV = (C_Q_HEADS + 2 * C_KV_HEADS) * HEAD_DIM
C_ROPE_THETA = 10000.0

D_FF = ((-(-8 * D_MODEL // 3)) + 255) // 256 * 256

kernel_name = "hybrid_deltanet_dilated_axial_gqa_encoder"


def _rms_norm(x, w):
    xf = x.astype(jnp.float32)
    y = xf * lax.rsqrt(jnp.mean(xf * xf, axis=-1, keepdims=True) + NORM_EPS)
    return (y * w.astype(jnp.float32)).astype(x.dtype)


def _l2norm(x):
    xf = x.astype(jnp.float32)
    return xf * lax.rsqrt(jnp.sum(xf * xf, axis=-1, keepdims=True) + L2_EPS)


def _rope_cos_sin(pos, dim, theta):
    inv = 1.0 / (theta ** (jnp.arange(0, dim, 2, dtype=jnp.float32) / dim))
    ang = pos.astype(jnp.float32)[:, None] * inv[None, :]
    return jnp.cos(ang), jnp.sin(ang)


def _apply_rope(x, cos, sin):
    x1, x2 = jnp.split(x, 2, axis=-1)
    c = cos[:, None, :]
    s = sin[:, None, :]
    return jnp.concatenate([x1 * c - x2 * s, x2 * c + x1 * s], axis=-1).astype(x.dtype)


def _centred_depthwise_conv(x, w):
    pad = w.shape[0] // 2
    return lax.conv_general_dilated(
        x, w[:, None, :].astype(x.dtype), window_strides=(1,), padding=((pad, pad),),
        dimension_numbers=("NWC", "WIO", "NWC"), feature_group_count=x.shape[-1])


def _gated_delta_chunked(q, k, v, g, beta):
    bsz, seq, nh, dk = q.shape
    dv = v.shape[-1]
    c = A_CHUNK
    n = seq // c
    f32 = jnp.float32

    def chunks(t):
        t = t.astype(f32).reshape((bsz, n, c, nh) + t.shape[3:])
        return jnp.moveaxis(jnp.moveaxis(t, 1, 0), 3, 2)

    qc, kc, vc, gc, bc = (chunks(t) for t in (q, k, v, g, beta))
    gc = jnp.cumsum(gc, axis=-1)
    kb = kc * bc[..., None]
    vb = vc * bc[..., None]
    idx = jnp.arange(c)
    incl = idx[:, None] >= idx[None, :]
    strict = idx[:, None] > idx[None, :]
    decay = jnp.exp(jnp.where(incl, gc[..., :, None] - gc[..., None, :], -jnp.inf))
    a_mat = jnp.where(strict, jnp.einsum("nbhid,nbhjd->nbhij", kb, kc) * decay, 0.0)
    eye = jnp.eye(c, dtype=f32)
    t_mat = lax.linalg.triangular_solve(a_mat + eye, jnp.broadcast_to(eye, a_mat.shape),
                                        left_side=True, lower=True, unit_diagonal=True)
    u = t_mat @ vb
    w = t_mat @ (kb * jnp.exp(gc)[..., None])
    attn = jnp.einsum("nbhid,nbhjd->nbhij", qc, kc) * decay

    def step(state, xs):
        q_i, k_i, u_i, w_i, attn_i, g_i = xs
        v_new = u_i - jnp.einsum("bhck,bhkv->bhcv", w_i, state)
        o_i = (jnp.einsum("bhck,bhkv->bhcv", q_i * jnp.exp(g_i)[..., None], state)
               + jnp.einsum("bhij,bhjv->bhiv", attn_i, v_new))
        g_last = g_i[..., -1:]
        state = (state * jnp.exp(g_last)[..., None]
                 + jnp.einsum("bhck,bhcv->bhkv", k_i * jnp.exp(g_last - g_i)[..., None], v_new))
        return state, o_i

    state0 = jnp.zeros((bsz, nh, dk, dv), f32)
    _, out = lax.scan(step, state0, (qc, kc, u, w, attn, gc))
    return jnp.transpose(out, (1, 0, 3, 2, 4)).reshape(bsz, seq, nh, dv).astype(v.dtype)


def _dilated_window_attention(q, k, v):
    bsz, seq, _, hg, dh = q.shape
    scale = dh ** -0.5
    offsets = [jnp.arange(-(wd // (2 * dl)), wd // (2 * dl) + 1) * dl for wd, dl in B_PATTERNS]
    k_groups = [k[:, :, gi] for gi in range(B_GROUPS)]
    v_groups = [v[:, :, gi] for gi in range(B_GROUPS)]

    def block(i):
        start = i * Q_BLOCK
        qpos = start + jnp.arange(Q_BLOCK)
        qb = lax.dynamic_slice_in_dim(q, start, Q_BLOCK, axis=1)
        outs, lses = [], []
        for gi in range(B_GROUPS):
            kpos = qpos[:, None] + offsets[gi][None, :]
            valid = (kpos >= 0) & (kpos < seq)
            kidx = jnp.clip(kpos, 0, seq - 1)
            kg = jnp.take(k_groups[gi], kidx, axis=1)
            vg = jnp.take(v_groups[gi], kidx, axis=1)
            s = jnp.einsum("bqhd,bqkhd->bhqk", qb[:, :, gi], kg).astype(jnp.float32) * scale
            s = jnp.where(valid[None, None], s, -jnp.inf)
            lse = jax.nn.logsumexp(s, axis=-1)
            p = jnp.exp(s - lse[..., None]).astype(vg.dtype)
            outs.append(jnp.einsum("bhqk,bqkhd->bqhd", p, vg).astype(jnp.float32))
            lses.append(lse)
        wts = jax.nn.softmax(jnp.stack(lses, axis=0), axis=0)
        wts = jnp.transpose(wts, (0, 1, 3, 2))[..., None]
        o = jnp.sum(jnp.stack(outs, axis=0) * wts, axis=0)
        return o.astype(q.dtype)

    out = lax.map(block, jnp.arange(seq // Q_BLOCK))
    return jnp.moveaxis(out, 0, 1).reshape(bsz, seq, hg * dh)


def _mixer_ab(h, w_in, conv_w, a_log, dt_bias, out_norm, w_out, cos_b, sin_b):
    bsz, seq, _ = h.shape
    proj = h @ w_in
    cuts = np.cumsum([A_QKV_CH, A_V, 2 * A_HEADS, 2 * A_HEADS, B_WIDTH, B_WIDTH]).tolist()
    a_qkv, a_z, a_beta, a_alpha, b_q, b_k, b_v = jnp.split(proj, cuts, axis=-1)

    a_qkv = jax.nn.silu(_centred_depthwise_conv(a_qkv, conv_w))
    aq, ak, av = jnp.split(a_qkv, [A_QK, 2 * A_QK], axis=-1)
    aq = _l2norm(aq.reshape(bsz, seq, A_HEADS, A_DK)) * (A_DK ** -0.5)
    ak = _l2norm(ak.reshape(bsz, seq, A_HEADS, A_DK))
    av = av.reshape(bsz, seq, A_HEADS, A_DV)
    beta = jax.nn.sigmoid(a_beta.astype(jnp.float32)).reshape(bsz, seq, 2, A_HEADS)
    g = -jnp.exp(a_log.astype(jnp.float32)) * jax.nn.softplus(
        a_alpha.astype(jnp.float32).reshape(bsz, seq, 2, A_HEADS) + dt_bias.astype(jnp.float32))
    o_fwd = _gated_delta_chunked(aq, ak, av, g[:, :, 0], beta[:, :, 0])
    flip = lambda t: jnp.flip(t, axis=1)
    o_bwd = flip(_gated_delta_chunked(flip(aq), flip(ak), flip(av), flip(g[:, :, 1]), flip(beta[:, :, 1])))
    o_a = _rms_norm(o_fwd + o_bwd, out_norm) * jax.nn.silu(a_z.reshape(bsz, seq, A_HEADS, A_DV))
    o_a = o_a.reshape(bsz, seq, A_V)

    def rot(t):
        t = t.reshape(bsz, seq, B_HEADS, HEAD_DIM)
        t = jnp.concatenate([_apply_rope(t[..., :B_ROT_DIM], cos_b, sin_b), t[..., B_ROT_DIM:]], axis=-1)
        return t.reshape(bsz, seq, B_GROUPS, B_HEADS_PER_GROUP, HEAD_DIM)
    bq, bk = rot(b_q), rot(b_k)
    bv = b_v.reshape(bsz, seq, B_GROUPS, B_HEADS_PER_GROUP, HEAD_DIM)
    o_b = _dilated_window_attention(bq, bk, bv)

    return jnp.concatenate([o_a.astype(h.dtype), o_b.astype(h.dtype)], axis=-1) @ w_out


def _mixer_c(h, w_qkv, q_norm, k_norm, w_out, cos_r, sin_r, cos_c, sin_c):
    bsz, seq, _ = h.shape
    proj = h @ w_qkv
    q, k, v = jnp.split(proj, [C_Q_HEADS * HEAD_DIM, (C_Q_HEADS + C_KV_HEADS) * HEAD_DIM], axis=-1)
    q = _rms_norm(q.reshape(bsz, seq, C_Q_HEADS, HEAD_DIM), q_norm)
    k = _rms_norm(k.reshape(bsz, seq, C_KV_HEADS, HEAD_DIM), k_norm)
    v = v.reshape(bsz, seq, C_KV_HEADS, HEAD_DIM)
    half = HEAD_DIM // 2

    def axial(t):
        return jnp.concatenate([_apply_rope(t[..., :half], cos_r, sin_r),
                                _apply_rope(t[..., half:], cos_c, sin_c)], axis=-1)
    q = axial(q) * (HEAD_DIM ** -0.5)
    k = axial(k)
    qg = q.reshape(bsz, seq, C_KV_HEADS, C_REP, HEAD_DIM)

    def block(i):
        qb = lax.dynamic_slice_in_dim(qg, i * Q_BLOCK, Q_BLOCK, axis=1)
        s = jnp.einsum("bqgrd,bkgd->bgrqk", qb, k).astype(jnp.float32)
        p = jax.nn.softmax(s, axis=-1).astype(v.dtype)
        return jnp.einsum("bgrqk,bkgd->bqgrd", p, v)

    o = lax.map(block, jnp.arange(seq // Q_BLOCK))
    o = jnp.moveaxis(o, 0, 1).reshape(bsz, seq, C_Q_HEADS * HEAD_DIM)
    return o @ w_out


def _swiglu(h, w_gate, w_up, w_down):
    return (jax.nn.silu(h @ w_gate) * (h @ w_up)) @ w_down


def setup_inputs(seed: int = 0) -> dict:
    key = jax.random.key(seed)
    ks = jax.random.split(key, 20)
    f32 = jnp.float32

    def dense(k, shape, fan_in):
        return jax.random.normal(k, shape, f32) * (fan_in ** -0.5)

    def gain(k, shape):
        return 1.0 + 0.05 * jax.random.normal(k, shape, f32)

    x = jax.random.normal(ks[0], (BATCH, SEQ, D_MODEL), f32)
    norm_mix = gain(ks[1], (DEPTH, D_MODEL))
    norm_ffn = gain(ks[2], (DEPTH, D_MODEL))
    norm_final = gain(ks[3], (D_MODEL,))
    ab_w_in = dense(ks[4], (N_EVEN, D_MODEL, AB_IN), D_MODEL)
    ab_conv_w = dense(ks[5], (N_EVEN, A_CONV_W, A_QKV_CH), A_CONV_W)
    ab_a_log = jnp.log(jax.random.uniform(ks[6], (N_EVEN, 2, A_HEADS), f32, 1.0, 16.0))
    dt = jnp.exp(jax.random.uniform(ks[7], (N_EVEN, 2, A_HEADS), f32, math.log(1e-3), math.log(1e-1)))
    ab_dt_bias = dt + jnp.log(-jnp.expm1(-dt))
    ab_out_norm = gain(ks[8], (N_EVEN, A_DV))
    ab_w_out = dense(ks[9], (N_EVEN, AB_OUT, D_MODEL), AB_OUT)
    c_w_qkv = dense(ks[10], (N_ODD, D_MODEL, C_QKV), D_MODEL)
    c_q_norm = gain(ks[11], (N_ODD, HEAD_DIM))
    c_k_norm = gain(ks[12], (N_ODD, HEAD_DIM))
    c_w_out = dense(ks[13], (N_ODD, C_Q_HEADS * HEAD_DIM, D_MODEL), C_Q_HEADS * HEAD_DIM)
    ffn_w_gate = dense(ks[14], (DEPTH, D_MODEL, D_FF), D_MODEL)
    ffn_w_up = dense(ks[15], (DEPTH, D_MODEL, D_FF), D_MODEL)
    ffn_w_down = dense(ks[16], (DEPTH, D_FF, D_MODEL), D_FF)
    return {"x": x, "norm_mix": norm_mix, "norm_ffn": norm_ffn, "norm_final": norm_final,
            "ab_w_in": ab_w_in, "ab_conv_w": ab_conv_w, "ab_a_log": ab_a_log,
            "ab_dt_bias": ab_dt_bias, "ab_out_norm": ab_out_norm, "ab_w_out": ab_w_out,
            "c_w_qkv": c_w_qkv, "c_q_norm": c_q_norm, "c_k_norm": c_k_norm, "c_w_out": c_w_out,
            "ffn_w_gate": ffn_w_gate, "ffn_w_up": ffn_w_up, "ffn_w_down": ffn_w_down}


def reference(x, norm_mix, norm_ffn, norm_final, ab_w_in, ab_conv_w, ab_a_log, ab_dt_bias,
              ab_out_norm, ab_w_out, c_w_qkv, c_q_norm, c_k_norm, c_w_out,
              ffn_w_gate, ffn_w_up, ffn_w_down):
    seq = x.shape[1]
    rows = seq // GRID_W
    tok = jnp.arange(seq)
    row_pos = jnp.repeat(jnp.arange(rows), GRID_W)
    col_pos = jnp.tile(jnp.arange(GRID_W), rows)
    cos_b, sin_b = _rope_cos_sin(tok, B_ROT_DIM, B_ROPE_THETA)
    cos_r, sin_r = _rope_cos_sin(row_pos, HEAD_DIM // 2, C_ROPE_THETA)
    cos_c, sin_c = _rope_cos_sin(col_pos, HEAD_DIM // 2, C_ROPE_THETA)

    for layer in range(DEPTH):
        j = layer // 2
        h = _rms_norm(x, norm_mix[layer])
        if layer % 2 == 0:
            x = x + _mixer_ab(h, ab_w_in[j], ab_conv_w[j], ab_a_log[j], ab_dt_bias[j],
                              ab_out_norm[j], ab_w_out[j], cos_b, sin_b)
        else:
            x = x + _mixer_c(h, c_w_qkv[j], c_q_norm[j], c_k_norm[j], c_w_out[j],
                             cos_r, sin_r, cos_c, sin_c)
        h = _rms_norm(x, norm_ffn[layer])
        x = x + _swiglu(h, ffn_w_gate[layer], ffn_w_up[layer], ffn_w_down[layer])
    return _rms_norm(x, norm_final)
```

```python
import functools
import math

import jax
import jax.numpy as jnp
import numpy as np
from jax import lax
from jax.experimental import pallas as pl
from jax.experimental.pallas import tpu as pltpu

HEAD_DIM = 128
NORM_EPS = 1e-6
L2_EPS = 1e-6
GRID_W = 64
A_HEADS = 16
A_CONV_W = 5
A_CHUNK = 64
B_PATTERNS = ((128, 1), (512, 4), (2048, 16))
B_HEADS_PER_GROUP = 8
B_ROT_DIM = HEAD_DIM // 4
B_ROPE_THETA = 500000.0
C_Q_HEADS = 32
C_KV_HEADS = 8
C_REP = C_Q_HEADS // C_KV_HEADS
C_ROPE_THETA = 10000.0

LANES = 128
V7X_VMEM_BYTES = 64 * 1024 * 1024
VMEM_LIMIT = 56 * 1024 * 1024

BF16 = jnp.bfloat16
F32 = jnp.float32
HI = lax.Precision.HIGHEST


def _cparams(sem, vmem=VMEM_LIMIT):
    return pltpu.CompilerParams(dimension_semantics=sem, vmem_limit_bytes=vmem)


def _rmsnorm_kernel(x_ref, w_ref, o_ref):
    x = x_ref[...].astype(F32)
    ms = jnp.mean(x * x, axis=-1, keepdims=True)
    o_ref[...] = (x * lax.rsqrt(ms + NORM_EPS) * w_ref[...]).astype(o_ref.dtype)


def _rmsnorm(x2d, w, out_dtype, tr=256):
    m, d = x2d.shape
    return pl.pallas_call(
        _rmsnorm_kernel,
        out_shape=jax.ShapeDtypeStruct((m, d), out_dtype),
        grid=(m // tr,),
        in_specs=[pl.BlockSpec((tr, d), lambda i: (i, 0)),
                  pl.BlockSpec((1, d), lambda i: (0, 0))],
        out_specs=pl.BlockSpec((tr, d), lambda i: (i, 0)),
        compiler_params=_cparams(("parallel",)),
        name="rmsnorm",
    )(x2d, w.reshape(1, d).astype(F32))


def _mm_kernel(*refs, nk, has_res):
    if has_res:
        a_ref, b_ref, r_ref, o_ref = refs[:4]
        rest = refs[4:]
    else:
        a_ref, b_ref, o_ref = refs[:3]
        r_ref = None
        rest = refs[3:]
    part = jnp.dot(a_ref[...], b_ref[...], preferred_element_type=F32)
    if nk == 1:
        if has_res:
            part = part + r_ref[...]
        o_ref[...] = part.astype(o_ref.dtype)
        return
    acc_ref = rest[0]
    k = pl.program_id(2)

    @pl.when(k == 0)
    def _():
        acc_ref[...] = part

    @pl.when(jnp.logical_and(k > 0, k < nk - 1))
    def _():
        acc_ref[...] += part

    @pl.when(k == nk - 1)
    def _():
        tot = acc_ref[...] + part
        if has_res:
            tot = tot + r_ref[...]
        o_ref[...] = tot.astype(o_ref.dtype)


def _matmul(a, b, *, tm, tn, tk=None, out_dtype=F32, residual=None):
    m, kdim = a.shape
    _, n = b.shape
    tk = kdim if tk is None else tk
    tm, tn = min(tm, m), min(tn, n)
    nk = kdim // tk
    assert m % tm == 0 and n % tn == 0 and kdim % tk == 0
    in_specs = [pl.BlockSpec((tm, tk), lambda i, j, k: (i, k)),
                pl.BlockSpec((tk, tn), lambda i, j, k: (k, j))]
    args = [a, b]
    if residual is not None:
        in_specs.append(pl.BlockSpec((tm, tn), lambda i, j, k: (i, j)))
        args.append(residual)
    scratch = [pltpu.VMEM((tm, tn), F32)] if nk > 1 else []
    return pl.pallas_call(
        functools.partial(_mm_kernel, nk=nk, has_res=residual is not None),
        out_shape=jax.ShapeDtypeStruct((m, n), out_dtype),
        grid=(m // tm, n // tn, nk),
        in_specs=in_specs,
        out_specs=pl.BlockSpec((tm, tn), lambda i, j, k: (i, j)),
        scratch_shapes=scratch,
        compiler_params=_cparams(("parallel", "parallel", "arbitrary")),
        name="matmul",
    )(*args)


def _gateup_kernel(h_ref, wg_ref, wu_ref, o_ref):
    h = h_ref[...]
    g = jnp.dot(h, wg_ref[...], preferred_element_type=F32)
    u = jnp.dot(h, wu_ref[...], preferred_element_type=F32)
    o_ref[...] = (g * jax.nn.sigmoid(g) * u).astype(o_ref.dtype)


def _gateup(h, wg, wu, *, tm, tn):
    m, kdim = h.shape
    n = wg.shape[1]
    tm, tn = min(tm, m), min(tn, n)
    assert m % tm == 0 and n % tn == 0
    return pl.pallas_call(
        _gateup_kernel,
        out_shape=jax.ShapeDtypeStruct((m, n), BF16),
        grid=(m // tm, n // tn),
        in_specs=[pl.BlockSpec((tm, kdim), lambda i, j: (i, 0)),
                  pl.BlockSpec((kdim, tn), lambda i, j: (0, j)),
                  pl.BlockSpec((kdim, tn), lambda i, j: (0, j))],
        out_specs=pl.BlockSpec((tm, tn), lambda i, j: (i, j)),
        compiler_params=_cparams(("parallel", "parallel")),
        name="ffn_gate_up",
    )(h, wg, wu)


_CONV_HALO = 16


def _conv_kernel(prev_ref, cur_ref, next_ref, w_ref, o_ref, ext_ref, *, ts, cw, n_qk_blocks, n_q_blocks):
    i = pl.program_id(1)
    nblk = pl.num_programs(1)
    j = pl.program_id(2)
    pad = A_CONV_W // 2
    prev = prev_ref[0].astype(F32)
    nxt = next_ref[0].astype(F32)
    ext_ref[0:_CONV_HALO, :] = jnp.where(i > 0, prev, 0.0)
    ext_ref[_CONV_HALO:_CONV_HALO + ts, :] = cur_ref[0].astype(F32)
    ext_ref[_CONV_HALO + ts:, :] = jnp.where(i < nblk - 1, nxt, 0.0)
    w = w_ref[...]
    y = jnp.zeros((ts, cw), F32)
    for t in range(A_CONV_W):
        y = y + ext_ref[_CONV_HALO - pad + t:_CONV_HALO - pad + t + ts, :] * w[t:t + 1, :]
    y = y * jax.nn.sigmoid(y)
    is_qk = j < n_qk_blocks
    scale = jnp.where(j < n_q_blocks, HEAD_DIM ** -0.5, 1.0).astype(F32)
    for hh in range(cw // HEAD_DIM):
        yh = y[:, hh * HEAD_DIM:(hh + 1) * HEAD_DIM]
        ss = jnp.sum(yh * yh, axis=-1, keepdims=True)
        yn = yh * (lax.rsqrt(ss + L2_EPS) * scale)
        o_ref[0, :, hh * HEAD_DIM:(hh + 1) * HEAD_DIM] = jnp.where(is_qk, yn, yh).astype(o_ref.dtype)


def _conv_silu_l2(proj_a, conv_w, *, qk_width, q_width, n_ch, ts=512, cw=512):
    bsz, seq, _ = proj_a.shape
    ts = min(ts, seq)
    hb = ts // _CONV_HALO
    nhalo = seq // _CONV_HALO
    w8 = jnp.zeros((8, n_ch), F32).at[:A_CONV_W].set(conv_w.astype(F32))
    kern = functools.partial(_conv_kernel, ts=ts, cw=cw, n_qk_blocks=qk_width // cw, n_q_blocks=q_width // cw)
    return pl.pallas_call(
        kern,
        out_shape=jax.ShapeDtypeStruct((bsz, seq, n_ch), BF16),
        grid=(bsz, seq // ts, n_ch // cw),
        in_specs=[
            pl.BlockSpec((1, _CONV_HALO, cw), lambda b, i, j: (b, jnp.maximum(i * hb - 1, 0), j)),
            pl.BlockSpec((1, ts, cw), lambda b, i, j: (b, i, j)),
            pl.BlockSpec((1, _CONV_HALO, cw), lambda b, i, j: (b, jnp.minimum((i + 1) * hb, nhalo - 1), j)),
            pl.BlockSpec((8, cw), lambda b, i, j: (0, j)),
        ],
        out_specs=pl.BlockSpec((1, ts, cw), lambda b, i, j: (b, i, j)),
        scratch_shapes=[pltpu.VMEM((ts + 2 * _CONV_HALO, cw), F32)],
        compiler_params=_cparams(("parallel", "parallel", "parallel")),
        name="conv_silu_l2norm",
    )(proj_a, proj_a, proj_a, w8)


def _delta_kernel(q_ref, k_ref, v_ref, gt_ref, alog_ref, dtb_ref, o_ref, state_ref, *, hg, n_heads):
    c = A_CHUNK
    d = pl.program_id(0) % 2
    grp = pl.program_id(1)
    n = pl.program_id(2)

    @pl.when(n == 0)
    def _():
        state_ref[...] = jnp.zeros_like(state_ref)

    row = lax.broadcasted_iota(jnp.int32, (c, c), 0)
    col = lax.broadcasted_iota(jnp.int32, (c, c), 1)
    ahead = jnp.where(d == 0, row - col, col - row)
    incl = ahead >= 0
    strict = ahead > 0
    tri = jnp.where(incl, 1.0, 0.0).astype(F32)
    eye = jnp.where(row == col, 1.0, 0.0).astype(F32)

    graw = gt_ref[0]
    beta_all = jax.nn.sigmoid(graw)
    z = graw + dtb_ref[0]
    softplus = jnp.maximum(z, 0.0) + jnp.log(1.0 + jnp.exp(-jnp.abs(z)))
    g_all = -jnp.exp(alog_ref[0]) * softplus
    gc_all = jnp.dot(tri, g_all, precision=HI, preferred_element_type=F32)
    gct_all = lax.dot_general(g_all, tri, (((0,), (1,)), ((), ())), precision=HI,
                              preferred_element_type=F32)
    gtot_all = jnp.sum(g_all, axis=0, keepdims=True)

    for hh in range(hg):
        head_lane = n_heads + hh
        sl = slice(hh * HEAD_DIM, (hh + 1) * HEAD_DIM)
        q = q_ref[0, :, sl].astype(F32)
        k = k_ref[0, :, sl].astype(F32)
        v = v_ref[0, :, sl].astype(F32)
        lane = lax.broadcasted_iota(jnp.int32, (c, LANES), 1)
        sel_a = lane == (n_heads + grp * hg + hh)
        sel_b = lane == (grp * hg + hh)
        gcol = jnp.sum(jnp.where(sel_a, gc_all, 0.0), axis=1, keepdims=True)
        beta = jnp.sum(jnp.where(sel_b, beta_all, 0.0), axis=1, keepdims=True)
        gtot = jnp.sum(jnp.where(sel_a[0:1], gtot_all, 0.0), axis=1, keepdims=True)
        subl = lax.broadcasted_iota(jnp.int32, (LANES, c), 0)
        grow = jnp.sum(jnp.where(subl == (n_heads + grp * hg + hh), gct_all, 0.0), axis=0, keepdims=True)
        del head_lane

        diff = gcol - grow
        decay = jnp.where(incl, jnp.exp(jnp.where(incl, diff, 0.0)), 0.0)
        kb = k * beta
        vb = v * beta
        kk = lax.dot_general(kb, k, (((1,), (1,)), ((), ())), precision=HI, preferred_element_type=F32)
        a_mat = jnp.where(strict, kk * decay, 0.0)
        x = -a_mat
        t_mat = eye + x
        p = 2
        while p < c:
            x = jnp.dot(x, x, precision=HI, preferred_element_type=F32)
            t_mat = t_mat + jnp.dot(t_mat, x, precision=HI, preferred_element_type=F32)
            p *= 2
        eg = jnp.exp(gcol)
        u = jnp.dot(t_mat, vb, precision=HI, preferred_element_type=F32)
        w = jnp.dot(t_mat, kb * eg, precision=HI, preferred_element_type=F32)
        attn = lax.dot_general(q, k, (((1,), (1,)), ((), ())), precision=HI, preferred_element_type=F32) * decay
        state = state_ref[hh]
        v_new = u - jnp.dot(w, state, precision=HI, preferred_element_type=F32)
        o = (jnp.dot(q * eg, state, precision=HI, preferred_element_type=F32)
             + jnp.dot(attn, v_new, precision=HI, preferred_element_type=F32))
        kd = k * jnp.exp(gtot - gcol)
        state_ref[hh] = (state * jnp.exp(gtot)
                         + lax.dot_general(kd, v_new, (((0,), (0,)), ((), ())), precision=HI,
                                           preferred_element_type=F32))
        o_ref[0, 0, :, sl] = o.astype(o_ref.dtype)


def _delta_scan(qkv, gates, alog_row, dtb_row, *, n_heads, hg=8):
    bsz, seq, _ = qkv.shape
    c = A_CHUNK
    nchunk = seq // c
    ngrp = n_heads // hg
    wblk = hg * HEAD_DIM

    def cidx(bd, n):
        return jnp.where(bd % 2 == 0, n, nchunk - 1 - n)

    kern = functools.partial(_delta_kernel, hg=hg, n_heads=n_heads)
    return pl.pallas_call(
        kern,
        out_shape=jax.ShapeDtypeStruct((2, bsz, seq, n_heads * HEAD_DIM), F32),
        grid=(bsz * 2, ngrp, nchunk),
        in_specs=[
            pl.BlockSpec((1, c, wblk), lambda bd, g, n: (bd // 2, cidx(bd, n), g)),
            pl.BlockSpec((1, c, wblk), lambda bd, g, n: (bd // 2, cidx(bd, n), ngrp + g)),
            pl.BlockSpec((1, c, wblk), lambda bd, g, n: (bd // 2, cidx(bd, n), 2 * ngrp + g)),
            pl.BlockSpec((1, c, LANES), lambda bd, g, n: (bd // 2, cidx(bd, n), bd % 2)),
            pl.BlockSpec((1, 1, LANES), lambda bd, g, n: (bd % 2, 0, 0)),
            pl.BlockSpec((1, 1, LANES), lambda bd, g, n: (bd % 2, 0, 0)),
        ],
        out_specs=pl.BlockSpec((1, 1, c, wblk), lambda bd, g, n: (bd % 2, bd // 2, cidx(bd, n), g)),
        scratch_shapes=[pltpu.VMEM((hg, HEAD_DIM, HEAD_DIM), F32)],
        compiler_params=_cparams(("parallel", "parallel", "arbitrary")),
        name="gated_delta_scan",
    )(qkv, qkv, qkv, gates, alog_row, dtb_row)


def _delta_out_kernel(of_ref, ob_ref, z_ref, w_ref, o_ref, *, cw):
    o = of_ref[0] + ob_ref[0]
    z = z_ref[...].astype(F32)
    w = w_ref[...]
    for hh in range(cw // HEAD_DIM):
        sl = slice(hh * HEAD_DIM, (hh + 1) * HEAD_DIM)
        oh = o[:, sl]
        ms = jnp.mean(oh * oh, axis=-1, keepdims=True)
        y = (oh * lax.rsqrt(ms + NORM_EPS) * w).astype(BF16).astype(F32)
        zh = z[:, sl]
        o_ref[:, sl] = (oh * lax.rsqrt(ms + NORM_EPS) * w * (zh * jax.nn.sigmoid(zh))).astype(o_ref.dtype)
        del y


def _delta_out(o2, proj_a, out_norm, *, z_col_block, tr=512, cw=512):
    _, t, vdim = o2.shape
    tr = min(tr, t)
    return pl.pallas_call(
        functools.partial(_delta_out_kernel, cw=cw),
        out_shape=jax.ShapeDtypeStruct((t, vdim), BF16),
        grid=(t // tr, vdim // cw),
        in_specs=[
            pl.BlockSpec((1, tr, cw), lambda i, j: (0, i, j)),
            pl.BlockSpec((1, tr, cw), lambda i, j: (1, i, j)),
            pl.BlockSpec((tr, cw), lambda i, j: (i, z_col_block + j)),
            pl.BlockSpec((1, HEAD_DIM), lambda i, j: (0, 0)),
        ],
        out_specs=pl.BlockSpec((tr, cw), lambda i, j: (i, j)),
        compiler_params=_cparams(("parallel", "parallel")),
        name="delta_out_norm_gate",
    )(o2, o2, proj_a, out_norm.reshape(1, HEAD_DIM).astype(F32))


def _rotate_pairs(x, half):
    lane = lax.broadcasted_iota(jnp.int32, x.shape, 1)
    first = (lane % (2 * half)) < half
    return jnp.where(first, pltpu.roll(x, LANES - half, 1), pltpu.roll(x, half, 1))


def _rope_b_kernel(x_ref, cos_ref, sin_ref, o_ref, *, cw):
    cos = cos_ref[...]
    sin = sin_ref[...]
    for hh in range(cw // HEAD_DIM):
        sl = slice(hh * HEAD_DIM, (hh + 1) * HEAD_DIM)
        x = x_ref[:, sl].astype(F32)
        o_ref[:, sl] = (x * cos + _rotate_pairs(x, B_ROT_DIM // 2) * sin).astype(o_ref.dtype)


def _rope_b(proj_b, cos_t, sin_t, *, width, seq, tr=512, cw=1024):
    t = proj_b.shape[0]
    tr = min(tr, seq)
    nsb = seq // tr
    return pl.pallas_call(
        functools.partial(_rope_b_kernel, cw=cw),
        out_shape=jax.ShapeDtypeStruct((t, width), BF16),
        grid=(t // tr, width // cw),
        in_specs=[pl.BlockSpec((tr, cw), lambda i, j: (i, j)),
                  pl.BlockSpec((tr, HEAD_DIM), lambda i, j: (i % nsb, 0)),
                  pl.BlockSpec((tr, HEAD_DIM), lambda i, j: (i % nsb, 0))],
        out_specs=pl.BlockSpec((tr, cw), lambda i, j: (i, j)),
        compiler_params=_cparams(("parallel", "parallel")),
        name="rope_partial",
    )(proj_b, cos_t, sin_t)


_BW = 64
_BQ = 128


def _dil_kernel(*refs, qb, nh, has_prev, last, seq_len):
    if has_prev:
        (q_ref, kc_ref, kp_ref, kn_ref, vc_ref, vp_ref, vn_ref, op_ref, lp_ref,
         o_ref, l_ref, kext, vext) = refs
    else:
        (q_ref, kc_ref, kp_ref, kn_ref, vc_ref, vp_ref, vn_ref,
         o_ref, l_ref, kext, vext) = refs
    i = pl.program_id(2)
    base = i * qb
    kext[0:_BW, :] = kp_ref[0]
    kext[_BW:_BW + qb, :] = kc_ref[0]
    kext[_BW + qb:, :] = kn_ref[0]
    vext[0:_BW, :] = vp_ref[0]
    vext[_BW:_BW + qb, :] = vc_ref[0]
    vext[_BW + qb:, :] = vn_ref[0]
    scale = HEAD_DIM ** -0.5
    nkeys = _BQ + 2 * _BW
    qi = lax.broadcasted_iota(jnp.int32, (_BQ, nkeys), 0)
    kj = lax.broadcasted_iota(jnp.int32, (_BQ, nkeys), 1)
    lane = lax.broadcasted_iota(jnp.int32, (_BQ, LANES), 1)
    for sb in range(qb // _BQ):
        qpos = base + sb * _BQ + qi
        kpos = base + sb * _BQ - _BW + kj
        valid = (jnp.abs(kpos - qpos) <= _BW) & (kpos >= 0) & (kpos < seq_len)
        lse_blk = jnp.zeros((_BQ, LANES), F32)
        for hh in range(nh):
            sl = slice(hh * HEAD_DIM, (hh + 1) * HEAD_DIM)
            q = q_ref[0, sb * _BQ:(sb + 1) * _BQ, sl]
            kx = kext[sb * _BQ:sb * _BQ + nkeys, sl]
            vx = vext[sb * _BQ:sb * _BQ + nkeys, sl]
            s = lax.dot_general(q, kx, (((1,), (1,)), ((), ())), preferred_element_type=F32) * scale
            s = jnp.where(valid, s, -1e30)
            m = jnp.max(s, axis=-1, keepdims=True)
            e = jnp.where(valid, jnp.exp(s - m), 0.0)
            l = jnp.sum(e, axis=-1, keepdims=True)
            lse = m + jnp.log(l)
            p = (e / l).astype(BF16)
            o = jnp.dot(p, vx, preferred_element_type=F32)
            if has_prev:
                lse_p = jnp.sum(jnp.where(lane == hh, lp_ref[0, sb * _BQ:(sb + 1) * _BQ, :], 0.0),
                                axis=-1, keepdims=True)
                o_p = op_ref[0, sb * _BQ:(sb + 1) * _BQ, sl]
                mx = jnp.maximum(lse, lse_p)
                wa = jnp.exp(lse - mx)
                wb = jnp.exp(lse_p - mx)
                tot = wa + wb
                o = (o * wa + o_p * wb) / tot
                lse = mx + jnp.log(tot)
            o_ref[0, sb * _BQ:(sb + 1) * _BQ, sl] = o.astype(o_ref.dtype)
            lse_blk = jnp.where(lane == hh, lse, lse_blk)
        l_ref[0, sb * _BQ:(sb + 1) * _BQ, :] = lse_blk
    del last


def _dilated_group(bq, bk, bv, gi, dil, prev, *, last, out_dtype):
    bsz, seq, width = bq.shape
    gw = B_HEADS_PER_GROUP * HEAD_DIM
    ngrp_cols = width // gw
    ln = seq // dil
    qb = min(512, ln)
    nqb = ln // qb
    hb = qb // _BW
    nhalo = ln // _BW
    view = lambda a: a.reshape(bsz, ln, dil * a.shape[-1])
    cur = pl.BlockSpec((1, qb, gw), lambda b, r, i: (b, i, r * ngrp_cols + gi))
    prv = pl.BlockSpec((1, _BW, gw), lambda b, r, i: (b, jnp.maximum(i * hb - 1, 0), r * ngrp_cols + gi))
    nxt = pl.BlockSpec((1, _BW, gw), lambda b, r, i: (b, jnp.minimum((i + 1) * hb, nhalo - 1), r * ngrp_cols + gi))
    o_spec = pl.BlockSpec((1, qb, gw), lambda b, r, i: (b, i, r))
    l_spec = pl.BlockSpec((1, qb, LANES), lambda b, r, i: (b, i, r))
    in_specs = [cur, cur, prv, nxt, cur, prv, nxt]
    args = [view(bq), view(bk), view(bk), view(bk), view(bv), view(bv), view(bv)]
    if prev is not None:
        in_specs += [o_spec, l_spec]
        args += [view(prev[0]), view(prev[1])]
    kern = functools.partial(_dil_kernel, qb=qb, nh=B_HEADS_PER_GROUP, has_prev=prev is not None,
                             last=last, seq_len=ln)
    o, l = pl.pallas_call(
        kern,
        out_shape=(jax.ShapeDtypeStruct((bsz, ln, dil * gw), out_dtype),
                   jax.ShapeDtypeStruct((bsz, ln, dil * LANES), F32)),
        grid=(bsz, dil, nqb),
        in_specs=in_specs,
        out_specs=(o_spec, l_spec),
        scratch_shapes=[pltpu.VMEM((qb + 2 * _BW, gw), BF16), pltpu.VMEM((qb + 2 * _BW, gw), BF16)],
        compiler_params=_cparams(("parallel", "parallel", "parallel")),
        name=f"dilated_attn_d{dil}",
    )(*args)
    return o.reshape(bsz, seq, gw), l.reshape(bsz, seq, LANES)


def _qknorm_rope_kernel(x_ref, wq_ref, wk_ref, cos_ref, sin_ref, o_ref, *, cw, n_q_blocks):
    j = pl.program_id(1)
    is_q = j < n_q_blocks
    w = jnp.where(is_q, wq_ref[...], wk_ref[...])
    scale = jnp.where(is_q, HEAD_DIM ** -0.5, 1.0).astype(F32)
    cos = cos_ref[...]
    sin = sin_ref[...]
    for hh in range(cw // HEAD_DIM):
        sl = slice(hh * HEAD_DIM, (hh + 1) * HEAD_DIM)
        x = x_ref[:, sl].astype(F32)
        ms = jnp.mean(x * x, axis=-1, keepdims=True)
        y = x * lax.rsqrt(ms + NORM_EPS) * w
        y = y * cos + _rotate_pairs(y, HEAD_DIM // 4) * sin
        o_ref[:, sl] = (y * scale).astype(o_ref.dtype)


def _qknorm_rope(qkv, q_norm, k_norm, cos_t, sin_t, *, q_width, qk_width, seq, tr=512, cw=512):
    t = qkv.shape[0]
    tr = min(tr, seq)
    nsb = seq // tr
    return pl.pallas_call(
        functools.partial(_qknorm_rope_kernel, cw=cw, n_q_blocks=q_width // cw),
        out_shape=jax.ShapeDtypeStruct((t, qk_width), BF16),
        grid=(t // tr, qk_width // cw),
        in_specs=[pl.BlockSpec((tr, cw), lambda i, j: (i, j)),
                  pl.BlockSpec((1, HEAD_DIM), lambda i, j: (0, 0)),
                  pl.BlockSpec((1, HEAD_DIM), lambda i, j: (0, 0)),
                  pl.BlockSpec((tr, HEAD_DIM), lambda i, j: (i % nsb, 0)),
                  pl.BlockSpec((tr, HEAD_DIM), lambda i, j: (i % nsb, 0))],
        out_specs=pl.BlockSpec((tr, cw), lambda i, j: (i, j)),
        compiler_params=_cparams(("parallel", "parallel")),
        name="qk_norm_axial_rope",
    )(qkv, q_norm.reshape(1, HEAD_DIM).astype(F32), k_norm.reshape(1, HEAD_DIM).astype(F32), cos_t, sin_t)


def _flash_kernel(q_ref, k_ref, v_ref, o_ref, *, tq, tk, rep, seq):
    nkv = seq // tk
    for r in range(rep):
        sl = slice(r * HEAD_DIM, (r + 1) * HEAD_DIM)
        q = q_ref[0, :, sl]

        def body(t, carry):
            m, l, acc = carry
            off = pl.multiple_of(t * tk, tk)
            kt = k_ref[0, pl.ds(off, tk), :]
            vt = v_ref[0, pl.ds(off, tk), :]
            s = lax.dot_general(q, kt, (((1,), (1,)), ((), ())), preferred_element_type=F32)
            m_new = jnp.maximum(m, jnp.max(s, axis=-1, keepdims=True))
            alpha = jnp.exp(m - m_new)
            p = jnp.exp(s - m_new)
            l = alpha * l + jnp.sum(p, axis=-1, keepdims=True)
            acc = alpha * acc + jnp.dot(p.astype(BF16), vt, preferred_element_type=F32)
            return m_new, l, acc

        m0 = jnp.full((tq, 1), -1e30, F32)
        l0 = jnp.zeros((tq, 1), F32)
        a0 = jnp.zeros((tq, HEAD_DIM), F32)
        m, l, acc = lax.fori_loop(0, nkv, body, (m0, l0, a0))
        o_ref[0, :, sl] = (acc / l).astype(o_ref.dtype)


def _flash_attention(qk, qkv, *, bsz, seq, tq=512, tk=512):
    qw = C_Q_HEADS * HEAD_DIM
    tq, tk = min(tq, seq), min(tk, seq)
    gq = C_REP * HEAD_DIM
    qk3 = qk.reshape(bsz, seq, qk.shape[-1])
    qkv3 = qkv.reshape(bsz, seq, qkv.shape[-1])
    k_blk0 = qw // HEAD_DIM
    v_blk0 = (qw + C_KV_HEADS * HEAD_DIM) // HEAD_DIM
    out = pl.pallas_call(
        functools.partial(_flash_kernel, tq=tq, tk=tk, rep=C_REP, seq=seq),
        out_shape=jax.ShapeDtypeStruct((bsz, seq, qw), BF16),
        grid=(bsz, C_KV_HEADS, seq // tq),
        in_specs=[pl.BlockSpec((1, tq, gq), lambda b, g, i: (b, i, g)),
                  pl.BlockSpec((1, seq, HEAD_DIM), lambda b, g, i: (b, 0, k_blk0 + g)),
                  pl.BlockSpec((1, seq, HEAD_DIM), lambda b, g, i: (b, 0, v_blk0 + g))],
        out_specs=pl.BlockSpec((1, tq, gq), lambda b, g, i: (b, i, g)),
        compiler_params=_cparams(("parallel", "parallel", "parallel")),
        name="gqa_flash_attention",
    )(qk3, qk3, qkv3)
    return out.reshape(bsz * seq, qw)


def _rope_cos_sin(pos, dim, theta):
    inv = 1.0 / (theta ** (jnp.arange(0, dim, 2, dtype=F32) / dim))
    ang = pos.astype(F32)[:, None] * inv[None, :]
    return jnp.cos(ang), jnp.sin(ang)


def _tables(seq):
    tok = jnp.arange(seq)
    cos_b, sin_b = _rope_cos_sin(tok, B_ROT_DIM, B_ROPE_THETA)
    ones = jnp.ones((seq, HEAD_DIM - B_ROT_DIM), F32)
    cb = jnp.concatenate([cos_b, cos_b, ones], axis=-1)
    sb = jnp.concatenate([-sin_b, sin_b, jnp.zeros_like(ones)], axis=-1)
    cos_r, sin_r = _rope_cos_sin(tok // GRID_W, HEAD_DIM // 2, C_ROPE_THETA)
    cos_c, sin_c = _rope_cos_sin(tok % GRID_W, HEAD_DIM // 2, C_ROPE_THETA)
    cc = jnp.concatenate([cos_r, cos_r, cos_c, cos_c], axis=-1)
    sc = jnp.concatenate([-sin_r, sin_r, -sin_c, sin_c], axis=-1)
    return cb, sb, cc, sc


def _ffn(x2d, norm_w, w_gate, w_up, w_down):
    h = _rmsnorm(x2d, norm_w, BF16)
    act = _gateup(h, w_gate.astype(BF16), w_up.astype(BF16), tm=2048, tn=256)
    dff = act.shape[1]
    tk = dff // 2 if (dff // 2) % LANES == 0 else dff
    return _matmul(act, w_down.astype(BF16), tm=1024, tn=512, tk=tk, out_dtype=F32, residual=x2d)


def _mixer_ab(x2d, norm_w, w_in, conv_w, a_log, dt_bias, out_norm, w_out, cb, sb, *, bsz, seq):
    t, dm = x2d.shape
    n_heads = a_log.shape[-1]
    a_qk = n_heads * HEAD_DIM
    a_v = n_heads * HEAD_DIM
    a_qkv = 2 * a_qk + a_v
    ngate = 2 * n_heads
    b_w = (w_in.shape[1] - a_qkv - a_v - 2 * ngate) // 3
    h = _rmsnorm(x2d, norm_w, BF16)

    w_a = w_in[:, :a_qkv + a_v].astype(BF16)
    w_beta = w_in[:, a_qkv + a_v:a_qkv + a_v + ngate].reshape(dm, 2, n_heads)
    w_alpha = w_in[:, a_qkv + a_v + ngate:a_qkv + a_v + 2 * ngate].reshape(dm, 2, n_heads)
    w_g = jnp.concatenate([w_beta, w_alpha, jnp.zeros((dm, 2, LANES - 2 * n_heads), F32)], axis=-1)
    w_g = w_g.reshape(dm, 2 * LANES).astype(BF16)
    w_b = w_in[:, a_qkv + a_v + 2 * ngate:].astype(BF16)

    proj_a = _matmul(h, w_a, tm=1024, tn=1024, out_dtype=BF16)
    gates = _matmul(h, w_g, tm=1024, tn=2 * LANES, out_dtype=F32)
    proj_b = _matmul(h, w_b, tm=1024, tn=1024, out_dtype=BF16)

    qkv = _conv_silu_l2(proj_a.reshape(bsz, seq, -1), conv_w, qk_width=2 * a_qk, q_width=a_qk, n_ch=a_qkv)
    pad = jnp.zeros((2, LANES - 2 * n_heads), F32)
    alog_row = jnp.concatenate([jnp.zeros((2, n_heads), F32), a_log.astype(F32), pad], axis=-1).reshape(2, 1, LANES)
    dtb_row = jnp.concatenate([jnp.zeros((2, n_heads), F32), dt_bias.astype(F32), pad], axis=-1).reshape(2, 1, LANES)
    o2 = _delta_scan(qkv, gates.reshape(bsz, seq, 2 * LANES), alog_row, dtb_row, n_heads=n_heads)
    o_a = _delta_out(o2.reshape(2, t, a_v), proj_a, out_norm, z_col_block=a_qkv // 512)

    qk_rot = _rope_b(proj_b, cb, sb, width=2 * b_w, seq=seq)
    bq = qk_rot.reshape(bsz, seq, 2 * b_w)
    bv = proj_b.reshape(bsz, seq, 3 * b_w)
    prev = None
    ngroups = len(B_PATTERNS)
    for gi, (_, dil) in enumerate(B_PATTERNS):
        prev = _dilated_group_call(bq, bv, gi, dil, prev, b_w=b_w, last=gi == ngroups - 1)
    o_b = prev[0].reshape(t, -1)

    cat = jnp.concatenate([o_a, o_b], axis=-1)
    return _matmul(cat, w_out.astype(BF16), tm=1024, tn=1024, out_dtype=F32, residual=x2d)


def _dilated_group_call(qk_rot, proj_b, gi, dil, prev, *, b_w, last):
    bsz, seq, _ = qk_rot.shape
    gw = B_HEADS_PER_GROUP * HEAD_DIM
    ncol_qk = 2 * b_w // gw
    ncol_v = 3 * b_w // gw
    nb = b_w // gw
    ln = seq // dil
    qb = min(512, ln)
    nqb = ln // qb
    hb = qb // _BW
    nhalo = ln // _BW
    vqk = qk_rot.reshape(bsz, ln, dil * 2 * b_w)
    vv = proj_b.reshape(bsz, ln, dil * 3 * b_w)
    lo = lambda i: jnp.maximum(i * hb - 1, 0)
    hi = lambda i: jnp.minimum((i + 1) * hb, nhalo - 1)
    q_spec = pl.BlockSpec((1, qb, gw), lambda b, r, i: (b, i, r * ncol_qk + gi))
    kc = pl.BlockSpec((1, qb, gw), lambda b, r, i: (b, i, r * ncol_qk + nb + gi))
    kp = pl.BlockSpec((1, _BW, gw), lambda b, r, i: (b, lo(i), r * ncol_qk + nb + gi))
    kn = pl.BlockSpec((1, _BW, gw), lambda b, r, i: (b, hi(i), r * ncol_qk + nb + gi))
    vc = pl.BlockSpec((1, qb, gw), lambda b, r, i: (b, i, r * ncol_v + 2 * nb + gi))
    vp = pl.BlockSpec((1, _BW, gw), lambda b, r, i: (b, lo(i), r * ncol_v + 2 * nb + gi))
    vn = pl.BlockSpec((1, _BW, gw), lambda b, r, i: (b, hi(i), r * ncol_v + 2 * nb + gi))
    o_spec = pl.BlockSpec((1, qb, gw), lambda b, r, i: (b, i, r))
    l_spec = pl.BlockSpec((1, qb, LANES), lambda b, r, i: (b, i, r))
    in_specs = [q_spec, kc, kp, kn, vc, vp, vn]
    args = [vqk, vqk, vqk, vqk, vv, vv, vv]
    if prev is not None:
        in_specs += [o_spec, l_spec]
        args += [prev[0].reshape(bsz, ln, dil * gw), prev[1].reshape(bsz, ln, dil * LANES)]
    out_dtype = BF16 if last else F32
    kern = functools.partial(_dil_kernel, qb=qb, nh=B_HEADS_PER_GROUP, has_prev=prev is not None,
                             last=last, seq_len=ln)
    o, l = pl.pallas_call(
        kern,
        out_shape=(jax.ShapeDtypeStruct((bsz, ln, dil * gw), out_dtype),
                   jax.ShapeDtypeStruct((bsz, ln, dil * LANES), F32)),
        grid=(bsz, dil, nqb),
        in_specs=in_specs,
        out_specs=(o_spec, l_spec),
        scratch_shapes=[pltpu.VMEM((qb + 2 * _BW, gw), BF16), pltpu.VMEM((qb + 2 * _BW, gw), BF16)],
        compiler_params=_cparams(("parallel", "parallel", "parallel")),
        name=f"dilated_attn_d{dil}",
    )(*args)
    return o.reshape(bsz, seq, gw), l.reshape(bsz, seq, LANES)


def _mixer_c(x2d, norm_w, w_qkv, q_norm, k_norm, w_out, cc, sc, *, bsz, seq):
    h = _rmsnorm(x2d, norm_w, BF16)
    qkv = _matmul(h, w_qkv.astype(BF16), tm=1024, tn=1024, out_dtype=BF16)
    qw = C_Q_HEADS * HEAD_DIM
    kw = C_KV_HEADS * HEAD_DIM
    qk = _qknorm_rope(qkv, q_norm, k_norm, cc, sc, q_width=qw, qk_width=qw + kw, seq=seq)
    o = _flash_attention(qk, qkv, bsz=bsz, seq=seq)
    return _matmul(o, w_out.astype(BF16), tm=1024, tn=1024, out_dtype=F32, residual=x2d)


def kernel(x, norm_mix, norm_ffn, norm_final, ab_w_in, ab_conv_w, ab_a_log, ab_dt_bias, ab_out_norm, ab_w_out,
           c_w_qkv, c_q_norm, c_k_norm, c_w_out, ffn_w_gate, ffn_w_up, ffn_w_down):
    bsz, seq, dm = x.shape
    depth = norm_mix.shape[0]
    cb, sb, cc, sc = _tables(seq)
    x2d = x.reshape(bsz * seq, dm)
    for layer in range(depth):
        j = layer // 2
        if layer % 2 == 0:
            x2d = _mixer_ab(x2d, norm_mix[layer], ab_w_in[j], ab_conv_w[j], ab_a_log[j], ab_dt_bias[j],
                            ab_out_norm[j], ab_w_out[j], cb, sb, bsz=bsz, seq=seq)
        else:
            x2d = _mixer_c(x2d, norm_mix[layer], c_w_qkv[j], c_q_norm[j], c_k_norm[j], c_w_out[j],
                           cc, sc, bsz=bsz, seq=seq)
        x2d = _ffn(x2d, norm_ffn[layer], ffn_w_gate[layer], ffn_w_up[layer], ffn_w_down[layer])
    out = _rmsnorm(x2d, norm_final, x.dtype)
    return out.reshape(bsz, seq, dm)
```

```python
import functools
import math

import jax
import jax.numpy as jnp
import numpy as np
from jax import lax
from jax.experimental import pallas as pl
from jax.experimental.pallas import tpu as pltpu

HEAD_DIM = 128
NORM_EPS = 1e-6
L2_EPS = 1e-6
GRID_W = 64
A_HEADS = 16
A_CONV_W = 5
A_CHUNK = 64
B_PATTERNS = ((128, 1), (512, 4), (2048, 16))
B_HEADS_PER_GROUP = 8
B_ROT_DIM = HEAD_DIM // 4
B_ROPE_THETA = 500000.0
C_Q_HEADS = 32
C_KV_HEADS = 8
C_REP = C_Q_HEADS // C_KV_HEADS
C_ROPE_THETA = 10000.0

LANES = 128
V7X_VMEM_BYTES = 64 * 1024 * 1024
VMEM_LIMIT = 56 * 1024 * 1024

BF16 = jnp.bfloat16
F32 = jnp.float32
HI = lax.Precision.HIGHEST


def _cparams(sem, vmem=VMEM_LIMIT):
    return pltpu.CompilerParams(dimension_semantics=sem, vmem_limit_bytes=vmem)


def _rmsnorm_kernel(x_ref, w_ref, o_ref):
    x = x_ref[...].astype(F32)
    ms = jnp.mean(x * x, axis=-1, keepdims=True)
    o_ref[...] = (x * lax.rsqrt(ms + NORM_EPS) * w_ref[...]).astype(o_ref.dtype)


def _rmsnorm(x2d, w, out_dtype, tr=256):
    m, d = x2d.shape
    return pl.pallas_call(
        _rmsnorm_kernel,
        out_shape=jax.ShapeDtypeStruct((m, d), out_dtype),
        grid=(m // tr,),
        in_specs=[pl.BlockSpec((tr, d), lambda i: (i, 0)),
                  pl.BlockSpec((1, d), lambda i: (0, 0))],
        out_specs=pl.BlockSpec((tr, d), lambda i: (i, 0)),
        compiler_params=_cparams(("parallel",)),
        name="rmsnorm",
    )(x2d, w.reshape(1, d).astype(F32))


def _mm_kernel(*refs, nk, has_res):
    if has_res:
        a_ref, b_ref, r_ref, o_ref = refs[:4]
        rest = refs[4:]
    else:
        a_ref, b_ref, o_ref = refs[:3]
        r_ref = None
        rest = refs[3:]
    part = jnp.dot(a_ref[...], b_ref[...], preferred_element_type=F32)
    if nk == 1:
        if has_res:
            part = part + r_ref[...]
        o_ref[...] = part.astype(o_ref.dtype)
        return
    acc_ref = rest[0]
    k = pl.program_id(2)

    @pl.when(k == 0)
    def _():
        acc_ref[...] = part

    @pl.when(jnp.logical_and(k > 0, k < nk - 1))
    def _():
        acc_ref[...] += part

    @pl.when(k == nk - 1)
    def _():
        tot = acc_ref[...] + part
        if has_res:
            tot = tot + r_ref[...]
        o_ref[...] = tot.astype(o_ref.dtype)


def _matmul(a, b, *, tm, tn, tk=None, out_dtype=F32, residual=None):
    m, kdim = a.shape
    _, n = b.shape
    tk = kdim if tk is None else tk
    tm, tn = min(tm, m), min(tn, n)
    nk = kdim // tk
    assert m % tm == 0 and n % tn == 0 and kdim % tk == 0
    in_specs = [pl.BlockSpec((tm, tk), lambda i, j, k: (i, k)),
                pl.BlockSpec((tk, tn), lambda i, j, k: (k, j))]
    args = [a, b]
    if residual is not None:
        in_specs.append(pl.BlockSpec((tm, tn), lambda i, j, k: (i, j)))
        args.append(residual)
    scratch = [pltpu.VMEM((tm, tn), F32)] if nk > 1 else []
    return pl.pallas_call(
        functools.partial(_mm_kernel, nk=nk, has_res=residual is not None),
        out_shape=jax.ShapeDtypeStruct((m, n), out_dtype),
        grid=(m // tm, n // tn, nk),
        in_specs=in_specs,
        out_specs=pl.BlockSpec((tm, tn), lambda i, j, k: (i, j)),
        scratch_shapes=scratch,
        compiler_params=_cparams(("parallel", "parallel", "arbitrary")),
        name="matmul",
    )(*args)


def _gateup_kernel(h_ref, wg_ref, wu_ref, o_ref):
    h = h_ref[...]
    g = jnp.dot(h, wg_ref[...], preferred_element_type=F32)
    u = jnp.dot(h, wu_ref[...], preferred_element_type=F32)
    o_ref[...] = (g * jax.nn.sigmoid(g) * u).astype(o_ref.dtype)


def _gateup(h, wg, wu, *, tm, tn):
    m, kdim = h.shape
    n = wg.shape[1]
    tm, tn = min(tm, m), min(tn, n)
    assert m % tm == 0 and n % tn == 0
    return pl.pallas_call(
        _gateup_kernel,
        out_shape=jax.ShapeDtypeStruct((m, n), BF16),
        grid=(m // tm, n // tn),
        in_specs=[pl.BlockSpec((tm, kdim), lambda i, j: (i, 0)),
                  pl.BlockSpec((kdim, tn), lambda i, j: (0, j)),
                  pl.BlockSpec((kdim, tn), lambda i, j: (0, j))],
        out_specs=pl.BlockSpec((tm, tn), lambda i, j: (i, j)),
        compiler_params=_cparams(("parallel", "parallel")),
        name="ffn_gate_up",
    )(h, wg, wu)


_CONV_HALO = 16


def _conv_kernel(prev_ref, cur_ref, next_ref, w_ref, o_ref, ext_ref, *, ts, cw, n_qk_blocks, n_q_blocks):
    i = pl.program_id(1)
    nblk = pl.num_programs(1)
    j = pl.program_id(2)
    pad = A_CONV_W // 2
    prev = prev_ref[0].astype(F32)
    nxt = next_ref[0].astype(F32)
    ext_ref[0:_CONV_HALO, :] = jnp.where(i > 0, prev, 0.0)
    ext_ref[_CONV_HALO:_CONV_HALO + ts, :] = cur_ref[0].astype(F32)
    ext_ref[_CONV_HALO + ts:, :] = jnp.where(i < nblk - 1, nxt, 0.0)
    w = w_ref[...]
    y = jnp.zeros((ts, cw), F32)
    for t in range(A_CONV_W):
        y = y + ext_ref[_CONV_HALO - pad + t:_CONV_HALO - pad + t + ts, :] * w[t:t + 1, :]
    y = y * jax.nn.sigmoid(y)
    is_qk = j < n_qk_blocks
    scale = jnp.where(j < n_q_blocks, HEAD_DIM ** -0.5, 1.0).astype(F32)
    for hh in range(cw // HEAD_DIM):
        yh = y[:, hh * HEAD_DIM:(hh + 1) * HEAD_DIM]
        ss = jnp.sum(yh * yh, axis=-1, keepdims=True)
        yn = yh * (lax.rsqrt(ss + L2_EPS) * scale)
        o_ref[0, :, hh * HEAD_DIM:(hh + 1) * HEAD_DIM] = jnp.where(is_qk, yn, yh).astype(o_ref.dtype)


def _conv_silu_l2(proj_a, conv_w, *, qk_width, q_width, n_ch, ts=512, cw=512):
    bsz, seq, _ = proj_a.shape
    ts = min(ts, seq)
    hb = ts // _CONV_HALO
    nhalo = seq // _CONV_HALO
    w8 = jnp.zeros((8, n_ch), F32).at[:A_CONV_W].set(conv_w.astype(F32))
    kern = functools.partial(_conv_kernel, ts=ts, cw=cw, n_qk_blocks=qk_width // cw, n_q_blocks=q_width // cw)
    return pl.pallas_call(
        kern,
        out_shape=jax.ShapeDtypeStruct((bsz, seq, n_ch), BF16),
        grid=(bsz, seq // ts, n_ch // cw),
        in_specs=[
            pl.BlockSpec((1, _CONV_HALO, cw), lambda b, i, j: (b, jnp.maximum(i * hb - 1, 0), j)),
            pl.BlockSpec((1, ts, cw), lambda b, i, j: (b, i, j)),
            pl.BlockSpec((1, _CONV_HALO, cw), lambda b, i, j: (b, jnp.minimum((i + 1) * hb, nhalo - 1), j)),
            pl.BlockSpec((8, cw), lambda b, i, j: (0, j)),
        ],
        out_specs=pl.BlockSpec((1, ts, cw), lambda b, i, j: (b, i, j)),
        scratch_shapes=[pltpu.VMEM((ts + 2 * _CONV_HALO, cw), F32)],
        compiler_params=_cparams(("parallel", "parallel", "parallel")),
        name="conv_silu_l2norm",
    )(proj_a, proj_a, proj_a, w8)


def _split_bf16(x):
    hi = x.astype(BF16)
    lo = (x - hi.astype(F32)).astype(BF16)
    return hi, lo


def _dot3(a, b):
    a_hi, a_lo = a
    b_hi, b_lo = b
    return (jnp.dot(a_hi, b_hi, preferred_element_type=F32)
            + (jnp.dot(a_hi, b_lo, preferred_element_type=F32) + jnp.dot(a_lo, b_hi, preferred_element_type=F32)))


def _delta_kernel(q_ref, k_ref, v_ref, gt_ref, alog_ref, dtb_ref, o_ref, state_ref, *, hg, n_heads):
    c = A_CHUNK
    d = pl.program_id(0) % 2
    grp = pl.program_id(1)
    n = pl.program_id(2)

    @pl.when(n == 0)
    def _():
        state_ref[...] = jnp.zeros_like(state_ref)

    row = lax.broadcasted_iota(jnp.int32, (c, c), 0)
    col = lax.broadcasted_iota(jnp.int32, (c, c), 1)
    ahead = jnp.where(d == 0, row - col, col - row)
    incl = ahead >= 0
    strict = ahead > 0
    tri = jnp.where(incl, 1.0, 0.0).astype(F32)
    eye = jnp.where(row == col, 1.0, 0.0).astype(F32)

    graw = gt_ref[0]
    beta_all = jax.nn.sigmoid(graw)
    z = graw + dtb_ref[0]
    softplus = jnp.maximum(z, 0.0) + jnp.log(1.0 + jnp.exp(-jnp.abs(z)))
    g_all = -jnp.exp(alog_ref[0]) * softplus
    gc_all = jnp.dot(tri, g_all, precision=HI, preferred_element_type=F32)
    gct_all = lax.dot_general(g_all, tri, (((0,), (1,)), ((), ())), precision=HI,
                              preferred_element_type=F32)
    gtot_all = jnp.sum(g_all, axis=0, keepdims=True)
    lane = lax.broadcasted_iota(jnp.int32, (c, LANES), 1)
    subl = lax.broadcasted_iota(jnp.int32, (LANES, c), 0)
    contract_last = (((1,), (1,)), ((), ()))
    contract_first = (((0,), (0,)), ((), ()))

    heads = range(hg)
    sls = [slice(hh * HEAD_DIM, (hh + 1) * HEAD_DIM) for hh in heads]
    q16 = [q_ref[0, :, sl] for sl in sls]
    k16 = [k_ref[0, :, sl] for sl in sls]
    gcol, beta, gtot, decay = [], [], [], []
    for hh in heads:
        a_lane = n_heads + grp * hg + hh
        sel_a = lane == a_lane
        sel_b = lane == (grp * hg + hh)
        gcol.append(jnp.sum(jnp.where(sel_a, gc_all, 0.0), axis=1, keepdims=True))
        beta.append(jnp.sum(jnp.where(sel_b, beta_all, 0.0), axis=1, keepdims=True))
        gtot.append(jnp.sum(jnp.where(sel_a[0:1], gtot_all, 0.0), axis=1, keepdims=True))
        grow = jnp.sum(jnp.where(subl == a_lane, gct_all, 0.0), axis=0, keepdims=True)
        diff = gcol[hh] - grow
        decay.append(jnp.where(incl, jnp.exp(jnp.where(incl, diff, 0.0)), 0.0))
    kb = [k16[hh].astype(F32) * beta[hh] for hh in heads]
    kq = [lax.dot_general(jnp.concatenate([kb[hh].astype(BF16), q16[hh]], axis=0), k16[hh], contract_last,
                          preferred_element_type=F32) for hh in heads]
    attn = [(kq[hh][c:] * decay[hh]).astype(BF16) for hh in heads]
    x = [-jnp.where(strict, kq[hh][:c] * decay[hh], 0.0) for hh in heads]
    t_mat = [eye + x[hh] for hh in heads]
    p = 2
    while p < c:
        xs = [_split_bf16(x[hh]) for hh in heads]
        x = [_dot3(xs[hh], xs[hh]) for hh in heads]
        t_mat = [t_mat[hh] + _dot3(_split_bf16(t_mat[hh]), _split_bf16(x[hh])) for hh in heads]
        p *= 2
    eg = [jnp.exp(gcol[hh]) for hh in heads]
    rhs = [jnp.concatenate([v_ref[0, :, sls[hh]].astype(F32) * beta[hh], kb[hh] * eg[hh]], axis=1).astype(BF16)
           for hh in heads]
    uw = [jnp.dot(t_mat[hh].astype(BF16), rhs[hh], preferred_element_type=F32) for hh in heads]
    state = [state_ref[hh] for hh in heads]
    ws = [jnp.dot(jnp.concatenate([uw[hh][:, HEAD_DIM:], q16[hh].astype(F32) * eg[hh]], axis=0).astype(BF16),
                  state[hh].astype(BF16), preferred_element_type=F32) for hh in heads]
    vn16 = [(uw[hh][:, :HEAD_DIM] - ws[hh][:c]).astype(BF16) for hh in heads]
    for hh in heads:
        o = ws[hh][c:] + jnp.dot(attn[hh], vn16[hh], preferred_element_type=F32)
        o_ref[0, 0, :, sls[hh]] = o.astype(o_ref.dtype)
    for hh in heads:
        kd = (k16[hh].astype(F32) * jnp.exp(gtot[hh] - gcol[hh])).astype(BF16)
        state_ref[hh] = state[hh] * jnp.exp(gtot[hh]) + lax.dot_general(kd, vn16[hh], contract_first,
                                                                       preferred_element_type=F32)


def _delta_scan(qkv, gates, alog_row, dtb_row, *, n_heads, hg=8):
    bsz, seq, _ = qkv.shape
    c = A_CHUNK
    nchunk = seq // c
    ngrp = n_heads // hg
    wblk = hg * HEAD_DIM

    def cidx(bd, n):
        return jnp.where(bd % 2 == 0, n, nchunk - 1 - n)

    kern = functools.partial(_delta_kernel, hg=hg, n_heads=n_heads)
    return pl.pallas_call(
        kern,
        out_shape=jax.ShapeDtypeStruct((2, bsz, seq, n_heads * HEAD_DIM), F32),
        grid=(bsz * 2, ngrp, nchunk),
        in_specs=[
            pl.BlockSpec((1, c, wblk), lambda bd, g, n: (bd // 2, cidx(bd, n), g)),
            pl.BlockSpec((1, c, wblk), lambda bd, g, n: (bd // 2, cidx(bd, n), ngrp + g)),
            pl.BlockSpec((1, c, wblk), lambda bd, g, n: (bd // 2, cidx(bd, n), 2 * ngrp + g)),
            pl.BlockSpec((1, c, LANES), lambda bd, g, n: (bd // 2, cidx(bd, n), bd % 2)),
            pl.BlockSpec((1, 1, LANES), lambda bd, g, n: (bd % 2, 0, 0)),
            pl.BlockSpec((1, 1, LANES), lambda bd, g, n: (bd % 2, 0, 0)),
        ],
        out_specs=pl.BlockSpec((1, 1, c, wblk), lambda bd, g, n: (bd % 2, bd // 2, cidx(bd, n), g)),
        scratch_shapes=[pltpu.VMEM((hg, HEAD_DIM, HEAD_DIM), F32)],
        compiler_params=_cparams(("parallel", "parallel", "arbitrary")),
        name="gated_delta_scan",
    )(qkv, qkv, qkv, gates, alog_row, dtb_row)


def _delta_out_kernel(of_ref, ob_ref, z_ref, w_ref, o_ref, *, cw):
    o = of_ref[0] + ob_ref[0]
    z = z_ref[...].astype(F32)
    w = w_ref[...]
    for hh in range(cw // HEAD_DIM):
        sl = slice(hh * HEAD_DIM, (hh + 1) * HEAD_DIM)
        oh = o[:, sl]
        ms = jnp.mean(oh * oh, axis=-1, keepdims=True)
        y = (oh * lax.rsqrt(ms + NORM_EPS) * w).astype(BF16).astype(F32)
        zh = z[:, sl]
        o_ref[:, sl] = (oh * lax.rsqrt(ms + NORM_EPS) * w * (zh * jax.nn.sigmoid(zh))).astype(o_ref.dtype)
        del y


def _delta_out(o2, proj_a, out_norm, *, z_col_block, tr=512, cw=512):
    _, t, vdim = o2.shape
    tr = min(tr, t)
    return pl.pallas_call(
        functools.partial(_delta_out_kernel, cw=cw),
        out_shape=jax.ShapeDtypeStruct((t, vdim), BF16),
        grid=(t // tr, vdim // cw),
        in_specs=[
            pl.BlockSpec((1, tr, cw), lambda i, j: (0, i, j)),
            pl.BlockSpec((1, tr, cw), lambda i, j: (1, i, j)),
            pl.BlockSpec((tr, cw), lambda i, j: (i, z_col_block + j)),
            pl.BlockSpec((1, HEAD_DIM), lambda i, j: (0, 0)),
        ],
        out_specs=pl.BlockSpec((tr, cw), lambda i, j: (i, j)),
        compiler_params=_cparams(("parallel", "parallel")),
        name="delta_out_norm_gate",
    )(o2, o2, proj_a, out_norm.reshape(1, HEAD_DIM).astype(F32))


def _rotate_pairs(x, half):
    lane = lax.broadcasted_iota(jnp.int32, x.shape, 1)
    first = (lane % (2 * half)) < half
    return jnp.where(first, pltpu.roll(x, LANES - half, 1), pltpu.roll(x, half, 1))


def _rope_b_kernel(x_ref, cos_ref, sin_ref, o_ref, *, cw):
    cos = cos_ref[...]
    sin = sin_ref[...]
    for hh in range(cw // HEAD_DIM):
        sl = slice(hh * HEAD_DIM, (hh + 1) * HEAD_DIM)
        x = x_ref[:, sl].astype(F32)
        o_ref[:, sl] = (x * cos + _rotate_pairs(x, B_ROT_DIM // 2) * sin).astype(o_ref.dtype)


def _rope_b(proj_b, cos_t, sin_t, *, width, seq, tr=512, cw=1024):
    t = proj_b.shape[0]
    tr = min(tr, seq)
    nsb = seq // tr
    return pl.pallas_call(
        functools.partial(_rope_b_kernel, cw=cw),
        out_shape=jax.ShapeDtypeStruct((t, width), BF16),
        grid=(t // tr, width // cw),
        in_specs=[pl.BlockSpec((tr, cw), lambda i, j: (i, j)),
                  pl.BlockSpec((tr, HEAD_DIM), lambda i, j: (i % nsb, 0)),
                  pl.BlockSpec((tr, HEAD_DIM), lambda i, j: (i % nsb, 0))],
        out_specs=pl.BlockSpec((tr, cw), lambda i, j: (i, j)),
        compiler_params=_cparams(("parallel", "parallel")),
        name="rope_partial",
    )(proj_b, cos_t, sin_t)


_BW = 64
_BQ = 128


def _dil_kernel(*refs, qb, nh, has_prev, last, seq_len):
    if has_prev:
        (q_ref, kc_ref, kp_ref, kn_ref, vc_ref, vp_ref, vn_ref, op_ref, lp_ref,
         o_ref, l_ref, kext, vext) = refs
    else:
        (q_ref, kc_ref, kp_ref, kn_ref, vc_ref, vp_ref, vn_ref,
         o_ref, l_ref, kext, vext) = refs
    i = pl.program_id(2)
    base = i * qb
    kext[0:_BW, :] = kp_ref[0]
    kext[_BW:_BW + qb, :] = kc_ref[0]
    kext[_BW + qb:, :] = kn_ref[0]
    vext[0:_BW, :] = vp_ref[0]
    vext[_BW:_BW + qb, :] = vc_ref[0]
    vext[_BW + qb:, :] = vn_ref[0]
    scale = HEAD_DIM ** -0.5
    nkeys = _BQ + 2 * _BW
    qi = lax.broadcasted_iota(jnp.int32, (_BQ, nkeys), 0)
    kj = lax.broadcasted_iota(jnp.int32, (_BQ, nkeys), 1)
    lane = lax.broadcasted_iota(jnp.int32, (_BQ, LANES), 1)
    for sb in range(qb // _BQ):
        qpos = base + sb * _BQ + qi
        kpos = base + sb * _BQ - _BW + kj
        valid = (jnp.abs(kpos - qpos) <= _BW) & (kpos >= 0) & (kpos < seq_len)
        lse_blk = jnp.zeros((_BQ, LANES), F32)
        for hh in range(nh):
            sl = slice(hh * HEAD_DIM, (hh + 1) * HEAD_DIM)
            q = q_ref[0, sb * _BQ:(sb + 1) * _BQ, sl]
            kx = kext[sb * _BQ:sb * _BQ + nkeys, sl]
            vx = vext[sb * _BQ:sb * _BQ + nkeys, sl]
            s = lax.dot_general(q, kx, (((1,), (1,)), ((), ())), preferred_element_type=F32) * scale
            s = jnp.where(valid, s, -1e30)
            m = jnp.max(s, axis=-1, keepdims=True)
            e = jnp.where(valid, jnp.exp(s - m), 0.0)
            l = jnp.sum(e, axis=-1, keepdims=True)
            lse = m + jnp.log(l)
            p = (e / l).astype(BF16)
            o = jnp.dot(p, vx, preferred_element_type=F32)
            if has_prev:
                lse_p = jnp.sum(jnp.where(lane == hh, lp_ref[0, sb * _BQ:(sb + 1) * _BQ, :], 0.0),
                                axis=-1, keepdims=True)
                o_p = op_ref[0, sb * _BQ:(sb + 1) * _BQ, sl]
                mx = jnp.maximum(lse, lse_p)
                wa = jnp.exp(lse - mx)
                wb = jnp.exp(lse_p - mx)
                tot = wa + wb
                o = (o * wa + o_p * wb) / tot
                lse = mx + jnp.log(tot)
            o_ref[0, sb * _BQ:(sb + 1) * _BQ, sl] = o.astype(o_ref.dtype)
            lse_blk = jnp.where(lane == hh, lse, lse_blk)
        l_ref[0, sb * _BQ:(sb + 1) * _BQ, :] = lse_blk
    del last


def _dilated_group(bq, bk, bv, gi, dil, prev, *, last, out_dtype):
    bsz, seq, width = bq.shape
    gw = B_HEADS_PER_GROUP * HEAD_DIM
    ngrp_cols = width // gw
    ln = seq // dil
    qb = min(512, ln)
    nqb = ln // qb
    hb = qb // _BW
    nhalo = ln // _BW
    view = lambda a: a.reshape(bsz, ln, dil * a.shape[-1])
    cur = pl.BlockSpec((1, qb, gw), lambda b, r, i: (b, i, r * ngrp_cols + gi))
    prv = pl.BlockSpec((1, _BW, gw), lambda b, r, i: (b, jnp.maximum(i * hb - 1, 0), r * ngrp_cols + gi))
    nxt = pl.BlockSpec((1, _BW, gw), lambda b, r, i: (b, jnp.minimum((i + 1) * hb, nhalo - 1), r * ngrp_cols + gi))
    o_spec = pl.BlockSpec((1, qb, gw), lambda b, r, i: (b, i, r))
    l_spec = pl.BlockSpec((1, qb, LANES), lambda b, r, i: (b, i, r))
    in_specs = [cur, cur, prv, nxt, cur, prv, nxt]
    args = [view(bq), view(bk), view(bk), view(bk), view(bv), view(bv), view(bv)]
    if prev is not None:
        in_specs += [o_spec, l_spec]
        args += [view(prev[0]), view(prev[1])]
    kern = functools.partial(_dil_kernel, qb=qb, nh=B_HEADS_PER_GROUP, has_prev=prev is not None,
                             last=last, seq_len=ln)
    o, l = pl.pallas_call(
        kern,
        out_shape=(jax.ShapeDtypeStruct((bsz, ln, dil * gw), out_dtype),
                   jax.ShapeDtypeStruct((bsz, ln, dil * LANES), F32)),
        grid=(bsz, dil, nqb),
        in_specs=in_specs,
        out_specs=(o_spec, l_spec),
        scratch_shapes=[pltpu.VMEM((qb + 2 * _BW, gw), BF16), pltpu.VMEM((qb + 2 * _BW, gw), BF16)],
        compiler_params=_cparams(("parallel", "parallel", "parallel")),
        name=f"dilated_attn_d{dil}",
    )(*args)
    return o.reshape(bsz, seq, gw), l.reshape(bsz, seq, LANES)


def _qknorm_rope_kernel(x_ref, wq_ref, wk_ref, cos_ref, sin_ref, o_ref, *, cw, n_q_blocks):
    j = pl.program_id(1)
    is_q = j < n_q_blocks
    w = jnp.where(is_q, wq_ref[...], wk_ref[...])
    scale = jnp.where(is_q, HEAD_DIM ** -0.5, 1.0).astype(F32)
    cos = cos_ref[...]
    sin = sin_ref[...]
    for hh in range(cw // HEAD_DIM):
        sl = slice(hh * HEAD_DIM, (hh + 1) * HEAD_DIM)
        x = x_ref[:, sl].astype(F32)
        ms = jnp.mean(x * x, axis=-1, keepdims=True)
        y = x * lax.rsqrt(ms + NORM_EPS) * w
        y = y * cos + _rotate_pairs(y, HEAD_DIM // 4) * sin
        o_ref[:, sl] = (y * scale).astype(o_ref.dtype)


def _qknorm_rope(qkv, q_norm, k_norm, cos_t, sin_t, *, q_width, qk_width, seq, tr=512, cw=512):
    t = qkv.shape[0]
    tr = min(tr, seq)
    nsb = seq // tr
    return pl.pallas_call(
        functools.partial(_qknorm_rope_kernel, cw=cw, n_q_blocks=q_width // cw),
        out_shape=jax.ShapeDtypeStruct((t, qk_width), BF16),
        grid=(t // tr, qk_width // cw),
        in_specs=[pl.BlockSpec((tr, cw), lambda i, j: (i, j)),
                  pl.BlockSpec((1, HEAD_DIM), lambda i, j: (0, 0)),
                  pl.BlockSpec((1, HEAD_DIM), lambda i, j: (0, 0)),
                  pl.BlockSpec((tr, HEAD_DIM), lambda i, j: (i % nsb, 0)),
                  pl.BlockSpec((tr, HEAD_DIM), lambda i, j: (i % nsb, 0))],
        out_specs=pl.BlockSpec((tr, cw), lambda i, j: (i, j)),
        compiler_params=_cparams(("parallel", "parallel")),
        name="qk_norm_axial_rope",
    )(qkv, q_norm.reshape(1, HEAD_DIM).astype(F32), k_norm.reshape(1, HEAD_DIM).astype(F32), cos_t, sin_t)


def _flash_kernel(q_ref, k_ref, v_ref, o_ref, *, tq, tk, rep, seq):
    nkv = seq // tk
    for r in range(rep):
        sl = slice(r * HEAD_DIM, (r + 1) * HEAD_DIM)
        q = q_ref[0, :, sl]

        def body(t, carry):
            m, l, acc = carry
            off = pl.multiple_of(t * tk, tk)
            kt = k_ref[0, pl.ds(off, tk), :]
            vt = v_ref[0, pl.ds(off, tk), :]
            s = lax.dot_general(q, kt, (((1,), (1,)), ((), ())), preferred_element_type=F32)
            m_new = jnp.maximum(m, jnp.max(s, axis=-1, keepdims=True))
            alpha = jnp.exp(m - m_new)
            p = jnp.exp(s - m_new)
            l = alpha * l + jnp.sum(p, axis=-1, keepdims=True)
            acc = alpha * acc + jnp.dot(p.astype(BF16), vt, preferred_element_type=F32)
            return m_new, l, acc

        m0 = jnp.full((tq, 1), -1e30, F32)
        l0 = jnp.zeros((tq, 1), F32)
        a0 = jnp.zeros((tq, HEAD_DIM), F32)
        m, l, acc = lax.fori_loop(0, nkv, body, (m0, l0, a0))
        o_ref[0, :, sl] = (acc / l).astype(o_ref.dtype)


def _flash_attention(qk, qkv, *, bsz, seq, tq=512, tk=512):
    qw = C_Q_HEADS * HEAD_DIM
    tq, tk = min(tq, seq), min(tk, seq)
    gq = C_REP * HEAD_DIM
    qk3 = qk.reshape(bsz, seq, qk.shape[-1])
    qkv3 = qkv.reshape(bsz, seq, qkv.shape[-1])
    k_blk0 = qw // HEAD_DIM
    v_blk0 = (qw + C_KV_HEADS * HEAD_DIM) // HEAD_DIM
    out = pl.pallas_call(
        functools.partial(_flash_kernel, tq=tq, tk=tk, rep=C_REP, seq=seq),
        out_shape=jax.ShapeDtypeStruct((bsz, seq, qw), BF16),
        grid=(bsz, C_KV_HEADS, seq // tq),
        in_specs=[pl.BlockSpec((1, tq, gq), lambda b, g, i: (b, i, g)),
                  pl.BlockSpec((1, seq, HEAD_DIM), lambda b, g, i: (b, 0, k_blk0 + g)),
                  pl.BlockSpec((1, seq, HEAD_DIM), lambda b, g, i: (b, 0, v_blk0 + g))],
        out_specs=pl.BlockSpec((1, tq, gq), lambda b, g, i: (b, i, g)),
        compiler_params=_cparams(("parallel", "parallel", "parallel")),
        name="gqa_flash_attention",
    )(qk3, qk3, qkv3)
    return out.reshape(bsz * seq, qw)


def _rope_cos_sin(pos, dim, theta):
    inv = 1.0 / (theta ** (jnp.arange(0, dim, 2, dtype=F32) / dim))
    ang = pos.astype(F32)[:, None] * inv[None, :]
    return jnp.cos(ang), jnp.sin(ang)


def _tables(seq):
    tok = jnp.arange(seq)
    cos_b, sin_b = _rope_cos_sin(tok, B_ROT_DIM, B_ROPE_THETA)
    ones = jnp.ones((seq, HEAD_DIM - B_ROT_DIM), F32)
    cb = jnp.concatenate([cos_b, cos_b, ones], axis=-1)
    sb = jnp.concatenate([-sin_b, sin_b, jnp.zeros_like(ones)], axis=-1)
    cos_r, sin_r = _rope_cos_sin(tok // GRID_W, HEAD_DIM // 2, C_ROPE_THETA)
    cos_c, sin_c = _rope_cos_sin(tok % GRID_W, HEAD_DIM // 2, C_ROPE_THETA)
    cc = jnp.concatenate([cos_r, cos_r, cos_c, cos_c], axis=-1)
    sc = jnp.concatenate([-sin_r, sin_r, -sin_c, sin_c], axis=-1)
    return cb, sb, cc, sc


def _ffn(x2d, norm_w, w_gate, w_up, w_down):
    h = _rmsnorm(x2d, norm_w, BF16)
    act = _gateup(h, w_gate.astype(BF16), w_up.astype(BF16), tm=2048, tn=256)
    dff = act.shape[1]
    tk = dff // 2 if (dff // 2) % LANES == 0 else dff
    return _matmul(act, w_down.astype(BF16), tm=1024, tn=512, tk=tk, out_dtype=F32, residual=x2d)


def _mixer_ab(x2d, norm_w, w_in, conv_w, a_log, dt_bias, out_norm, w_out, cb, sb, *, bsz, seq):
    t, dm = x2d.shape
    n_heads = a_log.shape[-1]
    a_qk = n_heads * HEAD_DIM
    a_v = n_heads * HEAD_DIM
    a_qkv = 2 * a_qk + a_v
    ngate = 2 * n_heads
    b_w = (w_in.shape[1] - a_qkv - a_v - 2 * ngate) // 3
    h = _rmsnorm(x2d, norm_w, BF16)

    w_a = w_in[:, :a_qkv + a_v].astype(BF16)
    w_beta = w_in[:, a_qkv + a_v:a_qkv + a_v + ngate].reshape(dm, 2, n_heads)
    w_alpha = w_in[:, a_qkv + a_v + ngate:a_qkv + a_v + 2 * ngate].reshape(dm, 2, n_heads)
    w_g = jnp.concatenate([w_beta, w_alpha, jnp.zeros((dm, 2, LANES - 2 * n_heads), F32)], axis=-1)
    w_g = w_g.reshape(dm, 2 * LANES).astype(BF16)
    w_b = w_in[:, a_qkv + a_v + 2 * ngate:].astype(BF16)

    proj_a = _matmul(h, w_a, tm=1024, tn=1024, out_dtype=BF16)
    gates = _matmul(h, w_g, tm=1024, tn=2 * LANES, out_dtype=F32)
    proj_b = _matmul(h, w_b, tm=1024, tn=1024, out_dtype=BF16)

    qkv = _conv_silu_l2(proj_a.reshape(bsz, seq, -1), conv_w, qk_width=2 * a_qk, q_width=a_qk, n_ch=a_qkv)
    pad = jnp.zeros((2, LANES - 2 * n_heads), F32)
    alog_row = jnp.concatenate([jnp.zeros((2, n_heads), F32), a_log.astype(F32), pad], axis=-1).reshape(2, 1, LANES)
    dtb_row = jnp.concatenate([jnp.zeros((2, n_heads), F32), dt_bias.astype(F32), pad], axis=-1).reshape(2, 1, LANES)
    o2 = _delta_scan(qkv, gates.reshape(bsz, seq, 2 * LANES), alog_row, dtb_row, n_heads=n_heads)
    o_a = _delta_out(o2.reshape(2, t, a_v), proj_a, out_norm, z_col_block=a_qkv // 512)

    qk_rot = _rope_b(proj_b, cb, sb, width=2 * b_w, seq=seq)
    bq = qk_rot.reshape(bsz, seq, 2 * b_w)
    bv = proj_b.reshape(bsz, seq, 3 * b_w)
    prev = None
    ngroups = len(B_PATTERNS)
    for gi, (_, dil) in enumerate(B_PATTERNS):
        prev = _dilated_group_call(bq, bv, gi, dil, prev, b_w=b_w, last=gi == ngroups - 1)
    o_b = prev[0].reshape(t, -1)

    cat = jnp.concatenate([o_a, o_b], axis=-1)
    return _matmul(cat, w_out.astype(BF16), tm=1024, tn=1024, out_dtype=F32, residual=x2d)


def _dilated_group_call(qk_rot, proj_b, gi, dil, prev, *, b_w, last):
    bsz, seq, _ = qk_rot.shape
    gw = B_HEADS_PER_GROUP * HEAD_DIM
    ncol_qk = 2 * b_w // gw
    ncol_v = 3 * b_w // gw
    nb = b_w // gw
    ln = seq // dil
    qb = min(512, ln)
    nqb = ln // qb
    hb = qb // _BW
    nhalo = ln // _BW
    vqk = qk_rot.reshape(bsz, ln, dil * 2 * b_w)
    vv = proj_b.reshape(bsz, ln, dil * 3 * b_w)
    lo = lambda i: jnp.maximum(i * hb - 1, 0)
    hi = lambda i: jnp.minimum((i + 1) * hb, nhalo - 1)
    q_spec = pl.BlockSpec((1, qb, gw), lambda b, r, i: (b, i, r * ncol_qk + gi))
    kc = pl.BlockSpec((1, qb, gw), lambda b, r, i: (b, i, r * ncol_qk + nb + gi))
    kp = pl.BlockSpec((1, _BW, gw), lambda b, r, i: (b, lo(i), r * ncol_qk + nb + gi))
    kn = pl.BlockSpec((1, _BW, gw), lambda b, r, i: (b, hi(i), r * ncol_qk + nb + gi))
    vc = pl.BlockSpec((1, qb, gw), lambda b, r, i: (b, i, r * ncol_v + 2 * nb + gi))
    vp = pl.BlockSpec((1, _BW, gw), lambda b, r, i: (b, lo(i), r * ncol_v + 2 * nb + gi))
    vn = pl.BlockSpec((1, _BW, gw), lambda b, r, i: (b, hi(i), r * ncol_v + 2 * nb + gi))
    o_spec = pl.BlockSpec((1, qb, gw), lambda b, r, i: (b, i, r))
    l_spec = pl.BlockSpec((1, qb, LANES), lambda b, r, i: (b, i, r))
    in_specs = [q_spec, kc, kp, kn, vc, vp, vn]
    args = [vqk, vqk, vqk, vqk, vv, vv, vv]
    if prev is not None:
        in_specs += [o_spec, l_spec]
        args += [prev[0].reshape(bsz, ln, dil * gw), prev[1].reshape(bsz, ln, dil * LANES)]
    out_dtype = BF16 if last else F32
    kern = functools.partial(_dil_kernel, qb=qb, nh=B_HEADS_PER_GROUP, has_prev=prev is not None,
                             last=last, seq_len=ln)
    o, l = pl.pallas_call(
        kern,
        out_shape=(jax.ShapeDtypeStruct((bsz, ln, dil * gw), out_dtype),
                   jax.ShapeDtypeStruct((bsz, ln, dil * LANES), F32)),
        grid=(bsz, dil, nqb),
        in_specs=in_specs,
        out_specs=(o_spec, l_spec),
        scratch_shapes=[pltpu.VMEM((qb + 2 * _BW, gw), BF16), pltpu.VMEM((qb + 2 * _BW, gw), BF16)],
        compiler_params=_cparams(("parallel", "parallel", "parallel")),
        name=f"dilated_attn_d{dil}",
    )(*args)
    return o.reshape(bsz, seq, gw), l.reshape(bsz, seq, LANES)


def _mixer_c(x2d, norm_w, w_qkv, q_norm, k_norm, w_out, cc, sc, *, bsz, seq):
    h = _rmsnorm(x2d, norm_w, BF16)
    qkv = _matmul(h, w_qkv.astype(BF16), tm=1024, tn=1024, out_dtype=BF16)
    qw = C_Q_HEADS * HEAD_DIM
    kw = C_KV_HEADS * HEAD_DIM
    qk = _qknorm_rope(qkv, q_norm, k_norm, cc, sc, q_width=qw, qk_width=qw + kw, seq=seq)
    o = _flash_attention(qk, qkv, bsz=bsz, seq=seq)
    return _matmul(o, w_out.astype(BF16), tm=1024, tn=1024, out_dtype=F32, residual=x2d)


def kernel(x, norm_mix, norm_ffn, norm_final, ab_w_in, ab_conv_w, ab_a_log, ab_dt_bias, ab_out_norm, ab_w_out,
           c_w_qkv, c_q_norm, c_k_norm, c_w_out, ffn_w_gate, ffn_w_up, ffn_w_down):
    bsz, seq, dm = x.shape
    depth = norm_mix.shape[0]
    cb, sb, cc, sc = _tables(seq)
    x2d = x.reshape(bsz * seq, dm)
    for layer in range(depth):
        j = layer // 2
        if layer % 2 == 0:
            x2d = _mixer_ab(x2d, norm_mix[layer], ab_w_in[j], ab_conv_w[j], ab_a_log[j], ab_dt_bias[j],
                            ab_out_norm[j], ab_w_out[j], cb, sb, bsz=bsz, seq=seq)
        else:
            x2d = _mixer_c(x2d, norm_mix[layer], c_w_qkv[j], c_q_norm[j], c_k_norm[j], c_w_out[j],
                           cc, sc, bsz=bsz, seq=seq)
        x2d = _ffn(x2d, norm_ffn[layer], ffn_w_gate[layer], ffn_w_up[layer], ffn_w_down[layer])
    out = _rmsnorm(x2d, norm_final, x.dtype)
    return out.reshape(bsz, seq, dm)
```

```python
import functools

import jax
import jax.numpy as jnp
from jax import lax
from jax.experimental import pallas as pl
from jax.experimental.pallas import tpu as pltpu

HEAD_DIM = 128
NORM_EPS = 1e-6
L2_EPS = 1e-6
GRID_W = 64
A_CONV_W = 5
A_CHUNK = 64
B_PATTERNS = ((128, 1), (512, 4), (2048, 16))
B_HEADS_PER_GROUP = 8
B_ROT_DIM = HEAD_DIM // 4
B_ROPE_THETA = 500000.0
C_Q_HEADS = 32
C_KV_HEADS = 8
C_REP = C_Q_HEADS // C_KV_HEADS
C_ROPE_THETA = 10000.0
LOG2E = 1.4426950408889634

LANES = 128
VMEM_LIMIT = 56 * 1024 * 1024

BF16 = jnp.bfloat16
F32 = jnp.float32
HI = lax.Precision.HIGHEST


def _cparams(sem, vmem=VMEM_LIMIT):
    return pltpu.CompilerParams(dimension_semantics=sem, vmem_limit_bytes=vmem)


def _rmsnorm_kernel(x_ref, w_ref, o_ref):
    x = x_ref[...].astype(F32)
    ms = jnp.mean(x * x, axis=-1, keepdims=True)
    o_ref[...] = (x * lax.rsqrt(ms + NORM_EPS) * w_ref[...]).astype(o_ref.dtype)


def _rmsnorm(x2d, w, out_dtype, tr=256):
    m, d = x2d.shape
    return pl.pallas_call(
        _rmsnorm_kernel,
        out_shape=jax.ShapeDtypeStruct((m, d), out_dtype),
        grid=(m // tr,),
        in_specs=[pl.BlockSpec((tr, d), lambda i: (i, 0)),
                  pl.BlockSpec((1, d), lambda i: (0, 0))],
        out_specs=pl.BlockSpec((tr, d), lambda i: (i, 0)),
        compiler_params=_cparams(("parallel",)),
        name="rmsnorm",
    )(x2d, w.reshape(1, d).astype(F32))


def _wmm_kernel(*refs, ksizes, has_res):
    na = len(ksizes)
    a_refs = refs[:na]
    w_ref = refs[na]
    r_ref = refs[na + 1] if has_res else None
    o_ref = refs[na + 1 + int(has_res)]
    w16 = refs[na + 2 + int(has_res)]

    @pl.when(pl.program_id(1) == 0)
    def _():
        w16[...] = w_ref[...].astype(BF16)

    off = 0
    acc = None
    for a_ref, ks in zip(a_refs, ksizes):
        part = jnp.dot(a_ref[...], w16[off:off + ks, :], preferred_element_type=F32)
        acc = part if acc is None else acc + part
        off += ks
    if has_res:
        acc = acc + r_ref[...]
    o_ref[...] = acc.astype(o_ref.dtype)


def _wmatmul(a_ops, w, *, w_lead, w_row_blk=0, w_col_blk0=0, n, tm, tn, out_dtype, residual=None):
    m = a_ops[0][0].shape[0]
    ksizes = tuple(k for _, _, k in a_ops)
    ktot = sum(ksizes)
    tm, tn = min(tm, m), min(tn, n)
    assert m % tm == 0 and n % tn == 0
    in_specs = [pl.BlockSpec((tm, ks), functools.partial(lambda j, i, blk: (i, blk), blk=blk))
                for _, blk, ks in a_ops]
    in_specs.append(pl.BlockSpec((None, ktot, tn), lambda j, i: (w_lead, w_row_blk, w_col_blk0 + j)))
    args = [a for a, _, _ in a_ops] + [w]
    if residual is not None:
        in_specs.append(pl.BlockSpec((tm, tn), lambda j, i: (i, j)))
        args.append(residual)
    return pl.pallas_call(
        functools.partial(_wmm_kernel, ksizes=ksizes, has_res=residual is not None),
        out_shape=jax.ShapeDtypeStruct((m, n), out_dtype),
        grid=(n // tn, m // tm),
        in_specs=in_specs,
        out_specs=pl.BlockSpec((tm, tn), lambda j, i: (i, j)),
        scratch_shapes=[pltpu.VMEM((ktot, tn), BF16)],
        compiler_params=_cparams(("arbitrary", "arbitrary")),
        name="proj_matmul",
    )(*args)


def _gateup_kernel(h_ref, wg_ref, wu_ref, o_ref, wg16, wu16):
    @pl.when(pl.program_id(1) == 0)
    def _():
        wg16[...] = wg_ref[...].astype(BF16)
        wu16[...] = wu_ref[...].astype(BF16)

    h = h_ref[...]
    g = jnp.dot(h, wg16[...], preferred_element_type=F32)
    u = jnp.dot(h, wu16[...], preferred_element_type=F32)
    o_ref[...] = (g * jax.nn.sigmoid(g) * u).astype(o_ref.dtype)


def _gateup(h, w_gate, w_up, layer, *, tm, tn):
    m, kdim = h.shape
    n = w_gate.shape[-1]
    tm, tn = min(tm, m), min(tn, n)
    assert m % tm == 0 and n % tn == 0
    wspec = pl.BlockSpec((None, kdim, tn), lambda j, i: (layer, 0, j))
    return pl.pallas_call(
        _gateup_kernel,
        out_shape=jax.ShapeDtypeStruct((m, n), BF16),
        grid=(n // tn, m // tm),
        in_specs=[pl.BlockSpec((tm, kdim), lambda j, i: (i, 0)), wspec, wspec],
        out_specs=pl.BlockSpec((tm, tn), lambda j, i: (i, j)),
        scratch_shapes=[pltpu.VMEM((kdim, tn), BF16), pltpu.VMEM((kdim, tn), BF16)],
        compiler_params=_cparams(("arbitrary", "arbitrary")),
        name="ffn_gate_up",
    )(h, w_gate, w_up)


_CONV_HALO = 16


def _conv_kernel(prev_ref, cur_ref, next_ref, w_ref, o_ref, ext_ref, *, ts, cw, n_qk_blocks, n_q_blocks):
    i = pl.program_id(1)
    nblk = pl.num_programs(1)
    j = pl.program_id(2)
    pad = A_CONV_W // 2
    prev = prev_ref[0].astype(F32)
    nxt = next_ref[0].astype(F32)
    ext_ref[0:_CONV_HALO, :] = jnp.where(i > 0, prev, 0.0)
    ext_ref[_CONV_HALO:_CONV_HALO + ts, :] = cur_ref[0].astype(F32)
    ext_ref[_CONV_HALO + ts:, :] = jnp.where(i < nblk - 1, nxt, 0.0)
    w = w_ref[...]
    y = jnp.zeros((ts, cw), F32)
    for t in range(A_CONV_W):
        y = y + ext_ref[_CONV_HALO - pad + t:_CONV_HALO - pad + t + ts, :] * w[t:t + 1, :]
    y = y * jax.nn.sigmoid(y)
    is_qk = j < n_qk_blocks
    scale = jnp.where(j < n_q_blocks, HEAD_DIM ** -0.5, 1.0).astype(F32)
    for hh in range(cw // HEAD_DIM):
        yh = y[:, hh * HEAD_DIM:(hh + 1) * HEAD_DIM]
        ss = jnp.sum(yh * yh, axis=-1, keepdims=True)
        yn = yh * (lax.rsqrt(ss + L2_EPS) * scale)
        o_ref[0, :, hh * HEAD_DIM:(hh + 1) * HEAD_DIM] = jnp.where(is_qk, yn, yh).astype(o_ref.dtype)


def _conv_silu_l2(proj_a, conv_w, *, qk_width, q_width, n_ch, ts=512, cw=512):
    bsz, seq, _ = proj_a.shape
    ts = min(ts, seq)
    hb = ts // _CONV_HALO
    nhalo = seq // _CONV_HALO
    w8 = jnp.zeros((8, n_ch), F32).at[:A_CONV_W].set(conv_w.astype(F32))
    kern = functools.partial(_conv_kernel, ts=ts, cw=cw, n_qk_blocks=qk_width // cw, n_q_blocks=q_width // cw)
    return pl.pallas_call(
        kern,
        out_shape=jax.ShapeDtypeStruct((bsz, seq, n_ch), BF16),
        grid=(bsz, seq // ts, n_ch // cw),
        in_specs=[
            pl.BlockSpec((1, _CONV_HALO, cw), lambda b, i, j: (b, jnp.maximum(i * hb - 1, 0), j)),
            pl.BlockSpec((1, ts, cw), lambda b, i, j: (b, i, j)),
            pl.BlockSpec((1, _CONV_HALO, cw), lambda b, i, j: (b, jnp.minimum((i + 1) * hb, nhalo - 1), j)),
            pl.BlockSpec((8, cw), lambda b, i, j: (0, j)),
        ],
        out_specs=pl.BlockSpec((1, ts, cw), lambda b, i, j: (b, i, j)),
        scratch_shapes=[pltpu.VMEM((ts + 2 * _CONV_HALO, cw), F32)],
        compiler_params=_cparams(("parallel", "parallel", "parallel")),
        name="conv_silu_l2norm",
    )(proj_a, proj_a, proj_a, w8)


def _split_bf16(x):
    hi = x.astype(BF16)
    lo = (x - hi.astype(F32)).astype(BF16)
    return hi, lo


def _dot3(a, b):
    a_hi, a_lo = a
    b_hi, b_lo = b
    return (jnp.dot(a_hi, b_hi, preferred_element_type=F32)
            + (jnp.dot(a_hi, b_lo, preferred_element_type=F32) + jnp.dot(a_lo, b_hi, preferred_element_type=F32)))


def _delta_kernel(q_ref, k_ref, v_ref, gt_ref, alog_ref, dtb_ref, o_ref, state_ref, *, hg, n_heads):
    c = A_CHUNK
    d = pl.program_id(0) % 2
    grp = pl.program_id(1)
    n = pl.program_id(2)

    @pl.when(n == 0)
    def _():
        state_ref[...] = jnp.zeros_like(state_ref)

    row = lax.broadcasted_iota(jnp.int32, (c, c), 0)
    col = lax.broadcasted_iota(jnp.int32, (c, c), 1)
    ahead = jnp.where(d == 0, row - col, col - row)
    incl = ahead >= 0
    strict = ahead > 0
    tri = jnp.where(incl, 1.0, 0.0).astype(F32)
    eye = jnp.where(row == col, 1.0, 0.0).astype(F32)

    graw = gt_ref[0]
    beta_all = jax.nn.sigmoid(graw)
    z = graw + dtb_ref[0]
    softplus = jnp.maximum(z, 0.0) + jnp.log(1.0 + jnp.exp(-jnp.abs(z)))
    g_all = -jnp.exp(alog_ref[0]) * softplus
    gc_all = jnp.dot(tri, g_all, precision=HI, preferred_element_type=F32)
    gct_all = lax.dot_general(g_all, tri, (((0,), (1,)), ((), ())), precision=HI,
                              preferred_element_type=F32)
    gtot_all = jnp.sum(g_all, axis=0, keepdims=True)
    lane = lax.broadcasted_iota(jnp.int32, (c, LANES), 1)
    subl = lax.broadcasted_iota(jnp.int32, (LANES, c), 0)
    contract_last = (((1,), (1,)), ((), ()))
    contract_first = (((0,), (0,)), ((), ()))

    heads = range(hg)
    sls = [slice(hh * HEAD_DIM, (hh + 1) * HEAD_DIM) for hh in heads]
    q16 = [q_ref[0, :, sl] for sl in sls]
    k16 = [k_ref[0, :, sl] for sl in sls]
    gcol, beta, gtot, decay = [], [], [], []
    for hh in heads:
        a_lane = n_heads + grp * hg + hh
        sel_a = lane == a_lane
        sel_b = lane == (grp * hg + hh)
        gcol.append(jnp.sum(jnp.where(sel_a, gc_all, 0.0), axis=1, keepdims=True))
        beta.append(jnp.sum(jnp.where(sel_b, beta_all, 0.0), axis=1, keepdims=True))
        gtot.append(jnp.sum(jnp.where(sel_a[0:1], gtot_all, 0.0), axis=1, keepdims=True))
        grow = jnp.sum(jnp.where(subl == a_lane, gct_all, 0.0), axis=0, keepdims=True)
        diff = gcol[hh] - grow
        decay.append(jnp.where(incl, jnp.exp(jnp.where(incl, diff, 0.0)), 0.0))
    kb = [k16[hh].astype(F32) * beta[hh] for hh in heads]
    kq = [lax.dot_general(jnp.concatenate([kb[hh].astype(BF16), q16[hh]], axis=0), k16[hh], contract_last,
                          preferred_element_type=F32) for hh in heads]
    attn = [(kq[hh][c:] * decay[hh]).astype(BF16) for hh in heads]
    x = [-jnp.where(strict, kq[hh][:c] * decay[hh], 0.0) for hh in heads]
    t_mat = [eye + x[hh] for hh in heads]
    p = 2
    while p < c:
        xs = [_split_bf16(x[hh]) for hh in heads]
        x = [_dot3(xs[hh], xs[hh]) for hh in heads]
        t_mat = [t_mat[hh] + _dot3(_split_bf16(t_mat[hh]), _split_bf16(x[hh])) for hh in heads]
        p *= 2
    eg = [jnp.exp(gcol[hh]) for hh in heads]
    rhs = [jnp.concatenate([v_ref[0, :, sls[hh]].astype(F32) * beta[hh], kb[hh] * eg[hh]], axis=1).astype(BF16)
           for hh in heads]
    uw = [jnp.dot(t_mat[hh].astype(BF16), rhs[hh], preferred_element_type=F32) for hh in heads]
    state = [state_ref[hh] for hh in heads]
    ws = [jnp.dot(jnp.concatenate([uw[hh][:, HEAD_DIM:], q16[hh].astype(F32) * eg[hh]], axis=0).astype(BF16),
                  state[hh].astype(BF16), preferred_element_type=F32) for hh in heads]
    vn16 = [(uw[hh][:, :HEAD_DIM] - ws[hh][:c]).astype(BF16) for hh in heads]
    for hh in heads:
        o = ws[hh][c:] + jnp.dot(attn[hh], vn16[hh], preferred_element_type=F32)
        o_ref[0, 0, :, sls[hh]] = o.astype(o_ref.dtype)
    for hh in heads:
        kd = (k16[hh].astype(F32) * jnp.exp(gtot[hh] - gcol[hh])).astype(BF16)
        state_ref[hh] = state[hh] * jnp.exp(gtot[hh]) + lax.dot_general(kd, vn16[hh], contract_first,
                                                                       preferred_element_type=F32)


def _delta_scan(qkv, gates, alog_row, dtb_row, *, n_heads, hg=16):
    bsz, seq, _ = qkv.shape
    c = A_CHUNK
    nchunk = seq // c
    ngrp = n_heads // hg
    wblk = hg * HEAD_DIM

    def cidx(bd, n):
        return jnp.where(bd % 2 == 0, n, nchunk - 1 - n)

    kern = functools.partial(_delta_kernel, hg=hg, n_heads=n_heads)
    return pl.pallas_call(
        kern,
        out_shape=jax.ShapeDtypeStruct((2, bsz, seq, n_heads * HEAD_DIM), F32),
        grid=(bsz * 2, ngrp, nchunk),
        in_specs=[
            pl.BlockSpec((1, c, wblk), lambda bd, g, n: (bd // 2, cidx(bd, n), g)),
            pl.BlockSpec((1, c, wblk), lambda bd, g, n: (bd // 2, cidx(bd, n), ngrp + g)),
            pl.BlockSpec((1, c, wblk), lambda bd, g, n: (bd // 2, cidx(bd, n), 2 * ngrp + g)),
            pl.BlockSpec((1, c, LANES), lambda bd, g, n: (bd // 2, cidx(bd, n), bd % 2)),
            pl.BlockSpec((1, 1, LANES), lambda bd, g, n: (bd % 2, 0, 0)),
            pl.BlockSpec((1, 1, LANES), lambda bd, g, n: (bd % 2, 0, 0)),
        ],
        out_specs=pl.BlockSpec((1, 1, c, wblk), lambda bd, g, n: (bd % 2, bd // 2, cidx(bd, n), g)),
        scratch_shapes=[pltpu.VMEM((hg, HEAD_DIM, HEAD_DIM), F32)],
        compiler_params=_cparams(("parallel", "parallel", "arbitrary")),
        name="gated_delta_scan",
    )(qkv, qkv, qkv, gates, alog_row, dtb_row)


def _delta_out_kernel(of_ref, ob_ref, z_ref, w_ref, o_ref, *, cw):
    o = of_ref[0] + ob_ref[0]
    z = z_ref[...].astype(F32)
    w = w_ref[...]
    for hh in range(cw // HEAD_DIM):
        sl = slice(hh * HEAD_DIM, (hh + 1) * HEAD_DIM)
        oh = o[:, sl]
        ms = jnp.mean(oh * oh, axis=-1, keepdims=True)
        zh = z[:, sl]
        o_ref[:, sl] = (oh * lax.rsqrt(ms + NORM_EPS) * w * (zh * jax.nn.sigmoid(zh))).astype(o_ref.dtype)


def _delta_out(o2, proj_a, out_norm, *, z_col_block, tr=512, cw=512):
    _, t, vdim = o2.shape
    tr = min(tr, t)
    return pl.pallas_call(
        functools.partial(_delta_out_kernel, cw=cw),
        out_shape=jax.ShapeDtypeStruct((t, vdim), BF16),
        grid=(t // tr, vdim // cw),
        in_specs=[
            pl.BlockSpec((1, tr, cw), lambda i, j: (0, i, j)),
            pl.BlockSpec((1, tr, cw), lambda i, j: (1, i, j)),
            pl.BlockSpec((tr, cw), lambda i, j: (i, z_col_block + j)),
            pl.BlockSpec((1, HEAD_DIM), lambda i, j: (0, 0)),
        ],
        out_specs=pl.BlockSpec((tr, cw), lambda i, j: (i, j)),
        compiler_params=_cparams(("parallel", "parallel")),
        name="delta_out_norm_gate",
    )(o2, o2, proj_a, out_norm.reshape(1, HEAD_DIM).astype(F32))


def _rotate_pairs(x, half):
    lane = lax.broadcasted_iota(jnp.int32, x.shape, 1)
    first = (lane % (2 * half)) < half
    return jnp.where(first, pltpu.roll(x, LANES - half, 1), pltpu.roll(x, half, 1))


def _deinterleave_matrix(tr, dil, transpose=False):
    per = tr // dil
    i_out = lax.broadcasted_iota(jnp.int32, (tr, tr), 1 if transpose else 0)
    i_in = lax.broadcasted_iota(jnp.int32, (tr, tr), 0 if transpose else 1)
    src = (i_out % per) * dil + i_out // per
    return jnp.where(i_in == src, 1.0, 0.0).astype(BF16)


def _rope_deint_kernel(q_ref, k_ref, v_ref, cos_ref, sin_ref, o_ref, *, tr, dil, gw):
    cos = cos_ref[...]
    sin = sin_ref[...]
    per = tr // dil
    perm = _deinterleave_matrix(tr, dil) if dil > 1 else None

    def emit(y16, col0):
        if dil > 1:
            y16 = jnp.dot(perm, y16, preferred_element_type=F32).astype(BF16)
        for r in range(dil):
            o_ref[0, r, :, col0:col0 + HEAD_DIM] = y16[r * per:(r + 1) * per, :]

    for part, ref in enumerate((q_ref, k_ref)):
        for hh in range(gw // HEAD_DIM):
            sl = slice(hh * HEAD_DIM, (hh + 1) * HEAD_DIM)
            x = ref[:, sl].astype(F32)
            y = x * cos + _rotate_pairs(x, B_ROT_DIM // 2) * sin
            emit(y.astype(BF16), part * gw + hh * HEAD_DIM)
    for hh in range(gw // HEAD_DIM):
        emit(v_ref[:, hh * HEAD_DIM:(hh + 1) * HEAD_DIM], 2 * gw + hh * HEAD_DIM)


def _rope_deint(proj_b, cos_t, sin_t, gi, dil, *, bsz, seq, n_groups, tr=256):
    gw = B_HEADS_PER_GROUP * HEAD_DIM
    nsb = seq // tr
    per = tr // dil
    return pl.pallas_call(
        functools.partial(_rope_deint_kernel, tr=tr, dil=dil, gw=gw),
        out_shape=jax.ShapeDtypeStruct((bsz, dil, seq // dil, 3 * gw), BF16),
        grid=(bsz, nsb),
        in_specs=[pl.BlockSpec((tr, gw), lambda b, i: (b * nsb + i, gi)),
                  pl.BlockSpec((tr, gw), lambda b, i: (b * nsb + i, n_groups + gi)),
                  pl.BlockSpec((tr, gw), lambda b, i: (b * nsb + i, 2 * n_groups + gi)),
                  pl.BlockSpec((tr, HEAD_DIM), lambda b, i: (i, 0)),
                  pl.BlockSpec((tr, HEAD_DIM), lambda b, i: (i, 0))],
        out_specs=pl.BlockSpec((1, dil, per, 3 * gw), lambda b, i: (b, 0, i, 0)),
        compiler_params=_cparams(("parallel", "parallel")),
        name=f"rope_deinterleave_d{dil}",
    )(proj_b, proj_b, proj_b, cos_t, sin_t)


_BW = 64
_BQ = 128


def _dil_kernel(q_ref, kc_ref, kp_ref, kn_ref, vc_ref, vp_ref, vn_ref, o_ref, l_ref, kext, vext,
                *, qb, nh, seq_len):
    i = pl.program_id(2)
    base = i * qb
    kext[0:_BW, :] = kp_ref[0, 0]
    kext[_BW:_BW + qb, :] = kc_ref[0, 0]
    kext[_BW + qb:, :] = kn_ref[0, 0]
    vext[0:_BW, :] = vp_ref[0, 0]
    vext[_BW:_BW + qb, :] = vc_ref[0, 0]
    vext[_BW + qb:, :] = vn_ref[0, 0]
    scale = HEAD_DIM ** -0.5
    nkeys = _BQ + 2 * _BW
    qi = lax.broadcasted_iota(jnp.int32, (_BQ, nkeys), 0)
    kj = lax.broadcasted_iota(jnp.int32, (_BQ, nkeys), 1)
    lane = lax.broadcasted_iota(jnp.int32, (_BQ, LANES), 1)
    for sb in range(qb // _BQ):
        qpos = base + sb * _BQ + qi
        kpos = base + sb * _BQ - _BW + kj
        valid = (jnp.abs(kpos - qpos) <= _BW) & (kpos >= 0) & (kpos < seq_len)
        lse_blk = jnp.zeros((_BQ, LANES), F32)
        for hh in range(nh):
            sl = slice(hh * HEAD_DIM, (hh + 1) * HEAD_DIM)
            q = q_ref[0, 0, sb * _BQ:(sb + 1) * _BQ, sl]
            kx = kext[sb * _BQ:sb * _BQ + nkeys, sl]
            vx = vext[sb * _BQ:sb * _BQ + nkeys, sl]
            s = lax.dot_general(q, kx, (((1,), (1,)), ((), ())), preferred_element_type=F32) * scale
            s = jnp.where(valid, s, -1e30)
            m = jnp.max(s, axis=-1, keepdims=True)
            e = jnp.where(valid, jnp.exp(s - m), 0.0)
            l = jnp.sum(e, axis=-1, keepdims=True)
            p = (e / l).astype(BF16)
            o = jnp.dot(p, vx, preferred_element_type=F32)
            o_ref[0, 0, sb * _BQ:(sb + 1) * _BQ, sl] = o.astype(o_ref.dtype)
            lse_blk = jnp.where(lane == hh, m + jnp.log(l), lse_blk)
        l_ref[0, 0, sb * _BQ:(sb + 1) * _BQ, :] = lse_blk


def _dilated_group(grp_qkv):
    bsz, dil, ln, _ = grp_qkv.shape
    gw = B_HEADS_PER_GROUP * HEAD_DIM
    qb = min(512, ln)
    hb = qb // _BW
    nhalo = ln // _BW
    lo = lambda i: jnp.maximum(i * hb - 1, 0)
    hi = lambda i: jnp.minimum((i + 1) * hb, nhalo - 1)

    def cur(col):
        return pl.BlockSpec((1, 1, qb, gw), lambda b, r, i: (b, r, i, col))

    def halo(col, f):
        return pl.BlockSpec((1, 1, _BW, gw), lambda b, r, i: (b, r, f(i), col))

    return pl.pallas_call(
        functools.partial(_dil_kernel, qb=qb, nh=B_HEADS_PER_GROUP, seq_len=ln),
        out_shape=(jax.ShapeDtypeStruct((bsz, dil, ln, gw), BF16),
                   jax.ShapeDtypeStruct((bsz, dil, ln, LANES), F32)),
        grid=(bsz, dil, ln // qb),
        in_specs=[cur(0), cur(1), halo(1, lo), halo(1, hi), cur(2), halo(2, lo), halo(2, hi)],
        out_specs=(pl.BlockSpec((1, 1, qb, gw), lambda b, r, i: (b, r, i, 0)),
                   pl.BlockSpec((1, 1, qb, LANES), lambda b, r, i: (b, r, i, 0))),
        scratch_shapes=[pltpu.VMEM((qb + 2 * _BW, gw), BF16), pltpu.VMEM((qb + 2 * _BW, gw), BF16)],
        compiler_params=_cparams(("parallel", "parallel", "parallel")),
        name=f"dilated_attn_d{dil}",
    )(*([grp_qkv] * 7))


def _merge_kernel(*refs, tr, dils, nh):
    ng = len(dils)
    o_refs = refs[:ng]
    l_refs = refs[ng:2 * ng]
    out_ref = refs[2 * ng]
    outs, lses = [], []
    for g, dil in enumerate(dils):
        per = tr // dil
        if dil == 1:
            outs.append(o_refs[g][0, 0].astype(F32))
            lses.append(l_refs[g][0, 0])
            continue
        pt = _deinterleave_matrix(tr, dil, transpose=True)
        y = jnp.concatenate([o_refs[g][0, r] for r in range(dil)], axis=0)
        outs.append(jnp.dot(pt, y, preferred_element_type=F32))
        ls = jnp.concatenate([l_refs[g][0, r] for r in range(dil)], axis=0)
        h1 = ls.astype(BF16)
        r1 = ls - h1.astype(F32)
        h2 = r1.astype(BF16)
        h3 = (r1 - h2.astype(F32)).astype(BF16)
        lses.append(jnp.dot(pt, h1, preferred_element_type=F32)
                    + (jnp.dot(pt, h2, preferred_element_type=F32) + jnp.dot(pt, h3, preferred_element_type=F32)))
    mx = lses[0]
    for g in range(1, ng):
        mx = jnp.maximum(mx, lses[g])
    es = [jnp.exp(l - mx) for l in lses]
    den = es[0]
    for g in range(1, ng):
        den = den + es[g]
    ws = [e / den for e in es]
    lane = lax.broadcasted_iota(jnp.int32, (tr, LANES), 1)
    for hh in range(nh):
        sl = slice(hh * HEAD_DIM, (hh + 1) * HEAD_DIM)
        acc = None
        for g in range(ng):
            wcol = jnp.sum(jnp.where(lane == hh, ws[g], 0.0), axis=1, keepdims=True)
            term = outs[g][:, sl] * wcol
            acc = term if acc is None else acc + term
        out_ref[:, sl] = acc.astype(out_ref.dtype)


def _merge_groups(os_, ls_, *, bsz, seq, tr=256):
    gw = B_HEADS_PER_GROUP * HEAD_DIM
    dils = tuple(o.shape[1] for o in os_)
    nsb = seq // tr
    in_specs = ([pl.BlockSpec((1, d, tr // d, gw), lambda b, i: (b, 0, i, 0)) for d in dils]
                + [pl.BlockSpec((1, d, tr // d, LANES), lambda b, i: (b, 0, i, 0)) for d in dils])
    return pl.pallas_call(
        functools.partial(_merge_kernel, tr=tr, dils=dils, nh=B_HEADS_PER_GROUP),
        out_shape=jax.ShapeDtypeStruct((bsz * seq, gw), BF16),
        grid=(bsz, nsb),
        in_specs=in_specs,
        out_specs=pl.BlockSpec((tr, gw), lambda b, i: (b * nsb + i, 0)),
        compiler_params=_cparams(("parallel", "parallel")),
        name="dilated_merge",
    )(*os_, *ls_)


def _qknorm_rope_kernel(x_ref, wq_ref, wk_ref, cos_ref, sin_ref, o_ref, *, cw, n_q_blocks):
    j = pl.program_id(1)
    is_q = j < n_q_blocks
    w = jnp.where(is_q, wq_ref[...], wk_ref[...])
    scale = jnp.where(is_q, HEAD_DIM ** -0.5 * LOG2E, 1.0).astype(F32)
    cos = cos_ref[...]
    sin = sin_ref[...]
    for hh in range(cw // HEAD_DIM):
        sl = slice(hh * HEAD_DIM, (hh + 1) * HEAD_DIM)
        x = x_ref[:, sl].astype(F32)
        ms = jnp.mean(x * x, axis=-1, keepdims=True)
        y = x * lax.rsqrt(ms + NORM_EPS) * w
        y = y * cos + _rotate_pairs(y, HEAD_DIM // 4) * sin
        o_ref[:, sl] = (y * scale).astype(o_ref.dtype)


def _qknorm_rope(qkv, q_norm, k_norm, cos_t, sin_t, *, q_width, qk_width, seq, tr=512, cw=512):
    t = qkv.shape[0]
    tr = min(tr, seq)
    nsb = seq // tr
    return pl.pallas_call(
        functools.partial(_qknorm_rope_kernel, cw=cw, n_q_blocks=q_width // cw),
        out_shape=jax.ShapeDtypeStruct((t, qk_width), BF16),
        grid=(t // tr, qk_width // cw),
        in_specs=[pl.BlockSpec((tr, cw), lambda i, j: (i, j)),
                  pl.BlockSpec((1, HEAD_DIM), lambda i, j: (0, 0)),
                  pl.BlockSpec((1, HEAD_DIM), lambda i, j: (0, 0)),
                  pl.BlockSpec((tr, HEAD_DIM), lambda i, j: (i % nsb, 0)),
                  pl.BlockSpec((tr, HEAD_DIM), lambda i, j: (i % nsb, 0))],
        out_specs=pl.BlockSpec((tr, cw), lambda i, j: (i, j)),
        compiler_params=_cparams(("parallel", "parallel")),
        name="qk_norm_axial_rope",
    )(qkv, q_norm.reshape(1, HEAD_DIM).astype(F32), k_norm.reshape(1, HEAD_DIM).astype(F32), cos_t, sin_t)


def _flash_kernel(q_ref, k_ref, v_ref, o_ref, qs_ref, v1_ref, m_ref, acc_ref, *, tq, tk, rep, seq):
    @pl.when(pl.program_id(2) == 0)
    def _():
        v1_ref[:, :HEAD_DIM] = v_ref[0]
        v1_ref[:, HEAD_DIM:] = jnp.ones((seq, HEAD_DIM), BF16)

    for r in range(rep):
        qs_ref[r * tq:(r + 1) * tq, :] = q_ref[0, :, r * HEAD_DIM:(r + 1) * HEAD_DIM]
    m_ref[...] = jnp.full(m_ref.shape, -1e30, F32)
    acc_ref[...] = jnp.zeros(acc_ref.shape, F32)

    def body(t, carry):
        off = pl.multiple_of(t * tk, tk)
        kt = k_ref[0, pl.ds(off, tk), :]
        vt = v1_ref[pl.ds(off, tk), :]
        s = [lax.dot_general(qs_ref[r * tq:(r + 1) * tq, :], kt, (((1,), (1,)), ((), ())),
                             preferred_element_type=F32) for r in range(rep)]
        for r in range(rep):
            rows = slice(r * tq, (r + 1) * tq)
            m_old = m_ref[rows, :]
            m_new = jnp.maximum(m_old, jnp.max(s[r], axis=-1, keepdims=True))
            p = jnp.exp2(s[r] - m_new).astype(BF16)
            pv = jnp.dot(p, vt, preferred_element_type=F32)
            acc_ref[rows, :] = acc_ref[rows, :] * jnp.exp2(m_old - m_new) + pv
            m_ref[rows, :] = m_new
        return carry

    lax.fori_loop(0, seq // tk, body, 0)
    for r in range(rep):
        a = acc_ref[r * tq:(r + 1) * tq, :]
        o_ref[0, :, r * HEAD_DIM:(r + 1) * HEAD_DIM] = (a[:, :HEAD_DIM] / a[:, HEAD_DIM:]).astype(o_ref.dtype)


def _flash_attention(qk, qkv, *, bsz, seq, tq=256, tk=4096):
    qw = C_Q_HEADS * HEAD_DIM
    tq, tk = min(tq, seq), min(tk, seq)
    gq = C_REP * HEAD_DIM
    qk3 = qk.reshape(bsz, seq, qk.shape[-1])
    qkv3 = qkv.reshape(bsz, seq, qkv.shape[-1])
    k_blk0 = qw // HEAD_DIM
    v_blk0 = (qw + C_KV_HEADS * HEAD_DIM) // HEAD_DIM
    out = pl.pallas_call(
        functools.partial(_flash_kernel, tq=tq, tk=tk, rep=C_REP, seq=seq),
        out_shape=jax.ShapeDtypeStruct((bsz, seq, qw), BF16),
        grid=(bsz, C_KV_HEADS, seq // tq),
        in_specs=[pl.BlockSpec((1, tq, gq), lambda b, g, i: (b, i, g)),
                  pl.BlockSpec((1, seq, HEAD_DIM), lambda b, g, i: (b, 0, k_blk0 + g)),
                  pl.BlockSpec((1, seq, HEAD_DIM), lambda b, g, i: (b, 0, v_blk0 + g))],
        out_specs=pl.BlockSpec((1, tq, gq), lambda b, g, i: (b, i, g)),
        scratch_shapes=[pltpu.VMEM((C_REP * tq, HEAD_DIM), BF16),
                        pltpu.VMEM((seq, 2 * HEAD_DIM), BF16),
                        pltpu.VMEM((C_REP * tq, 1), F32),
                        pltpu.VMEM((C_REP * tq, 2 * HEAD_DIM), F32)],
        compiler_params=_cparams(("parallel", "parallel", "arbitrary")),
        name="gqa_flash_attention",
    )(qk3, qk3, qkv3)
    return out.reshape(bsz * seq, qw)


def _rope_cos_sin(pos, dim, theta):
    inv = 1.0 / (theta ** (jnp.arange(0, dim, 2, dtype=F32) / dim))
    ang = pos.astype(F32)[:, None] * inv[None, :]
    return jnp.cos(ang), jnp.sin(ang)


def _tables(seq):
    tok = jnp.arange(seq)
    cos_b, sin_b = _rope_cos_sin(tok, B_ROT_DIM, B_ROPE_THETA)
    ones = jnp.ones((seq, HEAD_DIM - B_ROT_DIM), F32)
    cb = jnp.concatenate([cos_b, cos_b, ones], axis=-1)
    sb = jnp.concatenate([-sin_b, sin_b, jnp.zeros_like(ones)], axis=-1)
    cos_r, sin_r = _rope_cos_sin(tok // GRID_W, HEAD_DIM // 2, C_ROPE_THETA)
    cos_c, sin_c = _rope_cos_sin(tok % GRID_W, HEAD_DIM // 2, C_ROPE_THETA)
    cc = jnp.concatenate([cos_r, cos_r, cos_c, cos_c], axis=-1)
    sc = jnp.concatenate([-sin_r, sin_r, -sin_c, sin_c], axis=-1)
    return cb, sb, cc, sc


def _ffn(x2d, norm_w, w_gate, w_up, w_down, layer):
    h = _rmsnorm(x2d, norm_w, BF16)
    act = _gateup(h, w_gate, w_up, layer, tm=1024, tn=256)
    dff = act.shape[1]
    nsplit = 2 if (dff // 2) % LANES == 0 else 1
    for kb in range(nsplit):
        x2d = _wmatmul([(act, kb, dff // nsplit)], w_down, w_lead=layer, w_row_blk=kb, n=x2d.shape[1],
                       tm=1024, tn=256, out_dtype=F32, residual=x2d)
    return x2d


def _mixer_ab(x2d, norm_w, w_in, conv_w, a_log, dt_bias, out_norm, w_out, j, cb, sb, *, bsz, seq):
    t, dm = x2d.shape
    n_heads = a_log.shape[-1]
    a_qk = n_heads * HEAD_DIM
    a_v = n_heads * HEAD_DIM
    a_qkv = 2 * a_qk + a_v
    ngate = 2 * n_heads
    b_w = (w_in.shape[-1] - a_qkv - a_v - 2 * ngate) // 3
    h = _rmsnorm(x2d, norm_w, BF16)

    w_beta = w_in[j, :, a_qkv + a_v:a_qkv + a_v + ngate].reshape(dm, 2, n_heads)
    w_alpha = w_in[j, :, a_qkv + a_v + ngate:a_qkv + a_v + 2 * ngate].reshape(dm, 2, n_heads)
    w_g = jnp.concatenate([w_beta, w_alpha, jnp.zeros((dm, 2, LANES - 2 * n_heads), F32)], axis=-1)
    w_g = w_g.reshape(1, dm, 2 * LANES)
    w_b = w_in[j, :, a_qkv + a_v + 2 * ngate:].astype(BF16)[None]

    proj_a = _wmatmul([(h, 0, dm)], w_in, w_lead=j, n=a_qkv + a_v, tm=1024, tn=512, out_dtype=BF16)
    gates = _wmatmul([(h, 0, dm)], w_g, w_lead=0, n=2 * LANES, tm=1024, tn=2 * LANES, out_dtype=F32)
    proj_b = _wmatmul([(h, 0, dm)], w_b, w_lead=0, n=3 * b_w, tm=1024, tn=512, out_dtype=BF16)

    qkv = _conv_silu_l2(proj_a.reshape(bsz, seq, -1), conv_w, qk_width=2 * a_qk, q_width=a_qk, n_ch=a_qkv)
    pad = jnp.zeros((2, LANES - 2 * n_heads), F32)
    alog_row = jnp.concatenate([jnp.zeros((2, n_heads), F32), a_log.astype(F32), pad], axis=-1).reshape(2, 1, LANES)
    dtb_row = jnp.concatenate([jnp.zeros((2, n_heads), F32), dt_bias.astype(F32), pad], axis=-1).reshape(2, 1, LANES)
    o2 = _delta_scan(qkv, gates.reshape(bsz, seq, 2 * LANES), alog_row, dtb_row, n_heads=n_heads)
    o_a = _delta_out(o2.reshape(2, t, a_v), proj_a, out_norm, z_col_block=a_qkv // 512)

    n_groups = len(B_PATTERNS)
    os_, ls_ = [], []
    for gi, (_, dil) in enumerate(B_PATTERNS):
        grp = _rope_deint(proj_b, cb, sb, gi, dil, bsz=bsz, seq=seq, n_groups=n_groups)
        o_g, l_g = _dilated_group(grp)
        os_.append(o_g)
        ls_.append(l_g)
    o_b = _merge_groups(os_, ls_, bsz=bsz, seq=seq)

    return _wmatmul([(o_a, 0, a_v), (o_b, 0, o_b.shape[1])], w_out, w_lead=j, n=dm, tm=1024, tn=512,
                    out_dtype=F32, residual=x2d)


def _mixer_c(x2d, norm_w, w_qkv, q_norm, k_norm, w_out, j, cc, sc, *, bsz, seq):
    dm = x2d.shape[1]
    h = _rmsnorm(x2d, norm_w, BF16)
    qkv = _wmatmul([(h, 0, dm)], w_qkv, w_lead=j, n=w_qkv.shape[-1], tm=1024, tn=512, out_dtype=BF16)
    qw = C_Q_HEADS * HEAD_DIM
    kw = C_KV_HEADS * HEAD_DIM
    qk = _qknorm_rope(qkv, q_norm, k_norm, cc, sc, q_width=qw, qk_width=qw + kw, seq=seq)
    o = _flash_attention(qk, qkv, bsz=bsz, seq=seq)
    return _wmatmul([(o, 0, qw)], w_out, w_lead=j, n=dm, tm=1024, tn=512, out_dtype=F32, residual=x2d)


def kernel(x, norm_mix, norm_ffn, norm_final, ab_w_in, ab_conv_w, ab_a_log, ab_dt_bias, ab_out_norm, ab_w_out,
           c_w_qkv, c_q_norm, c_k_norm, c_w_out, ffn_w_gate, ffn_w_up, ffn_w_down):
    bsz, seq, dm = x.shape
    depth = norm_mix.shape[0]
    cb, sb, cc, sc = _tables(seq)
    x2d = x.reshape(bsz * seq, dm)
    for layer in range(depth):
        j = layer // 2
        if layer % 2 == 0:
            x2d = _mixer_ab(x2d, norm_mix[layer], ab_w_in, ab_conv_w[j], ab_a_log[j], ab_dt_bias[j],
                            ab_out_norm[j], ab_w_out, j, cb, sb, bsz=bsz, seq=seq)
        else:
            x2d = _mixer_c(x2d, norm_mix[layer], c_w_qkv, c_q_norm[j], c_k_norm[j], c_w_out, j,
                           cc, sc, bsz=bsz, seq=seq)
        x2d = _ffn(x2d, norm_ffn[layer], ffn_w_gate, ffn_w_up, ffn_w_down, layer)
    out = _rmsnorm(x2d, norm_final, x.dtype)
    return out.reshape(bsz, seq, dm)
```

```python
import functools

import jax
import jax.numpy as jnp
from jax import lax
from jax.experimental import pallas as pl
from jax.experimental.pallas import tpu as pltpu

HEAD_DIM = 128
NORM_EPS = 1e-6
L2_EPS = 1e-6
GRID_W = 64
A_CONV_W = 5
A_CHUNK = 64
B_PATTERNS = ((128, 1), (512, 4), (2048, 16))
B_HEADS_PER_GROUP = 8
B_ROT_DIM = HEAD_DIM // 4
B_ROPE_THETA = 500000.0
C_Q_HEADS = 32
C_KV_HEADS = 8
C_REP = C_Q_HEADS // C_KV_HEADS
C_ROPE_THETA = 10000.0
LOG2E = 1.4426950408889634

LANES = 128
VMEM_LIMIT = 56 * 1024 * 1024

BF16 = jnp.bfloat16
F32 = jnp.float32
HI = lax.Precision.HIGHEST


def _cparams(sem, vmem=VMEM_LIMIT):
    return pltpu.CompilerParams(dimension_semantics=sem, vmem_limit_bytes=vmem)


def _rmsnorm_kernel(x_ref, w_ref, o_ref):
    x = x_ref[...].astype(F32)
    ms = jnp.mean(x * x, axis=-1, keepdims=True)
    o_ref[...] = (x * lax.rsqrt(ms + NORM_EPS) * w_ref[...]).astype(o_ref.dtype)


def _rmsnorm(x2d, w, out_dtype, tr=256):
    m, d = x2d.shape
    return pl.pallas_call(
        _rmsnorm_kernel,
        out_shape=jax.ShapeDtypeStruct((m, d), out_dtype),
        grid=(m // tr,),
        in_specs=[pl.BlockSpec((tr, d), lambda i: (i, 0)),
                  pl.BlockSpec((1, d), lambda i: (0, 0))],
        out_specs=pl.BlockSpec((tr, d), lambda i: (i, 0)),
        compiler_params=_cparams(("parallel",)),
        name="rmsnorm",
    )(x2d, w.reshape(1, d).astype(F32))


def _stage_weight_chunk(chunk_ref, w16, ck):
    jp = pl.program_id(0)
    i = pl.program_id(1)
    w16[jp % 2, pl.ds(pl.multiple_of(i * ck, 16), ck), :] = chunk_ref[...].astype(BF16)


def _wmm_kernel(*refs, ksizes, has_res, ck):
    na = len(ksizes)
    a_refs = refs[:na]
    chunk_ref = refs[na]
    r_ref = refs[na + 1] if has_res else None
    o_ref = refs[na + 1 + int(has_res)]
    w16 = refs[na + 2 + int(has_res)]
    jp = pl.program_id(0)
    _stage_weight_chunk(chunk_ref, w16, ck)

    @pl.when(jp > 0)
    def _():
        slot = (jp + 1) % 2
        off = 0
        acc = None
        for a_ref, ks in zip(a_refs, ksizes):
            part = jnp.dot(a_ref[...], w16[slot, off:off + ks, :], preferred_element_type=F32)
            acc = part if acc is None else acc + part
            off += ks
        if has_res:
            acc = acc + r_ref[...]
        o_ref[...] = acc.astype(o_ref.dtype)


def _wmatmul(a_ops, w, *, w_lead, w_row_blk=0, w_col_blk0=0, n, tm, tn, out_dtype, residual=None):
    m = a_ops[0][0].shape[0]
    ksizes = tuple(k for _, _, k in a_ops)
    ktot = sum(ksizes)
    tm, tn = min(tm, m), min(tn, n)
    ni, nj = m // tm, n // tn
    ck = ktot // ni
    assert m % tm == 0 and n % tn == 0 and ktot % ni == 0 and ck % 16 == 0
    row_of = lambda jp, i: jnp.where(jp == 0, 0, i)
    col_of = lambda jp: jnp.maximum(jp - 1, 0)
    in_specs = [pl.BlockSpec((tm, ks), functools.partial(lambda jp, i, blk: (row_of(jp, i), blk), blk=blk))
                for _, blk, ks in a_ops]
    in_specs.append(pl.BlockSpec((None, ck, tn),
                                 lambda jp, i: (w_lead, w_row_blk * ni + i, w_col_blk0 + jnp.minimum(jp, nj - 1))))
    args = [a for a, _, _ in a_ops] + [w]
    if residual is not None:
        in_specs.append(pl.BlockSpec((tm, tn), lambda jp, i: (row_of(jp, i), col_of(jp))))
        args.append(residual)
    return pl.pallas_call(
        functools.partial(_wmm_kernel, ksizes=ksizes, has_res=residual is not None, ck=ck),
        out_shape=jax.ShapeDtypeStruct((m, n), out_dtype),
        grid=(nj + 1, ni),
        in_specs=in_specs,
        out_specs=pl.BlockSpec((tm, tn), lambda jp, i: (row_of(jp, i), col_of(jp))),
        scratch_shapes=[pltpu.VMEM((2, ktot, tn), BF16)],
        compiler_params=_cparams(("arbitrary", "arbitrary")),
        name="proj_matmul",
    )(*args)


def _gateup_kernel(h_ref, cg_ref, cu_ref, o_ref, wg16, wu16, *, ck):
    jp = pl.program_id(0)
    _stage_weight_chunk(cg_ref, wg16, ck)
    _stage_weight_chunk(cu_ref, wu16, ck)

    @pl.when(jp > 0)
    def _():
        slot = (jp + 1) % 2
        h = h_ref[...]
        g = jnp.dot(h, wg16[slot], preferred_element_type=F32)
        u = jnp.dot(h, wu16[slot], preferred_element_type=F32)
        o_ref[...] = (g * jax.nn.sigmoid(g) * u).astype(o_ref.dtype)


def _gateup(h, w_gate, w_up, layer, *, tm, tn):
    m, kdim = h.shape
    n = w_gate.shape[-1]
    tm, tn = min(tm, m), min(tn, n)
    ni, nj = m // tm, n // tn
    ck = kdim // ni
    assert m % tm == 0 and n % tn == 0 and kdim % ni == 0 and ck % 16 == 0
    wspec = pl.BlockSpec((None, ck, tn), lambda jp, i: (layer, i, jnp.minimum(jp, nj - 1)))
    return pl.pallas_call(
        functools.partial(_gateup_kernel, ck=ck),
        out_shape=jax.ShapeDtypeStruct((m, n), BF16),
        grid=(nj + 1, ni),
        in_specs=[pl.BlockSpec((tm, kdim), lambda jp, i: (jnp.where(jp == 0, 0, i), 0)), wspec, wspec],
        out_specs=pl.BlockSpec((tm, tn), lambda jp, i: (jnp.where(jp == 0, 0, i), jnp.maximum(jp - 1, 0))),
        scratch_shapes=[pltpu.VMEM((2, kdim, tn), BF16), pltpu.VMEM((2, kdim, tn), BF16)],
        compiler_params=_cparams(("arbitrary", "arbitrary")),
        name="ffn_gate_up",
    )(h, w_gate, w_up)


def _shift_cast_kernel(main_ref, next_ref, o_ref, *, shift):
    o_ref[...] = jnp.concatenate([main_ref[:, shift:], next_ref[:, :shift]], axis=1).astype(o_ref.dtype)


def _shift_cast_columns(w, lead, col0, width, *, tr=512, tc=512):
    _, kdim, _ = w.shape
    base = (col0 // tc) * tc
    shift = col0 - base
    tr = min(tr, kdim)
    assert 0 < shift < LANES and width % tc == 0 and kdim % tr == 0
    nb = tc // LANES
    return pl.pallas_call(
        functools.partial(_shift_cast_kernel, shift=shift),
        out_shape=jax.ShapeDtypeStruct((1, kdim, width), BF16),
        grid=(kdim // tr, width // tc),
        in_specs=[pl.BlockSpec((None, tr, tc), lambda i, j: (lead, i, base // tc + j)),
                  pl.BlockSpec((None, tr, LANES), lambda i, j: (lead, i, (base // tc + j + 1) * nb))],
        out_specs=pl.BlockSpec((None, tr, tc), lambda i, j: (0, i, j)),
        compiler_params=_cparams(("parallel", "parallel")),
        name="weight_shift_cast",
    )(w, w)


_CONV_HALO = 16


def _conv_kernel(prev_ref, cur_ref, next_ref, w_ref, o_ref, ext_ref, *, ts, cw, n_qk_blocks, n_q_blocks):
    i = pl.program_id(1)
    nblk = pl.num_programs(1)
    j = pl.program_id(2)
    pad = A_CONV_W // 2
    prev = prev_ref[0].astype(F32)
    nxt = next_ref[0].astype(F32)
    ext_ref[0:_CONV_HALO, :] = jnp.where(i > 0, prev, 0.0)
    ext_ref[_CONV_HALO:_CONV_HALO + ts, :] = cur_ref[0].astype(F32)
    ext_ref[_CONV_HALO + ts:, :] = jnp.where(i < nblk - 1, nxt, 0.0)
    w = w_ref[...]
    y = jnp.zeros((ts, cw), F32)
    for t in range(A_CONV_W):
        y = y + ext_ref[_CONV_HALO - pad + t:_CONV_HALO - pad + t + ts, :] * w[t:t + 1, :]
    y = y * jax.nn.sigmoid(y)
    is_qk = j < n_qk_blocks
    scale = jnp.where(j < n_q_blocks, HEAD_DIM ** -0.5, 1.0).astype(F32)
    for hh in range(cw // HEAD_DIM):
        yh = y[:, hh * HEAD_DIM:(hh + 1) * HEAD_DIM]
        ss = jnp.sum(yh * yh, axis=-1, keepdims=True)
        yn = yh * (lax.rsqrt(ss + L2_EPS) * scale)
        o_ref[0, :, hh * HEAD_DIM:(hh + 1) * HEAD_DIM] = jnp.where(is_qk, yn, yh).astype(o_ref.dtype)


def _conv_silu_l2(proj_a, conv_w, *, qk_width, q_width, n_ch, ts=512, cw=512):
    bsz, seq, _ = proj_a.shape
    ts = min(ts, seq)
    hb = ts // _CONV_HALO
    nhalo = seq // _CONV_HALO
    w8 = jnp.zeros((8, n_ch), F32).at[:A_CONV_W].set(conv_w.astype(F32))
    kern = functools.partial(_conv_kernel, ts=ts, cw=cw, n_qk_blocks=qk_width // cw, n_q_blocks=q_width // cw)
    return pl.pallas_call(
        kern,
        out_shape=jax.ShapeDtypeStruct((bsz, seq, n_ch), BF16),
        grid=(bsz, seq // ts, n_ch // cw),
        in_specs=[
            pl.BlockSpec((1, _CONV_HALO, cw), lambda b, i, j: (b, jnp.maximum(i * hb - 1, 0), j)),
            pl.BlockSpec((1, ts, cw), lambda b, i, j: (b, i, j)),
            pl.BlockSpec((1, _CONV_HALO, cw), lambda b, i, j: (b, jnp.minimum((i + 1) * hb, nhalo - 1), j)),
            pl.BlockSpec((8, cw), lambda b, i, j: (0, j)),
        ],
        out_specs=pl.BlockSpec((1, ts, cw), lambda b, i, j: (b, i, j)),
        scratch_shapes=[pltpu.VMEM((ts + 2 * _CONV_HALO, cw), F32)],
        compiler_params=_cparams(("parallel", "parallel", "parallel")),
        name="conv_silu_l2norm",
    )(proj_a, proj_a, proj_a, w8)


def _split_bf16(x):
    hi = x.astype(BF16)
    lo = (x - hi.astype(F32)).astype(BF16)
    return hi, lo


def _dot3(a, b):
    a_hi, a_lo = a
    b_hi, b_lo = b
    return (jnp.dot(a_hi, b_hi, preferred_element_type=F32)
            + (jnp.dot(a_hi, b_lo, preferred_element_type=F32) + jnp.dot(a_lo, b_hi, preferred_element_type=F32)))


def _delta_kernel(q_ref, k_ref, v_ref, gt_ref, alog_ref, dtb_ref, o_ref, state_ref, *, hg, n_heads):
    c = A_CHUNK
    d = pl.program_id(0) % 2
    grp = pl.program_id(1)
    n = pl.program_id(2)

    @pl.when(n == 0)
    def _():
        state_ref[...] = jnp.zeros_like(state_ref)

    row = lax.broadcasted_iota(jnp.int32, (c, c), 0)
    col = lax.broadcasted_iota(jnp.int32, (c, c), 1)
    ahead = jnp.where(d == 0, row - col, col - row)
    incl = ahead >= 0
    strict = ahead > 0
    tri = jnp.where(incl, 1.0, 0.0).astype(F32)
    eye = jnp.where(row == col, 1.0, 0.0).astype(F32)

    graw = gt_ref[0]
    beta_all = jax.nn.sigmoid(graw)
    z = graw + dtb_ref[0]
    softplus = jnp.maximum(z, 0.0) + jnp.log(1.0 + jnp.exp(-jnp.abs(z)))
    g_all = -jnp.exp(alog_ref[0]) * softplus
    gc_all = jnp.dot(tri, g_all, precision=HI, preferred_element_type=F32)
    gct_all = lax.dot_general(g_all, tri, (((0,), (1,)), ((), ())), precision=HI,
                              preferred_element_type=F32)
    gtot_all = jnp.sum(g_all, axis=0, keepdims=True)
    lane = lax.broadcasted_iota(jnp.int32, (c, LANES), 1)
    subl = lax.broadcasted_iota(jnp.int32, (LANES, c), 0)
    contract_last = (((1,), (1,)), ((), ()))
    contract_first = (((0,), (0,)), ((), ()))

    heads = range(hg)
    sls = [slice(hh * HEAD_DIM, (hh + 1) * HEAD_DIM) for hh in heads]
    q16 = [q_ref[0, :, sl] for sl in sls]
    k16 = [k_ref[0, :, sl] for sl in sls]
    gcol, beta, gtot, decay = [], [], [], []
    for hh in heads:
        a_lane = n_heads + grp * hg + hh
        sel_a = lane == a_lane
        sel_b = lane == (grp * hg + hh)
        gcol.append(jnp.sum(jnp.where(sel_a, gc_all, 0.0), axis=1, keepdims=True))
        beta.append(jnp.sum(jnp.where(sel_b, beta_all, 0.0), axis=1, keepdims=True))
        gtot.append(jnp.sum(jnp.where(sel_a[0:1], gtot_all, 0.0), axis=1, keepdims=True))
        grow = jnp.sum(jnp.where(subl == a_lane, gct_all, 0.0), axis=0, keepdims=True)
        diff = gcol[hh] - grow
        decay.append(jnp.where(incl, jnp.exp(jnp.where(incl, diff, 0.0)), 0.0))
    kb = [k16[hh].astype(F32) * beta[hh] for hh in heads]
    kq = [lax.dot_general(jnp.concatenate([kb[hh].astype(BF16), q16[hh]], axis=0), k16[hh], contract_last,
                          preferred_element_type=F32) for hh in heads]
    attn = [(kq[hh][c:] * decay[hh]).astype(BF16) for hh in heads]
    x = [-jnp.where(strict, kq[hh][:c] * decay[hh], 0.0) for hh in heads]
    t_mat = [eye + x[hh] for hh in heads]
    p = 2
    while p < c:
        xs = [_split_bf16(x[hh]) for hh in heads]
        x = [_dot3(xs[hh], xs[hh]) for hh in heads]
        t_mat = [t_mat[hh] + _dot3(_split_bf16(t_mat[hh]), _split_bf16(x[hh])) for hh in heads]
        p *= 2
    eg = [jnp.exp(gcol[hh]) for hh in heads]
    rhs = [jnp.concatenate([v_ref[0, :, sls[hh]].astype(F32) * beta[hh], kb[hh] * eg[hh]], axis=1).astype(BF16)
           for hh in heads]
    uw = [jnp.dot(t_mat[hh].astype(BF16), rhs[hh], preferred_element_type=F32) for hh in heads]
    state = [state_ref[hh] for hh in heads]
    ws = [jnp.dot(jnp.concatenate([uw[hh][:, HEAD_DIM:], q16[hh].astype(F32) * eg[hh]], axis=0).astype(BF16),
                  state[hh].astype(BF16), preferred_element_type=F32) for hh in heads]
    vn16 = [(uw[hh][:, :HEAD_DIM] - ws[hh][:c]).astype(BF16) for hh in heads]
    for hh in heads:
        o = ws[hh][c:] + jnp.dot(attn[hh], vn16[hh], preferred_element_type=F32)
        o_ref[0, 0, :, sls[hh]] = o.astype(o_ref.dtype)
    for hh in heads:
        kd = (k16[hh].astype(F32) * jnp.exp(gtot[hh] - gcol[hh])).astype(BF16)
        state_ref[hh] = state[hh] * jnp.exp(gtot[hh]) + lax.dot_general(kd, vn16[hh], contract_first,
                                                                       preferred_element_type=F32)


def _delta_scan(qkv, gates, alog_row, dtb_row, *, n_heads, hg=16):
    bsz, seq, _ = qkv.shape
    c = A_CHUNK
    nchunk = seq // c
    ngrp = n_heads // hg
    wblk = hg * HEAD_DIM

    def cidx(bd, n):
        return jnp.where(bd % 2 == 0, n, nchunk - 1 - n)

    kern = functools.partial(_delta_kernel, hg=hg, n_heads=n_heads)
    return pl.pallas_call(
        kern,
        out_shape=jax.ShapeDtypeStruct((2, bsz, seq, n_heads * HEAD_DIM), F32),
        grid=(bsz * 2, ngrp, nchunk),
        in_specs=[
            pl.BlockSpec((1, c, wblk), lambda bd, g, n: (bd // 2, cidx(bd, n), g)),
            pl.BlockSpec((1, c, wblk), lambda bd, g, n: (bd // 2, cidx(bd, n), ngrp + g)),
            pl.BlockSpec((1, c, wblk), lambda bd, g, n: (bd // 2, cidx(bd, n), 2 * ngrp + g)),
            pl.BlockSpec((1, c, LANES), lambda bd, g, n: (bd // 2, cidx(bd, n), bd % 2)),
            pl.BlockSpec((1, 1, LANES), lambda bd, g, n: (bd % 2, 0, 0)),
            pl.BlockSpec((1, 1, LANES), lambda bd, g, n: (bd % 2, 0, 0)),
        ],
        out_specs=pl.BlockSpec((1, 1, c, wblk), lambda bd, g, n: (bd % 2, bd // 2, cidx(bd, n), g)),
        scratch_shapes=[pltpu.VMEM((hg, HEAD_DIM, HEAD_DIM), F32)],
        compiler_params=_cparams(("parallel", "parallel", "arbitrary")),
        name="gated_delta_scan",
    )(qkv, qkv, qkv, gates, alog_row, dtb_row)


def _delta_out_kernel(of_ref, ob_ref, z_ref, w_ref, o_ref, *, cw):
    o = of_ref[0] + ob_ref[0]
    z = z_ref[...].astype(F32)
    w = w_ref[...]
    for hh in range(cw // HEAD_DIM):
        sl = slice(hh * HEAD_DIM, (hh + 1) * HEAD_DIM)
        oh = o[:, sl]
        ms = jnp.mean(oh * oh, axis=-1, keepdims=True)
        zh = z[:, sl]
        o_ref[:, sl] = (oh * lax.rsqrt(ms + NORM_EPS) * w * (zh * jax.nn.sigmoid(zh))).astype(o_ref.dtype)


def _delta_out(o2, proj_a, out_norm, *, z_col_block, tr=512, cw=512):
    _, t, vdim = o2.shape
    tr = min(tr, t)
    return pl.pallas_call(
        functools.partial(_delta_out_kernel, cw=cw),
        out_shape=jax.ShapeDtypeStruct((t, vdim), BF16),
        grid=(t // tr, vdim // cw),
        in_specs=[
            pl.BlockSpec((1, tr, cw), lambda i, j: (0, i, j)),
            pl.BlockSpec((1, tr, cw), lambda i, j: (1, i, j)),
            pl.BlockSpec((tr, cw), lambda i, j: (i, z_col_block + j)),
            pl.BlockSpec((1, HEAD_DIM), lambda i, j: (0, 0)),
        ],
        out_specs=pl.BlockSpec((tr, cw), lambda i, j: (i, j)),
        compiler_params=_cparams(("parallel", "parallel")),
        name="delta_out_norm_gate",
    )(o2, o2, proj_a, out_norm.reshape(1, HEAD_DIM).astype(F32))


def _rotate_pairs(x, half):
    lane = lax.broadcasted_iota(jnp.int32, x.shape, 1)
    first = (lane % (2 * half)) < half
    return jnp.where(first, pltpu.roll(x, LANES - half, 1), pltpu.roll(x, half, 1))


def _deinterleave_matrix(tr, dil, transpose=False):
    per = tr // dil
    i_out = lax.broadcasted_iota(jnp.int32, (tr, tr), 1 if transpose else 0)
    i_in = lax.broadcasted_iota(jnp.int32, (tr, tr), 0 if transpose else 1)
    src = (i_out % per) * dil + i_out // per
    return jnp.where(i_in == src, 1.0, 0.0).astype(BF16)


def _rope_deint_kernel(q_ref, k_ref, v_ref, cos_ref, sin_ref, o_ref, *, tr, dil, gw):
    cos = cos_ref[...]
    sin = sin_ref[...]
    per = tr // dil
    perm = _deinterleave_matrix(tr, dil) if dil > 1 else None

    def emit(y16, col0):
        if dil > 1:
            y16 = jnp.dot(perm, y16, preferred_element_type=F32).astype(BF16)
        for r in range(dil):
            o_ref[0, r, :, col0:col0 + HEAD_DIM] = y16[r * per:(r + 1) * per, :]

    for part, ref in enumerate((q_ref, k_ref)):
        for hh in range(gw // HEAD_DIM):
            sl = slice(hh * HEAD_DIM, (hh + 1) * HEAD_DIM)
            x = ref[:, sl].astype(F32)
            y = x * cos + _rotate_pairs(x, B_ROT_DIM // 2) * sin
            emit(y.astype(BF16), part * gw + hh * HEAD_DIM)
    for hh in range(gw // HEAD_DIM):
        emit(v_ref[:, hh * HEAD_DIM:(hh + 1) * HEAD_DIM], 2 * gw + hh * HEAD_DIM)


def _rope_deint(proj_b, cos_t, sin_t, gi, dil, *, bsz, seq, n_groups, tr=256):
    gw = B_HEADS_PER_GROUP * HEAD_DIM
    nsb = seq // tr
    per = tr // dil
    return pl.pallas_call(
        functools.partial(_rope_deint_kernel, tr=tr, dil=dil, gw=gw),
        out_shape=jax.ShapeDtypeStruct((bsz, dil, seq // dil, 3 * gw), BF16),
        grid=(bsz, nsb),
        in_specs=[pl.BlockSpec((tr, gw), lambda b, i: (b * nsb + i, gi)),
                  pl.BlockSpec((tr, gw), lambda b, i: (b * nsb + i, n_groups + gi)),
                  pl.BlockSpec((tr, gw), lambda b, i: (b * nsb + i, 2 * n_groups + gi)),
                  pl.BlockSpec((tr, HEAD_DIM), lambda b, i: (i, 0)),
                  pl.BlockSpec((tr, HEAD_DIM), lambda b, i: (i, 0))],
        out_specs=pl.BlockSpec((1, dil, per, 3 * gw), lambda b, i: (b, 0, i, 0)),
        compiler_params=_cparams(("parallel", "parallel")),
        name=f"rope_deinterleave_d{dil}",
    )(proj_b, proj_b, proj_b, cos_t, sin_t)


_BW = 64
_BQ = 128


def _dil_kernel(q_ref, kc_ref, kp_ref, kn_ref, vc_ref, vp_ref, vn_ref, o_ref, l_ref, kext, vext,
                *, qb, nh, seq_len):
    i = pl.program_id(2)
    base = i * qb
    kext[0:_BW, :] = kp_ref[0, 0]
    kext[_BW:_BW + qb, :] = kc_ref[0, 0]
    kext[_BW + qb:, :] = kn_ref[0, 0]
    vext[0:_BW, :] = vp_ref[0, 0]
    vext[_BW:_BW + qb, :] = vc_ref[0, 0]
    vext[_BW + qb:, :] = vn_ref[0, 0]
    scale = HEAD_DIM ** -0.5
    nkeys = _BQ + 2 * _BW
    qi = lax.broadcasted_iota(jnp.int32, (_BQ, nkeys), 0)
    kj = lax.broadcasted_iota(jnp.int32, (_BQ, nkeys), 1)
    lane = lax.broadcasted_iota(jnp.int32, (_BQ, LANES), 1)
    for sb in range(qb // _BQ):
        qpos = base + sb * _BQ + qi
        kpos = base + sb * _BQ - _BW + kj
        valid = (jnp.abs(kpos - qpos) <= _BW) & (kpos >= 0) & (kpos < seq_len)
        lse_blk = jnp.zeros((_BQ, LANES), F32)
        for hh in range(nh):
            sl = slice(hh * HEAD_DIM, (hh + 1) * HEAD_DIM)
            q = q_ref[0, 0, sb * _BQ:(sb + 1) * _BQ, sl]
            kx = kext[sb * _BQ:sb * _BQ + nkeys, sl]
            vx = vext[sb * _BQ:sb * _BQ + nkeys, sl]
            s = lax.dot_general(q, kx, (((1,), (1,)), ((), ())), preferred_element_type=F32) * scale
            s = jnp.where(valid, s, -1e30)
            m = jnp.max(s, axis=-1, keepdims=True)
            e = jnp.where(valid, jnp.exp(s - m), 0.0)
            l = jnp.sum(e, axis=-1, keepdims=True)
            p = (e / l).astype(BF16)
            o = jnp.dot(p, vx, preferred_element_type=F32)
            o_ref[0, 0, sb * _BQ:(sb + 1) * _BQ, sl] = o.astype(o_ref.dtype)
            lse_blk = jnp.where(lane == hh, m + jnp.log(l), lse_blk)
        l_ref[0, 0, sb * _BQ:(sb + 1) * _BQ, :] = lse_blk


def _dilated_group(grp_qkv):
    bsz, dil, ln, _ = grp_qkv.shape
    gw = B_HEADS_PER_GROUP * HEAD_DIM
    qb = min(512, ln)
    hb = qb // _BW
    nhalo = ln // _BW
    lo = lambda i: jnp.maximum(i * hb - 1, 0)
    hi = lambda i: jnp.minimum((i + 1) * hb, nhalo - 1)

    def cur(col):
        return pl.BlockSpec((1, 1, qb, gw), lambda b, r, i: (b, r, i, col))

    def halo(col, f):
        return pl.BlockSpec((1, 1, _BW, gw), lambda b, r, i: (b, r, f(i), col))

    return pl.pallas_call(
        functools.partial(_dil_kernel, qb=qb, nh=B_HEADS_PER_GROUP, seq_len=ln),
        out_shape=(jax.ShapeDtypeStruct((bsz, dil, ln, gw), BF16),
                   jax.ShapeDtypeStruct((bsz, dil, ln, LANES), F32)),
        grid=(bsz, dil, ln // qb),
        in_specs=[cur(0), cur(1), halo(1, lo), halo(1, hi), cur(2), halo(2, lo), halo(2, hi)],
        out_specs=(pl.BlockSpec((1, 1, qb, gw), lambda b, r, i: (b, r, i, 0)),
                   pl.BlockSpec((1, 1, qb, LANES), lambda b, r, i: (b, r, i, 0))),
        scratch_shapes=[pltpu.VMEM((qb + 2 * _BW, gw), BF16), pltpu.VMEM((qb + 2 * _BW, gw), BF16)],
        compiler_params=_cparams(("parallel", "parallel", "parallel")),
        name=f"dilated_attn_d{dil}",
    )(*([grp_qkv] * 7))


def _merge_kernel(*refs, tr, dils, nh):
    ng = len(dils)
    o_refs = refs[:ng]
    l_refs = refs[ng:2 * ng]
    out_ref = refs[2 * ng]
    outs, lses = [], []
    for g, dil in enumerate(dils):
        per = tr // dil
        if dil == 1:
            outs.append(o_refs[g][0, 0].astype(F32))
            lses.append(l_refs[g][0, 0])
            continue
        pt = _deinterleave_matrix(tr, dil, transpose=True)
        y = jnp.concatenate([o_refs[g][0, r] for r in range(dil)], axis=0)
        outs.append(jnp.dot(pt, y, preferred_element_type=F32))
        ls = jnp.concatenate([l_refs[g][0, r] for r in range(dil)], axis=0)
        h1 = ls.astype(BF16)
        r1 = ls - h1.astype(F32)
        h2 = r1.astype(BF16)
        h3 = (r1 - h2.astype(F32)).astype(BF16)
        lses.append(jnp.dot(pt, h1, preferred_element_type=F32)
                    + (jnp.dot(pt, h2, preferred_element_type=F32) + jnp.dot(pt, h3, preferred_element_type=F32)))
    mx = lses[0]
    for g in range(1, ng):
        mx = jnp.maximum(mx, lses[g])
    es = [jnp.exp(l - mx) for l in lses]
    den = es[0]
    for g in range(1, ng):
        den = den + es[g]
    ws = [e / den for e in es]
    lane = lax.broadcasted_iota(jnp.int32, (tr, LANES), 1)
    for hh in range(nh):
        sl = slice(hh * HEAD_DIM, (hh + 1) * HEAD_DIM)
        acc = None
        for g in range(ng):
            wcol = jnp.sum(jnp.where(lane == hh, ws[g], 0.0), axis=1, keepdims=True)
            term = outs[g][:, sl] * wcol
            acc = term if acc is None else acc + term
        out_ref[:, sl] = acc.astype(out_ref.dtype)


def _merge_groups(os_, ls_, *, bsz, seq, tr=256):
    gw = B_HEADS_PER_GROUP * HEAD_DIM
    dils = tuple(o.shape[1] for o in os_)
    nsb = seq // tr
    in_specs = ([pl.BlockSpec((1, d, tr // d, gw), lambda b, i: (b, 0, i, 0)) for d in dils]
                + [pl.BlockSpec((1, d, tr // d, LANES), lambda b, i: (b, 0, i, 0)) for d in dils])
    return pl.pallas_call(
        functools.partial(_merge_kernel, tr=tr, dils=dils, nh=B_HEADS_PER_GROUP),
        out_shape=jax.ShapeDtypeStruct((bsz * seq, gw), BF16),
        grid=(bsz, nsb),
        in_specs=in_specs,
        out_specs=pl.BlockSpec((tr, gw), lambda b, i: (b * nsb + i, 0)),
        compiler_params=_cparams(("parallel", "parallel")),
        name="dilated_merge",
    )(*os_, *ls_)


def _qknorm_rope_kernel(x_ref, wq_ref, wk_ref, cos_ref, sin_ref, o_ref, *, cw, n_q_blocks):
    j = pl.program_id(1)
    is_q = j < n_q_blocks
    w = jnp.where(is_q, wq_ref[...], wk_ref[...])
    scale = jnp.where(is_q, HEAD_DIM ** -0.5 * LOG2E, 1.0).astype(F32)
    cos = cos_ref[...]
    sin = sin_ref[...]
    for hh in range(cw // HEAD_DIM):
        sl = slice(hh * HEAD_DIM, (hh + 1) * HEAD_DIM)
        x = x_ref[:, sl].astype(F32)
        ms = jnp.mean(x * x, axis=-1, keepdims=True)
        y = x * lax.rsqrt(ms + NORM_EPS) * w
        y = y * cos + _rotate_pairs(y, HEAD_DIM // 4) * sin
        o_ref[:, sl] = (y * scale).astype(o_ref.dtype)


def _qknorm_rope(qkv, q_norm, k_norm, cos_t, sin_t, *, q_width, qk_width, seq, tr=512, cw=512):
    t = qkv.shape[0]
    tr = min(tr, seq)
    nsb = seq // tr
    return pl.pallas_call(
        functools.partial(_qknorm_rope_kernel, cw=cw, n_q_blocks=q_width // cw),
        out_shape=jax.ShapeDtypeStruct((t, qk_width), BF16),
        grid=(t // tr, qk_width // cw),
        in_specs=[pl.BlockSpec((tr, cw), lambda i, j: (i, j)),
                  pl.BlockSpec((1, HEAD_DIM), lambda i, j: (0, 0)),
                  pl.BlockSpec((1, HEAD_DIM), lambda i, j: (0, 0)),
                  pl.BlockSpec((tr, HEAD_DIM), lambda i, j: (i % nsb, 0)),
                  pl.BlockSpec((tr, HEAD_DIM), lambda i, j: (i % nsb, 0))],
        out_specs=pl.BlockSpec((tr, cw), lambda i, j: (i, j)),
        compiler_params=_cparams(("parallel", "parallel")),
        name="qk_norm_axial_rope",
    )(qkv, q_norm.reshape(1, HEAD_DIM).astype(F32), k_norm.reshape(1, HEAD_DIM).astype(F32), cos_t, sin_t)


def _flash_kernel(q_ref, k_ref, v_ref, o_ref, qs_ref, v1_ref, m_ref, acc_ref, *, tq, tk, rep, seq):
    @pl.when(pl.program_id(2) == 0)
    def _():
        v1_ref[:, :HEAD_DIM] = v_ref[0]
        v1_ref[:, HEAD_DIM:] = jnp.ones((seq, HEAD_DIM), BF16)

    for r in range(rep):
        qs_ref[r * tq:(r + 1) * tq, :] = q_ref[0, :, r * HEAD_DIM:(r + 1) * HEAD_DIM]
    m_ref[...] = jnp.full(m_ref.shape, -1e30, F32)
    acc_ref[...] = jnp.zeros(acc_ref.shape, F32)

    def body(t, carry):
        off = pl.multiple_of(t * tk, tk)
        kt = k_ref[0, pl.ds(off, tk), :]
        vt = v1_ref[pl.ds(off, tk), :]
        s = [lax.dot_general(qs_ref[r * tq:(r + 1) * tq, :], kt, (((1,), (1,)), ((), ())),
                             preferred_element_type=F32) for r in range(rep)]
        for r in range(rep):
            rows = slice(r * tq, (r + 1) * tq)
            m_old = m_ref[rows, :]
            m_new = jnp.maximum(m_old, jnp.max(s[r], axis=-1, keepdims=True))
            p = jnp.exp2(s[r] - m_new).astype(BF16)
            pv = jnp.dot(p, vt, preferred_element_type=F32)
            acc_ref[rows, :] = acc_ref[rows, :] * jnp.exp2(m_old - m_new) + pv
            m_ref[rows, :] = m_new
        return carry

    lax.fori_loop(0, seq // tk, body, 0)
    for r in range(rep):
        a = acc_ref[r * tq:(r + 1) * tq, :]
        o_ref[0, :, r * HEAD_DIM:(r + 1) * HEAD_DIM] = (a[:, :HEAD_DIM] / a[:, HEAD_DIM:]).astype(o_ref.dtype)


def _flash_attention(qk, qkv, *, bsz, seq, tq=256, tk=4096):
    qw = C_Q_HEADS * HEAD_DIM
    tq, tk = min(tq, seq), min(tk, seq)
    gq = C_REP * HEAD_DIM
    qk3 = qk.reshape(bsz, seq, qk.shape[-1])
    qkv3 = qkv.reshape(bsz, seq, qkv.shape[-1])
    k_blk0 = qw // HEAD_DIM
    v_blk0 = (qw + C_KV_HEADS * HEAD_DIM) // HEAD_DIM
    out = pl.pallas_call(
        functools.partial(_flash_kernel, tq=tq, tk=tk, rep=C_REP, seq=seq),
        out_shape=jax.ShapeDtypeStruct((bsz, seq, qw), BF16),
        grid=(bsz, C_KV_HEADS, seq // tq),
        in_specs=[pl.BlockSpec((1, tq, gq), lambda b, g, i: (b, i, g)),
                  pl.BlockSpec((1, seq, HEAD_DIM), lambda b, g, i: (b, 0, k_blk0 + g)),
                  pl.BlockSpec((1, seq, HEAD_DIM), lambda b, g, i: (b, 0, v_blk0 + g))],
        out_specs=pl.BlockSpec((1, tq, gq), lambda b, g, i: (b, i, g)),
        scratch_shapes=[pltpu.VMEM((C_REP * tq, HEAD_DIM), BF16),
                        pltpu.VMEM((seq, 2 * HEAD_DIM), BF16),
                        pltpu.VMEM((C_REP * tq, 1), F32),
                        pltpu.VMEM((C_REP * tq, 2 * HEAD_DIM), F32)],
        compiler_params=_cparams(("parallel", "parallel", "arbitrary")),
        name="gqa_flash_attention",
    )(qk3, qk3, qkv3)
    return out.reshape(bsz * seq, qw)


def _rope_cos_sin(pos, dim, theta):
    inv = 1.0 / (theta ** (jnp.arange(0, dim, 2, dtype=F32) / dim))
    ang = pos.astype(F32)[:, None] * inv[None, :]
    return jnp.cos(ang), jnp.sin(ang)


def _tables(seq):
    tok = jnp.arange(seq)
    cos_b, sin_b = _rope_cos_sin(tok, B_ROT_DIM, B_ROPE_THETA)
    ones = jnp.ones((seq, HEAD_DIM - B_ROT_DIM), F32)
    cb = jnp.concatenate([cos_b, cos_b, ones], axis=-1)
    sb = jnp.concatenate([-sin_b, sin_b, jnp.zeros_like(ones)], axis=-1)
    cos_r, sin_r = _rope_cos_sin(tok // GRID_W, HEAD_DIM // 2, C_ROPE_THETA)
    cos_c, sin_c = _rope_cos_sin(tok % GRID_W, HEAD_DIM // 2, C_ROPE_THETA)
    cc = jnp.concatenate([cos_r, cos_r, cos_c, cos_c], axis=-1)
    sc = jnp.concatenate([-sin_r, sin_r, -sin_c, sin_c], axis=-1)
    return cb, sb, cc, sc


def _ffn(x2d, norm_w, w_gate, w_up, w_down, layer):
    h = _rmsnorm(x2d, norm_w, BF16)
    act = _gateup(h, w_gate, w_up, layer, tm=2048, tn=256)
    dff = act.shape[1]
    nsplit = 2 if (dff // 2) % LANES == 0 else 1
    for kb in range(nsplit):
        x2d = _wmatmul([(act, kb, dff // nsplit)], w_down, w_lead=layer, w_row_blk=kb, n=x2d.shape[1],
                       tm=1024, tn=512, out_dtype=F32, residual=x2d)
    return x2d


def _mixer_ab(x2d, norm_w, w_in, conv_w, a_log, dt_bias, out_norm, w_out, j, cb, sb, *, bsz, seq):
    t, dm = x2d.shape
    n_heads = a_log.shape[-1]
    a_qk = n_heads * HEAD_DIM
    a_v = n_heads * HEAD_DIM
    a_qkv = 2 * a_qk + a_v
    ngate = 2 * n_heads
    b_w = (w_in.shape[-1] - a_qkv - a_v - 2 * ngate) // 3
    h = _rmsnorm(x2d, norm_w, BF16)

    w_beta = w_in[j, :, a_qkv + a_v:a_qkv + a_v + ngate].reshape(dm, 2, n_heads)
    w_alpha = w_in[j, :, a_qkv + a_v + ngate:a_qkv + a_v + 2 * ngate].reshape(dm, 2, n_heads)
    w_g = jnp.concatenate([w_beta, w_alpha, jnp.zeros((dm, 2, LANES - 2 * n_heads), F32)], axis=-1)
    w_g = w_g.reshape(1, dm, 2 * LANES)
    w_b = _shift_cast_columns(w_in, j, a_qkv + a_v + 2 * ngate, 3 * b_w)

    proj_a = _wmatmul([(h, 0, dm)], w_in, w_lead=j, n=a_qkv + a_v, tm=2048, tn=512, out_dtype=BF16)
    gates = _wmatmul([(h, 0, dm)], w_g, w_lead=0, n=2 * LANES, tm=2048, tn=2 * LANES, out_dtype=F32)
    proj_b = _wmatmul([(h, 0, dm)], w_b, w_lead=0, n=3 * b_w, tm=2048, tn=512, out_dtype=BF16)

    qkv = _conv_silu_l2(proj_a.reshape(bsz, seq, -1), conv_w, qk_width=2 * a_qk, q_width=a_qk, n_ch=a_qkv)
    pad = jnp.zeros((2, LANES - 2 * n_heads), F32)
    alog_row = jnp.concatenate([jnp.zeros((2, n_heads), F32), a_log.astype(F32), pad], axis=-1).reshape(2, 1, LANES)
    dtb_row = jnp.concatenate([jnp.zeros((2, n_heads), F32), dt_bias.astype(F32), pad], axis=-1).reshape(2, 1, LANES)
    o2 = _delta_scan(qkv, gates.reshape(bsz, seq, 2 * LANES), alog_row, dtb_row, n_heads=n_heads)
    o_a = _delta_out(o2.reshape(2, t, a_v), proj_a, out_norm, z_col_block=a_qkv // 512)

    n_groups = len(B_PATTERNS)
    os_, ls_ = [], []
    for gi, (_, dil) in enumerate(B_PATTERNS):
        grp = _rope_deint(proj_b, cb, sb, gi, dil, bsz=bsz, seq=seq, n_groups=n_groups)
        o_g, l_g = _dilated_group(grp)
        os_.append(o_g)
        ls_.append(l_g)
    o_b = _merge_groups(os_, ls_, bsz=bsz, seq=seq)

    return _wmatmul([(o_a, 0, a_v), (o_b, 0, o_b.shape[1])], w_out, w_lead=j, n=dm, tm=1024, tn=512,
                    out_dtype=F32, residual=x2d)


def _mixer_c(x2d, norm_w, w_qkv, q_norm, k_norm, w_out, j, cc, sc, *, bsz, seq):
    dm = x2d.shape[1]
    h = _rmsnorm(x2d, norm_w, BF16)
    qkv = _wmatmul([(h, 0, dm)], w_qkv, w_lead=j, n=w_qkv.shape[-1], tm=2048, tn=512, out_dtype=BF16)
    qw = C_Q_HEADS * HEAD_DIM
    kw = C_KV_HEADS * HEAD_DIM
    qk = _qknorm_rope(qkv, q_norm, k_norm, cc, sc, q_width=qw, qk_width=qw + kw, seq=seq)
    o = _flash_attention(qk, qkv, bsz=bsz, seq=seq)
    return _wmatmul([(o, 0, qw)], w_out, w_lead=j, n=dm, tm=1024, tn=512, out_dtype=F32, residual=x2d)


def kernel(x, norm_mix, norm_ffn, norm_final, ab_w_in, ab_conv_w, ab_a_log, ab_dt_bias, ab_out_norm, ab_w_out,
           c_w_qkv, c_q_norm, c_k_norm, c_w_out, ffn_w_gate, ffn_w_up, ffn_w_down):
    bsz, seq, dm = x.shape
    depth = norm_mix.shape[0]
    cb, sb, cc, sc = _tables(seq)
    x2d = x.reshape(bsz * seq, dm)
    for layer in range(depth):
        j = layer // 2
        if layer % 2 == 0:
            x2d = _mixer_ab(x2d, norm_mix[layer], ab_w_in, ab_conv_w[j], ab_a_log[j], ab_dt_bias[j],
                            ab_out_norm[j], ab_w_out, j, cb, sb, bsz=bsz, seq=seq)
        else:
            x2d = _mixer_c(x2d, norm_mix[layer], c_w_qkv, c_q_norm[j], c_k_norm[j], c_w_out, j,
                           cc, sc, bsz=bsz, seq=seq)
        x2d = _ffn(x2d, norm_ffn[layer], ffn_w_gate, ffn_w_up, ffn_w_down, layer)
    out = _rmsnorm(x2d, norm_final, x.dtype)
    return out.reshape(bsz, seq, dm)
```

```python
import functools

import jax
import jax.numpy as jnp
from jax import lax
from jax.experimental import pallas as pl
from jax.experimental.pallas import tpu as pltpu

HEAD_DIM = 128
NORM_EPS = 1e-6
L2_EPS = 1e-6
GRID_W = 64
A_CONV_W = 5
A_CHUNK = 64
B_PATTERNS = ((128, 1), (512, 4), (2048, 16))
B_HEADS_PER_GROUP = 8
B_ROT_DIM = HEAD_DIM // 4
B_ROPE_THETA = 500000.0
C_Q_HEADS = 32
C_KV_HEADS = 8
C_REP = C_Q_HEADS // C_KV_HEADS
C_ROPE_THETA = 10000.0
LOG2E = 1.4426950408889634

LANES = 128
VMEM_LIMIT = 56 * 1024 * 1024

BF16 = jnp.bfloat16
F32 = jnp.float32
HI = lax.Precision.HIGHEST


def _cparams(sem, vmem=VMEM_LIMIT):
    return pltpu.CompilerParams(dimension_semantics=sem, vmem_limit_bytes=vmem)


def _rmsnorm_kernel(x_ref, w_ref, o_ref):
    x = x_ref[...].astype(F32)
    ms = jnp.mean(x * x, axis=-1, keepdims=True)
    o_ref[...] = (x * lax.rsqrt(ms + NORM_EPS) * w_ref[...]).astype(o_ref.dtype)


def _rmsnorm(x2d, w, out_dtype, tr=256):
    m, d = x2d.shape
    return pl.pallas_call(
        _rmsnorm_kernel,
        out_shape=jax.ShapeDtypeStruct((m, d), out_dtype),
        grid=(m // tr,),
        in_specs=[pl.BlockSpec((tr, d), lambda i: (i, 0)),
                  pl.BlockSpec((1, d), lambda i: (0, 0))],
        out_specs=pl.BlockSpec((tr, d), lambda i: (i, 0)),
        compiler_params=_cparams(("parallel",)),
        name="rmsnorm",
    )(x2d, w.reshape(1, d).astype(F32))


def _stage_weight_chunk(chunk_ref, w16, ck):
    jp = pl.program_id(0)
    i = pl.program_id(1)
    w16[jp % 2, pl.ds(pl.multiple_of(i * ck, 16), ck), :] = chunk_ref[...].astype(BF16)


def _wmm_kernel(*refs, ksizes, has_res, ck):
    na = len(ksizes)
    a_refs = refs[:na]
    chunk_ref = refs[na]
    r_ref = refs[na + 1] if has_res else None
    o_ref = refs[na + 1 + int(has_res)]
    w16 = refs[na + 2 + int(has_res)]
    jp = pl.program_id(0)
    _stage_weight_chunk(chunk_ref, w16, ck)

    @pl.when(jp > 0)
    def _():
        slot = (jp + 1) % 2
        off = 0
        acc = None
        for a_ref, ks in zip(a_refs, ksizes):
            part = jnp.dot(a_ref[...], w16[slot, off:off + ks, :], preferred_element_type=F32)
            acc = part if acc is None else acc + part
            off += ks
        if has_res:
            acc = acc + r_ref[...]
        o_ref[...] = acc.astype(o_ref.dtype)


def _wmatmul(a_ops, w, *, w_lead, w_row_blk=0, w_col_blk0=0, n, tm, tn, out_dtype, residual=None):
    m = a_ops[0][0].shape[0]
    ksizes = tuple(k for _, _, k in a_ops)
    ktot = sum(ksizes)
    tm, tn = min(tm, m), min(tn, n)
    ni, nj = m // tm, n // tn
    ck = ktot // ni
    assert m % tm == 0 and n % tn == 0 and ktot % ni == 0 and ck % 16 == 0
    row_of = lambda jp, i: jnp.where(jp == 0, 0, i)
    col_of = lambda jp: jnp.maximum(jp - 1, 0)
    in_specs = [pl.BlockSpec((tm, ks), functools.partial(lambda jp, i, blk: (row_of(jp, i), blk), blk=blk))
                for _, blk, ks in a_ops]
    in_specs.append(pl.BlockSpec((None, ck, tn),
                                 lambda jp, i: (w_lead, w_row_blk * ni + i, w_col_blk0 + jnp.minimum(jp, nj - 1))))
    args = [a for a, _, _ in a_ops] + [w]
    if residual is not None:
        in_specs.append(pl.BlockSpec((tm, tn), lambda jp, i: (row_of(jp, i), col_of(jp))))
        args.append(residual)
    return pl.pallas_call(
        functools.partial(_wmm_kernel, ksizes=ksizes, has_res=residual is not None, ck=ck),
        out_shape=jax.ShapeDtypeStruct((m, n), out_dtype),
        grid=(nj + 1, ni),
        in_specs=in_specs,
        out_specs=pl.BlockSpec((tm, tn), lambda jp, i: (row_of(jp, i), col_of(jp))),
        scratch_shapes=[pltpu.VMEM((2, ktot, tn), BF16)],
        compiler_params=_cparams(("arbitrary", "arbitrary")),
        name="proj_matmul",
    )(*args)


def _wmm_t_kernel(a_ref, chunk_ref, o_ref, w16, *, ck):
    jp = pl.program_id(0)
    i = pl.program_id(1)
    w16[jp % 2, :, pl.ds(pl.multiple_of(i * ck, LANES), ck)] = chunk_ref[...].astype(BF16)

    @pl.when(jp > 0)
    def _():
        acc = lax.dot_general(a_ref[...], w16[(jp + 1) % 2], (((1,), (1,)), ((), ())),
                              preferred_element_type=F32)
        o_ref[...] = acc.astype(o_ref.dtype)


def _wmatmul_t(a, wt, *, w_lead, w_row_blk0=0, n, tm, tn, out_dtype):
    m, kdim = a.shape
    tm, tn = min(tm, m), min(tn, n)
    ni, nj = m // tm, n // tn
    ck = kdim // ni
    assert m % tm == 0 and n % tn == 0 and kdim % ni == 0 and ck % LANES == 0
    row_of = lambda jp, i: jnp.where(jp == 0, 0, i)
    col_of = lambda jp: jnp.maximum(jp - 1, 0)
    return pl.pallas_call(
        functools.partial(_wmm_t_kernel, ck=ck),
        out_shape=jax.ShapeDtypeStruct((m, n), out_dtype),
        grid=(nj + 1, ni),
        in_specs=[pl.BlockSpec((tm, kdim), lambda jp, i: (row_of(jp, i), 0)),
                  pl.BlockSpec((None, tn, ck), lambda jp, i: (w_lead, w_row_blk0 + jnp.minimum(jp, nj - 1), i))],
        out_specs=pl.BlockSpec((tm, tn), lambda jp, i: (row_of(jp, i), col_of(jp))),
        scratch_shapes=[pltpu.VMEM((2, tn, kdim), BF16)],
        compiler_params=_cparams(("arbitrary", "arbitrary")),
        name="proj_matmul_wt",
    )(a, wt)


def _gateup_kernel(h_ref, cg_ref, cu_ref, o_ref, wg16, wu16, *, ck):
    jp = pl.program_id(0)
    _stage_weight_chunk(cg_ref, wg16, ck)
    _stage_weight_chunk(cu_ref, wu16, ck)

    @pl.when(jp > 0)
    def _():
        slot = (jp + 1) % 2
        h = h_ref[...]
        g = jnp.dot(h, wg16[slot], preferred_element_type=F32)
        u = jnp.dot(h, wu16[slot], preferred_element_type=F32)
        o_ref[...] = (g * jax.nn.sigmoid(g) * u).astype(o_ref.dtype)


def _gateup(h, w_gate, w_up, layer, *, tm, tn):
    m, kdim = h.shape
    n = w_gate.shape[-1]
    tm, tn = min(tm, m), min(tn, n)
    ni, nj = m // tm, n // tn
    ck = kdim // ni
    assert m % tm == 0 and n % tn == 0 and kdim % ni == 0 and ck % 16 == 0
    wspec = pl.BlockSpec((None, ck, tn), lambda jp, i: (layer, i, jnp.minimum(jp, nj - 1)))
    return pl.pallas_call(
        functools.partial(_gateup_kernel, ck=ck),
        out_shape=jax.ShapeDtypeStruct((m, n), BF16),
        grid=(nj + 1, ni),
        in_specs=[pl.BlockSpec((tm, kdim), lambda jp, i: (jnp.where(jp == 0, 0, i), 0)), wspec, wspec],
        out_specs=pl.BlockSpec((tm, tn), lambda jp, i: (jnp.where(jp == 0, 0, i), jnp.maximum(jp - 1, 0))),
        scratch_shapes=[pltpu.VMEM((2, kdim, tn), BF16), pltpu.VMEM((2, kdim, tn), BF16)],
        compiler_params=_cparams(("arbitrary", "arbitrary")),
        name="ffn_gate_up",
    )(h, w_gate, w_up)


def _shift_cast_kernel(main_ref, next_ref, o_ref, *, shift):
    tr = main_ref.shape[0]
    o_ref[0:tr - shift, :] = main_ref[shift:, :].astype(o_ref.dtype)
    o_ref[tr - shift:, :] = next_ref[...].astype(o_ref.dtype)


def _shift_cast_rows(wt, lead, row0, nrows, *, tr=512, tc=1024):
    _, _, kdim = wt.shape
    base = (row0 // tr) * tr
    shift = row0 - base
    tc = min(tc, kdim)
    assert shift > 0 and shift % 16 == 0 and tr % shift == 0 and nrows % tr == 0 and kdim % tc == 0
    per = tr // shift
    return pl.pallas_call(
        functools.partial(_shift_cast_kernel, shift=shift),
        out_shape=jax.ShapeDtypeStruct((1, nrows, kdim), BF16),
        grid=(nrows // tr, kdim // tc),
        in_specs=[pl.BlockSpec((None, tr, tc), lambda i, j: (lead, base // tr + i, j)),
                  pl.BlockSpec((None, shift, tc), lambda i, j: (lead, (base // tr + i + 1) * per, j))],
        out_specs=pl.BlockSpec((None, tr, tc), lambda i, j: (0, i, j)),
        compiler_params=_cparams(("parallel", "parallel")),
        name="weight_shift_cast",
    )(wt, wt)


_CONV_HALO = 16


def _conv_kernel(prev_ref, cur_ref, next_ref, w_ref, o_ref, ext_ref, *, ts, cw, n_qk_blocks, n_q_blocks):
    i = pl.program_id(1)
    nblk = pl.num_programs(1)
    j = pl.program_id(2)
    pad = A_CONV_W // 2
    prev = prev_ref[0].astype(F32)
    nxt = next_ref[0].astype(F32)
    ext_ref[0:_CONV_HALO, :] = jnp.where(i > 0, prev, 0.0)
    ext_ref[_CONV_HALO:_CONV_HALO + ts, :] = cur_ref[0].astype(F32)
    ext_ref[_CONV_HALO + ts:, :] = jnp.where(i < nblk - 1, nxt, 0.0)
    w = w_ref[...]
    y = jnp.zeros((ts, cw), F32)
    for t in range(A_CONV_W):
        y = y + ext_ref[_CONV_HALO - pad + t:_CONV_HALO - pad + t + ts, :] * w[t:t + 1, :]
    y = y * jax.nn.sigmoid(y)
    is_qk = j < n_qk_blocks
    scale = jnp.where(j < n_q_blocks, HEAD_DIM ** -0.5, 1.0).astype(F32)
    for hh in range(cw // HEAD_DIM):
        yh = y[:, hh * HEAD_DIM:(hh + 1) * HEAD_DIM]
        ss = jnp.sum(yh * yh, axis=-1, keepdims=True)
        yn = yh * (lax.rsqrt(ss + L2_EPS) * scale)
        o_ref[0, :, hh * HEAD_DIM:(hh + 1) * HEAD_DIM] = jnp.where(is_qk, yn, yh).astype(o_ref.dtype)


def _conv_silu_l2(proj_a, conv_w, *, qk_width, q_width, n_ch, ts=512, cw=512):
    bsz, seq, _ = proj_a.shape
    ts = min(ts, seq)
    hb = ts // _CONV_HALO
    nhalo = seq // _CONV_HALO
    w8 = jnp.zeros((8, n_ch), F32).at[:A_CONV_W].set(conv_w.astype(F32))
    kern = functools.partial(_conv_kernel, ts=ts, cw=cw, n_qk_blocks=qk_width // cw, n_q_blocks=q_width // cw)
    return pl.pallas_call(
        kern,
        out_shape=jax.ShapeDtypeStruct((bsz, seq, n_ch), BF16),
        grid=(bsz, seq // ts, n_ch // cw),
        in_specs=[
            pl.BlockSpec((1, _CONV_HALO, cw), lambda b, i, j: (b, jnp.maximum(i * hb - 1, 0), j)),
            pl.BlockSpec((1, ts, cw), lambda b, i, j: (b, i, j)),
            pl.BlockSpec((1, _CONV_HALO, cw), lambda b, i, j: (b, jnp.minimum((i + 1) * hb, nhalo - 1), j)),
            pl.BlockSpec((8, cw), lambda b, i, j: (0, j)),
        ],
        out_specs=pl.BlockSpec((1, ts, cw), lambda b, i, j: (b, i, j)),
        scratch_shapes=[pltpu.VMEM((ts + 2 * _CONV_HALO, cw), F32)],
        compiler_params=_cparams(("parallel", "parallel", "parallel")),
        name="conv_silu_l2norm",
    )(proj_a, proj_a, proj_a, w8)


_INV_BASE = 8


def _split_bf16(x):
    hi = x.astype(BF16)
    lo = (x - hi.astype(F32)).astype(BF16)
    return hi, lo


def _dot3(a, b):
    a_hi, a_lo = a
    b_hi, b_lo = b
    return (jnp.dot(a_hi, b_hi, preferred_element_type=F32)
            + (jnp.dot(a_hi, b_lo, preferred_element_type=F32) + jnp.dot(a_lo, b_hi, preferred_element_type=F32)))


def _delta_kernel(q_ref, k_ref, v_ref, gt_ref, alog_ref, dtb_ref, o_ref, state_ref, *, hg, n_heads):
    c = A_CHUNK
    d = pl.program_id(0) % 2
    grp = pl.program_id(1)
    n = pl.program_id(2)

    @pl.when(n == 0)
    def _():
        state_ref[...] = jnp.zeros_like(state_ref)

    row = lax.broadcasted_iota(jnp.int32, (c, c), 0)
    col = lax.broadcasted_iota(jnp.int32, (c, c), 1)
    ahead = jnp.where(d == 0, row - col, col - row)
    incl = ahead >= 0
    strict = ahead > 0
    tri = jnp.where(incl, 1.0, 0.0).astype(F32)
    eye = jnp.where(row == col, 1.0, 0.0).astype(F32)

    graw = gt_ref[0]
    beta_all = jax.nn.sigmoid(graw)
    z = graw + dtb_ref[0]
    softplus = jnp.maximum(z, 0.0) + jnp.log(1.0 + jnp.exp(-jnp.abs(z)))
    g_all = -jnp.exp(alog_ref[0]) * softplus
    gc_all = jnp.dot(tri, g_all, precision=HI, preferred_element_type=F32)
    gct_all = lax.dot_general(g_all, tri, (((0,), (1,)), ((), ())), precision=HI,
                              preferred_element_type=F32)
    gtot_all = jnp.sum(g_all, axis=0, keepdims=True)
    lane = lax.broadcasted_iota(jnp.int32, (c, LANES), 1)
    subl = lax.broadcasted_iota(jnp.int32, (LANES, c), 0)
    contract_last = (((1,), (1,)), ((), ()))
    contract_first = (((0,), (0,)), ((), ()))

    heads = range(hg)
    sls = [slice(hh * HEAD_DIM, (hh + 1) * HEAD_DIM) for hh in heads]
    q16 = [q_ref[0, :, sl] for sl in sls]
    k16 = [k_ref[0, :, sl] for sl in sls]
    gcol, beta, gtot, decay = [], [], [], []
    for hh in heads:
        a_lane = n_heads + grp * hg + hh
        sel_a = lane == a_lane
        sel_b = lane == (grp * hg + hh)
        gcol.append(jnp.sum(jnp.where(sel_a, gc_all, 0.0), axis=1, keepdims=True))
        beta.append(jnp.sum(jnp.where(sel_b, beta_all, 0.0), axis=1, keepdims=True))
        gtot.append(jnp.sum(jnp.where(sel_a[0:1], gtot_all, 0.0), axis=1, keepdims=True))
        grow = jnp.sum(jnp.where(subl == a_lane, gct_all, 0.0), axis=0, keepdims=True)
        diff = gcol[hh] - grow
        decay.append(jnp.where(incl, jnp.exp(jnp.where(incl, diff, 0.0)), 0.0))
    kb = [k16[hh].astype(F32) * beta[hh] for hh in heads]
    kq = [lax.dot_general(jnp.concatenate([kb[hh].astype(BF16), q16[hh]], axis=0), k16[hh], contract_last,
                          preferred_element_type=F32) for hh in heads]
    attn = [(kq[hh][c:] * decay[hh]).astype(BF16) for hh in heads]
    a_mat = [jnp.where(strict, kq[hh][:c] * decay[hh], 0.0) for hh in heads]
    same_base = (row // _INV_BASE) == (col // _INV_BASE)
    x = [jnp.where(same_base, -a_mat[hh], 0.0) for hh in heads]
    t_mat = [eye + x[hh] for hh in heads]
    p = 2
    while p < _INV_BASE:
        xs = [_split_bf16(x[hh]) for hh in heads]
        x = [_dot3(xs[hh], xs[hh]) for hh in heads]
        t_mat = [t_mat[hh] + _dot3(_split_bf16(t_mat[hh]), _split_bf16(x[hh])) for hh in heads]
        p *= 2
    s = _INV_BASE
    while s < c:
        pair = ((row // (2 * s)) == (col // (2 * s))) & ((row // s) != (col // s))
        ts = [_split_bf16(t_mat[hh]) for hh in heads]
        tl = [_dot3(ts[hh], _split_bf16(jnp.where(pair, a_mat[hh], 0.0))) for hh in heads]
        t_mat = [t_mat[hh] - _dot3(_split_bf16(tl[hh]), ts[hh]) for hh in heads]
        s *= 2
    eg = [jnp.exp(gcol[hh]) for hh in heads]
    rhs = [jnp.concatenate([v_ref[0, :, sls[hh]].astype(F32) * beta[hh], kb[hh] * eg[hh]], axis=1).astype(BF16)
           for hh in heads]
    uw = [jnp.dot(t_mat[hh].astype(BF16), rhs[hh], preferred_element_type=F32) for hh in heads]
    state = [state_ref[hh] for hh in heads]
    ws = [jnp.dot(jnp.concatenate([uw[hh][:, HEAD_DIM:], q16[hh].astype(F32) * eg[hh]], axis=0).astype(BF16),
                  state[hh].astype(BF16), preferred_element_type=F32) for hh in heads]
    vn16 = [(uw[hh][:, :HEAD_DIM] - ws[hh][:c]).astype(BF16) for hh in heads]
    for hh in heads:
        o = ws[hh][c:] + jnp.dot(attn[hh], vn16[hh], preferred_element_type=F32)
        o_ref[0, 0, :, sls[hh]] = o.astype(o_ref.dtype)
    for hh in heads:
        kd = (k16[hh].astype(F32) * jnp.exp(gtot[hh] - gcol[hh])).astype(BF16)
        state_ref[hh] = state[hh] * jnp.exp(gtot[hh]) + lax.dot_general(kd, vn16[hh], contract_first,
                                                                       preferred_element_type=F32)


def _delta_scan(qkv, gates, alog_row, dtb_row, *, n_heads, hg=16):
    bsz, seq, _ = qkv.shape
    c = A_CHUNK
    nchunk = seq // c
    ngrp = n_heads // hg
    wblk = hg * HEAD_DIM

    def cidx(bd, n):
        return jnp.where(bd % 2 == 0, n, nchunk - 1 - n)

    kern = functools.partial(_delta_kernel, hg=hg, n_heads=n_heads)
    return pl.pallas_call(
        kern,
        out_shape=jax.ShapeDtypeStruct((2, bsz, seq, n_heads * HEAD_DIM), F32),
        grid=(bsz * 2, ngrp, nchunk),
        in_specs=[
            pl.BlockSpec((1, c, wblk), lambda bd, g, n: (bd // 2, cidx(bd, n), g)),
            pl.BlockSpec((1, c, wblk), lambda bd, g, n: (bd // 2, cidx(bd, n), ngrp + g)),
            pl.BlockSpec((1, c, wblk), lambda bd, g, n: (bd // 2, cidx(bd, n), 2 * ngrp + g)),
            pl.BlockSpec((1, c, LANES), lambda bd, g, n: (bd // 2, cidx(bd, n), bd % 2)),
            pl.BlockSpec((1, 1, LANES), lambda bd, g, n: (bd % 2, 0, 0)),
            pl.BlockSpec((1, 1, LANES), lambda bd, g, n: (bd % 2, 0, 0)),
        ],
        out_specs=pl.BlockSpec((1, 1, c, wblk), lambda bd, g, n: (bd % 2, bd // 2, cidx(bd, n), g)),
        scratch_shapes=[pltpu.VMEM((hg, HEAD_DIM, HEAD_DIM), F32)],
        compiler_params=_cparams(("parallel", "parallel", "arbitrary")),
        name="gated_delta_scan",
    )(qkv, qkv, qkv, gates, alog_row, dtb_row)


def _delta_out_kernel(of_ref, ob_ref, z_ref, w_ref, o_ref, *, cw):
    o = of_ref[0] + ob_ref[0]
    z = z_ref[...].astype(F32)
    w = w_ref[...]
    for hh in range(cw // HEAD_DIM):
        sl = slice(hh * HEAD_DIM, (hh + 1) * HEAD_DIM)
        oh = o[:, sl]
        ms = jnp.mean(oh * oh, axis=-1, keepdims=True)
        zh = z[:, sl]
        o_ref[:, sl] = (oh * lax.rsqrt(ms + NORM_EPS) * w * (zh * jax.nn.sigmoid(zh))).astype(o_ref.dtype)


def _delta_out(o2, proj_a, out_norm, *, z_col_block, tr=512, cw=512):
    _, t, vdim = o2.shape
    tr = min(tr, t)
    return pl.pallas_call(
        functools.partial(_delta_out_kernel, cw=cw),
        out_shape=jax.ShapeDtypeStruct((t, vdim), BF16),
        grid=(t // tr, vdim // cw),
        in_specs=[
            pl.BlockSpec((1, tr, cw), lambda i, j: (0, i, j)),
            pl.BlockSpec((1, tr, cw), lambda i, j: (1, i, j)),
            pl.BlockSpec((tr, cw), lambda i, j: (i, z_col_block + j)),
            pl.BlockSpec((1, HEAD_DIM), lambda i, j: (0, 0)),
        ],
        out_specs=pl.BlockSpec((tr, cw), lambda i, j: (i, j)),
        compiler_params=_cparams(("parallel", "parallel")),
        name="delta_out_norm_gate",
    )(o2, o2, proj_a, out_norm.reshape(1, HEAD_DIM).astype(F32))


def _rotate_pairs(x, half):
    lane = lax.broadcasted_iota(jnp.int32, x.shape, 1)
    first = (lane % (2 * half)) < half
    return jnp.where(first, pltpu.roll(x, LANES - half, 1), pltpu.roll(x, half, 1))


def _deinterleave_matrix(tr, dil, transpose=False):
    per = tr // dil
    i_out = lax.broadcasted_iota(jnp.int32, (tr, tr), 1 if transpose else 0)
    i_in = lax.broadcasted_iota(jnp.int32, (tr, tr), 0 if transpose else 1)
    src = (i_out % per) * dil + i_out // per
    return jnp.where(i_in == src, 1.0, 0.0).astype(BF16)


def _rope_deint_kernel(q_ref, k_ref, v_ref, cos_ref, sin_ref, o_ref, *, tr, dil, gw):
    cos = cos_ref[...]
    sin = sin_ref[...]
    per = tr // dil
    perm = _deinterleave_matrix(tr, dil) if dil > 1 else None

    def emit(y16, col0):
        if dil > 1:
            y16 = jnp.dot(perm, y16, preferred_element_type=F32).astype(BF16)
        for r in range(dil):
            o_ref[0, r, :, col0:col0 + HEAD_DIM] = y16[r * per:(r + 1) * per, :]

    for part, ref in enumerate((q_ref, k_ref)):
        for hh in range(gw // HEAD_DIM):
            sl = slice(hh * HEAD_DIM, (hh + 1) * HEAD_DIM)
            x = ref[:, sl].astype(F32)
            y = x * cos + _rotate_pairs(x, B_ROT_DIM // 2) * sin
            emit(y.astype(BF16), part * gw + hh * HEAD_DIM)
    for hh in range(gw // HEAD_DIM):
        emit(v_ref[:, hh * HEAD_DIM:(hh + 1) * HEAD_DIM], 2 * gw + hh * HEAD_DIM)


def _rope_deint(proj_b, cos_t, sin_t, gi, dil, *, bsz, seq, n_groups, tr=256):
    gw = B_HEADS_PER_GROUP * HEAD_DIM
    nsb = seq // tr
    per = tr // dil
    return pl.pallas_call(
        functools.partial(_rope_deint_kernel, tr=tr, dil=dil, gw=gw),
        out_shape=jax.ShapeDtypeStruct((bsz, dil, seq // dil, 3 * gw), BF16),
        grid=(bsz, nsb),
        in_specs=[pl.BlockSpec((tr, gw), lambda b, i: (b * nsb + i, gi)),
                  pl.BlockSpec((tr, gw), lambda b, i: (b * nsb + i, n_groups + gi)),
                  pl.BlockSpec((tr, gw), lambda b, i: (b * nsb + i, 2 * n_groups + gi)),
                  pl.BlockSpec((tr, HEAD_DIM), lambda b, i: (i, 0)),
                  pl.BlockSpec((tr, HEAD_DIM), lambda b, i: (i, 0))],
        out_specs=pl.BlockSpec((1, dil, per, 3 * gw), lambda b, i: (b, 0, i, 0)),
        compiler_params=_cparams(("parallel", "parallel")),
        name=f"rope_deinterleave_d{dil}",
    )(proj_b, proj_b, proj_b, cos_t, sin_t)


_BW = 64
_BQ = 128


def _dil_kernel(q_ref, kc_ref, kp_ref, kn_ref, vc_ref, vp_ref, vn_ref, o_ref, l_ref, kext, vext,
                *, qb, nh, seq_len):
    i = pl.program_id(2)
    base = i * qb
    kext[0:_BW, :] = kp_ref[0, 0]
    kext[_BW:_BW + qb, :] = kc_ref[0, 0]
    kext[_BW + qb:, :] = kn_ref[0, 0]
    vext[0:_BW, :] = vp_ref[0, 0]
    vext[_BW:_BW + qb, :] = vc_ref[0, 0]
    vext[_BW + qb:, :] = vn_ref[0, 0]
    scale = HEAD_DIM ** -0.5
    nkeys = _BQ + 2 * _BW
    qi = lax.broadcasted_iota(jnp.int32, (_BQ, nkeys), 0)
    kj = lax.broadcasted_iota(jnp.int32, (_BQ, nkeys), 1)
    lane = lax.broadcasted_iota(jnp.int32, (_BQ, LANES), 1)
    for sb in range(qb // _BQ):
        qpos = base + sb * _BQ + qi
        kpos = base + sb * _BQ - _BW + kj
        valid = (jnp.abs(kpos - qpos) <= _BW) & (kpos >= 0) & (kpos < seq_len)
        lse_blk = jnp.zeros((_BQ, LANES), F32)
        for hh in range(nh):
            sl = slice(hh * HEAD_DIM, (hh + 1) * HEAD_DIM)
            q = q_ref[0, 0, sb * _BQ:(sb + 1) * _BQ, sl]
            kx = kext[sb * _BQ:sb * _BQ + nkeys, sl]
            vx = vext[sb * _BQ:sb * _BQ + nkeys, sl]
            s = lax.dot_general(q, kx, (((1,), (1,)), ((), ())), preferred_element_type=F32) * scale
            s = jnp.where(valid, s, -1e30)
            m = jnp.max(s, axis=-1, keepdims=True)
            e = jnp.where(valid, jnp.exp(s - m), 0.0)
            l = jnp.sum(e, axis=-1, keepdims=True)
            p = (e / l).astype(BF16)
            o = jnp.dot(p, vx, preferred_element_type=F32)
            o_ref[0, 0, sb * _BQ:(sb + 1) * _BQ, sl] = o.astype(o_ref.dtype)
            lse_blk = jnp.where(lane == hh, m + jnp.log(l), lse_blk)
        l_ref[0, 0, sb * _BQ:(sb + 1) * _BQ, :] = lse_blk


def _dilated_group(grp_qkv):
    bsz, dil, ln, _ = grp_qkv.shape
    gw = B_HEADS_PER_GROUP * HEAD_DIM
    qb = min(512, ln)
    hb = qb // _BW
    nhalo = ln // _BW
    lo = lambda i: jnp.maximum(i * hb - 1, 0)
    hi = lambda i: jnp.minimum((i + 1) * hb, nhalo - 1)

    def cur(col):
        return pl.BlockSpec((1, 1, qb, gw), lambda b, r, i: (b, r, i, col))

    def halo(col, f):
        return pl.BlockSpec((1, 1, _BW, gw), lambda b, r, i: (b, r, f(i), col))

    return pl.pallas_call(
        functools.partial(_dil_kernel, qb=qb, nh=B_HEADS_PER_GROUP, seq_len=ln),
        out_shape=(jax.ShapeDtypeStruct((bsz, dil, ln, gw), BF16),
                   jax.ShapeDtypeStruct((bsz, dil, ln, LANES), F32)),
        grid=(bsz, dil, ln // qb),
        in_specs=[cur(0), cur(1), halo(1, lo), halo(1, hi), cur(2), halo(2, lo), halo(2, hi)],
        out_specs=(pl.BlockSpec((1, 1, qb, gw), lambda b, r, i: (b, r, i, 0)),
                   pl.BlockSpec((1, 1, qb, LANES), lambda b, r, i: (b, r, i, 0))),
        scratch_shapes=[pltpu.VMEM((qb + 2 * _BW, gw), BF16), pltpu.VMEM((qb + 2 * _BW, gw), BF16)],
        compiler_params=_cparams(("parallel", "parallel", "parallel")),
        name=f"dilated_attn_d{dil}",
    )(*([grp_qkv] * 7))


def _merge_kernel(*refs, tr, dils, nh):
    ng = len(dils)
    o_refs = refs[:ng]
    l_refs = refs[ng:2 * ng]
    out_ref = refs[2 * ng]
    outs, lses = [], []
    for g, dil in enumerate(dils):
        per = tr // dil
        if dil == 1:
            outs.append(o_refs[g][0, 0].astype(F32))
            lses.append(l_refs[g][0, 0])
            continue
        pt = _deinterleave_matrix(tr, dil, transpose=True)
        y = jnp.concatenate([o_refs[g][0, r] for r in range(dil)], axis=0)
        outs.append(jnp.dot(pt, y, preferred_element_type=F32))
        ls = jnp.concatenate([l_refs[g][0, r] for r in range(dil)], axis=0)
        h1 = ls.astype(BF16)
        r1 = ls - h1.astype(F32)
        h2 = r1.astype(BF16)
        h3 = (r1 - h2.astype(F32)).astype(BF16)
        lses.append(jnp.dot(pt, h1, preferred_element_type=F32)
                    + (jnp.dot(pt, h2, preferred_element_type=F32) + jnp.dot(pt, h3, preferred_element_type=F32)))
    mx = lses[0]
    for g in range(1, ng):
        mx = jnp.maximum(mx, lses[g])
    es = [jnp.exp(l - mx) for l in lses]
    den = es[0]
    for g in range(1, ng):
        den = den + es[g]
    ws = [e / den for e in es]
    lane = lax.broadcasted_iota(jnp.int32, (tr, LANES), 1)
    for hh in range(nh):
        sl = slice(hh * HEAD_DIM, (hh + 1) * HEAD_DIM)
        acc = None
        for g in range(ng):
            wcol = jnp.sum(jnp.where(lane == hh, ws[g], 0.0), axis=1, keepdims=True)
            term = outs[g][:, sl] * wcol
            acc = term if acc is None else acc + term
        out_ref[:, sl] = acc.astype(out_ref.dtype)


def _merge_groups(os_, ls_, *, bsz, seq, tr=256):
    gw = B_HEADS_PER_GROUP * HEAD_DIM
    dils = tuple(o.shape[1] for o in os_)
    nsb = seq // tr
    in_specs = ([pl.BlockSpec((1, d, tr // d, gw), lambda b, i: (b, 0, i, 0)) for d in dils]
                + [pl.BlockSpec((1, d, tr // d, LANES), lambda b, i: (b, 0, i, 0)) for d in dils])
    return pl.pallas_call(
        functools.partial(_merge_kernel, tr=tr, dils=dils, nh=B_HEADS_PER_GROUP),
        out_shape=jax.ShapeDtypeStruct((bsz * seq, gw), BF16),
        grid=(bsz, nsb),
        in_specs=in_specs,
        out_specs=pl.BlockSpec((tr, gw), lambda b, i: (b * nsb + i, 0)),
        compiler_params=_cparams(("parallel", "parallel")),
        name="dilated_merge",
    )(*os_, *ls_)


def _qknorm_rope_kernel(x_ref, wq_ref, wk_ref, cos_ref, sin_ref, o_ref, *, cw, n_q_blocks):
    j = pl.program_id(1)
    is_q = j < n_q_blocks
    w = jnp.where(is_q, wq_ref[...], wk_ref[...])
    scale = jnp.where(is_q, HEAD_DIM ** -0.5 * LOG2E, 1.0).astype(F32)
    cos = cos_ref[...]
    sin = sin_ref[...]
    for hh in range(cw // HEAD_DIM):
        sl = slice(hh * HEAD_DIM, (hh + 1) * HEAD_DIM)
        x = x_ref[:, sl].astype(F32)
        ms = jnp.mean(x * x, axis=-1, keepdims=True)
        y = x * lax.rsqrt(ms + NORM_EPS) * w
        y = y * cos + _rotate_pairs(y, HEAD_DIM // 4) * sin
        o_ref[:, sl] = (y * scale).astype(o_ref.dtype)


def _qknorm_rope(qkv, q_norm, k_norm, cos_t, sin_t, *, q_width, qk_width, seq, tr=512, cw=512):
    t = qkv.shape[0]
    tr = min(tr, seq)
    nsb = seq // tr
    return pl.pallas_call(
        functools.partial(_qknorm_rope_kernel, cw=cw, n_q_blocks=q_width // cw),
        out_shape=jax.ShapeDtypeStruct((t, qk_width), BF16),
        grid=(t // tr, qk_width // cw),
        in_specs=[pl.BlockSpec((tr, cw), lambda i, j: (i, j)),
                  pl.BlockSpec((1, HEAD_DIM), lambda i, j: (0, 0)),
                  pl.BlockSpec((1, HEAD_DIM), lambda i, j: (0, 0)),
                  pl.BlockSpec((tr, HEAD_DIM), lambda i, j: (i % nsb, 0)),
                  pl.BlockSpec((tr, HEAD_DIM), lambda i, j: (i % nsb, 0))],
        out_specs=pl.BlockSpec((tr, cw), lambda i, j: (i, j)),
        compiler_params=_cparams(("parallel", "parallel")),
        name="qk_norm_axial_rope",
    )(qkv, q_norm.reshape(1, HEAD_DIM).astype(F32), k_norm.reshape(1, HEAD_DIM).astype(F32), cos_t, sin_t)


def _flash_kernel(q_ref, k_ref, v_ref, o_ref, qs_ref, v1_ref, m_ref, acc_ref, *, tq, tk, rep, seq):
    @pl.when(pl.program_id(2) == 0)
    def _():
        v1_ref[:, :HEAD_DIM] = v_ref[0]
        v1_ref[:, HEAD_DIM:] = jnp.ones((seq, HEAD_DIM), BF16)

    for r in range(rep):
        qs_ref[r * tq:(r + 1) * tq, :] = q_ref[0, :, r * HEAD_DIM:(r + 1) * HEAD_DIM]
    m_ref[...] = jnp.full(m_ref.shape, -1e30, F32)
    acc_ref[...] = jnp.zeros(acc_ref.shape, F32)

    def body(t, carry):
        off = pl.multiple_of(t * tk, tk)
        kt = k_ref[0, pl.ds(off, tk), :]
        vt = v1_ref[pl.ds(off, tk), :]
        s = [lax.dot_general(qs_ref[r * tq:(r + 1) * tq, :], kt, (((1,), (1,)), ((), ())),
                             preferred_element_type=F32) for r in range(rep)]
        for r in range(rep):
            rows = slice(r * tq, (r + 1) * tq)
            m_old = m_ref[rows, :]
            m_new = jnp.maximum(m_old, jnp.max(s[r], axis=-1, keepdims=True))
            p = jnp.exp2(s[r] - m_new).astype(BF16)
            pv = jnp.dot(p, vt, preferred_element_type=F32)
            acc_ref[rows, :] = acc_ref[rows, :] * jnp.exp2(m_old - m_new) + pv
            m_ref[rows, :] = m_new
        return carry

    lax.fori_loop(0, seq // tk, body, 0)
    for r in range(rep):
        a = acc_ref[r * tq:(r + 1) * tq, :]
        o_ref[0, :, r * HEAD_DIM:(r + 1) * HEAD_DIM] = (a[:, :HEAD_DIM] / a[:, HEAD_DIM:]).astype(o_ref.dtype)


def _flash_attention(qk, qkv, *, bsz, seq, tq=256, tk=4096):
    qw = C_Q_HEADS * HEAD_DIM
    tq, tk = min(tq, seq), min(tk, seq)
    gq = C_REP * HEAD_DIM
    qk3 = qk.reshape(bsz, seq, qk.shape[-1])
    qkv3 = qkv.reshape(bsz, seq, qkv.shape[-1])
    k_blk0 = qw // HEAD_DIM
    v_blk0 = (qw + C_KV_HEADS * HEAD_DIM) // HEAD_DIM
    out = pl.pallas_call(
        functools.partial(_flash_kernel, tq=tq, tk=tk, rep=C_REP, seq=seq),
        out_shape=jax.ShapeDtypeStruct((bsz, seq, qw), BF16),
        grid=(bsz, C_KV_HEADS, seq // tq),
        in_specs=[pl.BlockSpec((1, tq, gq), lambda b, g, i: (b, i, g)),
                  pl.BlockSpec((1, seq, HEAD_DIM), lambda b, g, i: (b, 0, k_blk0 + g)),
                  pl.BlockSpec((1, seq, HEAD_DIM), lambda b, g, i: (b, 0, v_blk0 + g))],
        out_specs=pl.BlockSpec((1, tq, gq), lambda b, g, i: (b, i, g)),
        scratch_shapes=[pltpu.VMEM((C_REP * tq, HEAD_DIM), BF16),
                        pltpu.VMEM((seq, 2 * HEAD_DIM), BF16),
                        pltpu.VMEM((C_REP * tq, 1), F32),
                        pltpu.VMEM((C_REP * tq, 2 * HEAD_DIM), F32)],
        compiler_params=_cparams(("parallel", "parallel", "arbitrary")),
        name="gqa_flash_attention",
    )(qk3, qk3, qkv3)
    return out.reshape(bsz * seq, qw)


def _rope_cos_sin(pos, dim, theta):
    inv = 1.0 / (theta ** (jnp.arange(0, dim, 2, dtype=F32) / dim))
    ang = pos.astype(F32)[:, None] * inv[None, :]
    return jnp.cos(ang), jnp.sin(ang)


def _tables(seq):
    tok = jnp.arange(seq)
    cos_b, sin_b = _rope_cos_sin(tok, B_ROT_DIM, B_ROPE_THETA)
    ones = jnp.ones((seq, HEAD_DIM - B_ROT_DIM), F32)
    cb = jnp.concatenate([cos_b, cos_b, ones], axis=-1)
    sb = jnp.concatenate([-sin_b, sin_b, jnp.zeros_like(ones)], axis=-1)
    cos_r, sin_r = _rope_cos_sin(tok // GRID_W, HEAD_DIM // 2, C_ROPE_THETA)
    cos_c, sin_c = _rope_cos_sin(tok % GRID_W, HEAD_DIM // 2, C_ROPE_THETA)
    cc = jnp.concatenate([cos_r, cos_r, cos_c, cos_c], axis=-1)
    sc = jnp.concatenate([-sin_r, sin_r, -sin_c, sin_c], axis=-1)
    return cb, sb, cc, sc


def _ffn(x2d, norm_w, w_gate, w_up, w_down, layer):
    h = _rmsnorm(x2d, norm_w, BF16)
    act = _gateup(h, w_gate, w_up, layer, tm=2048, tn=256)
    dff = act.shape[1]
    nsplit = 2 if (dff // 2) % LANES == 0 else 1
    for kb in range(nsplit):
        x2d = _wmatmul([(act, kb, dff // nsplit)], w_down, w_lead=layer, w_row_blk=kb, n=x2d.shape[1],
                       tm=1024, tn=512, out_dtype=F32, residual=x2d)
    return x2d


def _mixer_ab(x2d, norm_w, w_in, conv_w, a_log, dt_bias, out_norm, w_out, j, cb, sb, *, bsz, seq):
    t, dm = x2d.shape
    n_heads = a_log.shape[-1]
    a_qk = n_heads * HEAD_DIM
    a_v = n_heads * HEAD_DIM
    a_qkv = 2 * a_qk + a_v
    ngate = 2 * n_heads
    b_w = (w_in.shape[-1] - a_qkv - a_v - 2 * ngate) // 3
    h = _rmsnorm(x2d, norm_w, BF16)

    w_in_t = jnp.swapaxes(w_in, 1, 2)
    g0 = a_qkv + a_v
    w_beta = w_in_t[j, g0:g0 + ngate].reshape(2, n_heads, dm)
    w_alpha = w_in_t[j, g0 + ngate:g0 + 2 * ngate].reshape(2, n_heads, dm)
    w_g = jnp.concatenate([w_beta, w_alpha, jnp.zeros((2, LANES - 2 * n_heads, dm), F32)], axis=1)
    w_g = w_g.reshape(1, 2 * LANES, dm)
    w_b = _shift_cast_rows(w_in_t, j, g0 + 2 * ngate, 3 * b_w)

    proj_a = _wmatmul_t(h, w_in_t, w_lead=j, n=a_qkv + a_v, tm=2048, tn=512, out_dtype=BF16)
    gates = _wmatmul_t(h, w_g, w_lead=0, n=2 * LANES, tm=2048, tn=2 * LANES, out_dtype=F32)
    proj_b = _wmatmul_t(h, w_b, w_lead=0, n=3 * b_w, tm=2048, tn=512, out_dtype=BF16)

    qkv = _conv_silu_l2(proj_a.reshape(bsz, seq, -1), conv_w, qk_width=2 * a_qk, q_width=a_qk, n_ch=a_qkv)
    pad = jnp.zeros((2, LANES - 2 * n_heads), F32)
    alog_row = jnp.concatenate([jnp.zeros((2, n_heads), F32), a_log.astype(F32), pad], axis=-1).reshape(2, 1, LANES)
    dtb_row = jnp.concatenate([jnp.zeros((2, n_heads), F32), dt_bias.astype(F32), pad], axis=-1).reshape(2, 1, LANES)
    o2 = _delta_scan(qkv, gates.reshape(bsz, seq, 2 * LANES), alog_row, dtb_row, n_heads=n_heads)
    o_a = _delta_out(o2.reshape(2, t, a_v), proj_a, out_norm, z_col_block=a_qkv // 512)

    n_groups = len(B_PATTERNS)
    os_, ls_ = [], []
    for gi, (_, dil) in enumerate(B_PATTERNS):
        grp = _rope_deint(proj_b, cb, sb, gi, dil, bsz=bsz, seq=seq, n_groups=n_groups)
        o_g, l_g = _dilated_group(grp)
        os_.append(o_g)
        ls_.append(l_g)
    o_b = _merge_groups(os_, ls_, bsz=bsz, seq=seq)

    return _wmatmul([(o_a, 0, a_v), (o_b, 0, o_b.shape[1])], w_out, w_lead=j, n=dm, tm=1024, tn=512,
                    out_dtype=F32, residual=x2d)


def _mixer_c(x2d, norm_w, w_qkv, q_norm, k_norm, w_out, j, cc, sc, *, bsz, seq):
    dm = x2d.shape[1]
    h = _rmsnorm(x2d, norm_w, BF16)
    qkv = _wmatmul([(h, 0, dm)], w_qkv, w_lead=j, n=w_qkv.shape[-1], tm=2048, tn=512, out_dtype=BF16)
    qw = C_Q_HEADS * HEAD_DIM
    kw = C_KV_HEADS * HEAD_DIM
    qk = _qknorm_rope(qkv, q_norm, k_norm, cc, sc, q_width=qw, qk_width=qw + kw, seq=seq)
    o = _flash_attention(qk, qkv, bsz=bsz, seq=seq)
    return _wmatmul([(o, 0, qw)], w_out, w_lead=j, n=dm, tm=1024, tn=512, out_dtype=F32, residual=x2d)


def kernel(x, norm_mix, norm_ffn, norm_final, ab_w_in, ab_conv_w, ab_a_log, ab_dt_bias, ab_out_norm, ab_w_out,
           c_w_qkv, c_q_norm, c_k_norm, c_w_out, ffn_w_gate, ffn_w_up, ffn_w_down):
    bsz, seq, dm = x.shape
    depth = norm_mix.shape[0]
    cb, sb, cc, sc = _tables(seq)
    x2d = x.reshape(bsz * seq, dm)
    for layer in range(depth):
        j = layer // 2
        if layer % 2 == 0:
            x2d = _mixer_ab(x2d, norm_mix[layer], ab_w_in, ab_conv_w[j], ab_a_log[j], ab_dt_bias[j],
                            ab_out_norm[j], ab_w_out, j, cb, sb, bsz=bsz, seq=seq)
        else:
            x2d = _mixer_c(x2d, norm_mix[layer], c_w_qkv, c_q_norm[j], c_k_norm[j], c_w_out, j,
                           cc, sc, bsz=bsz, seq=seq)
        x2d = _ffn(x2d, norm_ffn[layer], ffn_w_gate, ffn_w_up, ffn_w_down, layer)
    out = _rmsnorm(x2d, norm_final, x.dtype)
    return out.reshape(bsz, seq, dm)
```

```python
import functools

import jax
import jax.numpy as jnp
from jax import lax
from jax.experimental import pallas as pl
from jax.experimental.pallas import tpu as pltpu

HEAD_DIM = 128
NORM_EPS = 1e-6
L2_EPS = 1e-6
GRID_W = 64
A_CONV_W = 5
A_CHUNK = 64
B_PATTERNS = ((128, 1), (512, 4), (2048, 16))
B_HEADS_PER_GROUP = 8
B_ROT_DIM = HEAD_DIM // 4
B_ROPE_THETA = 500000.0
C_Q_HEADS = 32
C_KV_HEADS = 8
C_REP = C_Q_HEADS // C_KV_HEADS
C_ROPE_THETA = 10000.0
LOG2E = 1.4426950408889634

LANES = 128
VMEM_LIMIT = 56 * 1024 * 1024

BF16 = jnp.bfloat16
F32 = jnp.float32
HI = lax.Precision.HIGHEST


def _cparams(sem, vmem=VMEM_LIMIT):
    return pltpu.CompilerParams(dimension_semantics=sem, vmem_limit_bytes=vmem)


def _rmsnorm_kernel(x_ref, w_ref, o_ref):
    x = x_ref[...].astype(F32)
    ms = jnp.mean(x * x, axis=-1, keepdims=True)
    o_ref[...] = (x * lax.rsqrt(ms + NORM_EPS) * w_ref[...]).astype(o_ref.dtype)


def _rmsnorm(x2d, w, out_dtype, tr=256):
    m, d = x2d.shape
    return pl.pallas_call(
        _rmsnorm_kernel,
        out_shape=jax.ShapeDtypeStruct((m, d), out_dtype),
        grid=(m // tr,),
        in_specs=[pl.BlockSpec((tr, d), lambda i: (i, 0)),
                  pl.BlockSpec((1, d), lambda i: (0, 0))],
        out_specs=pl.BlockSpec((tr, d), lambda i: (i, 0)),
        compiler_params=_cparams(("parallel",)),
        name="rmsnorm",
    )(x2d, w.reshape(1, d).astype(F32))


def _stage_weight_chunk(chunk_ref, w16, ck):
    jp = pl.program_id(0)
    i = pl.program_id(1)
    w16[jp % 2, pl.ds(pl.multiple_of(i * ck, 16), ck), :] = chunk_ref[...].astype(BF16)


def _wmm_kernel(*refs, ksizes, has_res, ck):
    na = len(ksizes)
    a_refs = refs[:na]
    chunk_ref = refs[na]
    r_ref = refs[na + 1] if has_res else None
    o_ref = refs[na + 1 + int(has_res)]
    w16 = refs[na + 2 + int(has_res)]
    jp = pl.program_id(0)
    _stage_weight_chunk(chunk_ref, w16, ck)

    @pl.when(jp > 0)
    def _():
        slot = (jp + 1) % 2
        off = 0
        acc = None
        for a_ref, ks in zip(a_refs, ksizes):
            part = jnp.dot(a_ref[...], w16[slot, off:off + ks, :], preferred_element_type=F32)
            acc = part if acc is None else acc + part
            off += ks
        if has_res:
            acc = acc + r_ref[...]
        o_ref[...] = acc.astype(o_ref.dtype)


def _wmatmul(a_ops, w, *, w_lead, w_row_blk=0, w_col_blk0=0, n, tm, tn, out_dtype, residual=None):
    m = a_ops[0][0].shape[0]
    ksizes = tuple(k for _, _, k in a_ops)
    ktot = sum(ksizes)
    tm, tn = min(tm, m), min(tn, n)
    ni, nj = m // tm, n // tn
    ck = ktot // ni
    assert m % tm == 0 and n % tn == 0 and ktot % ni == 0 and ck % 16 == 0
    row_of = lambda jp, i: jnp.where(jp == 0, 0, i)
    col_of = lambda jp: jnp.maximum(jp - 1, 0)
    in_specs = [pl.BlockSpec((tm, ks), functools.partial(lambda jp, i, blk: (row_of(jp, i), blk), blk=blk))
                for _, blk, ks in a_ops]
    in_specs.append(pl.BlockSpec((None, ck, tn),
                                 lambda jp, i: (w_lead, w_row_blk * ni + i, w_col_blk0 + jnp.minimum(jp, nj - 1))))
    args = [a for a, _, _ in a_ops] + [w]
    if residual is not None:
        in_specs.append(pl.BlockSpec((tm, tn), lambda jp, i: (row_of(jp, i), col_of(jp))))
        args.append(residual)
    return pl.pallas_call(
        functools.partial(_wmm_kernel, ksizes=ksizes, has_res=residual is not None, ck=ck),
        out_shape=jax.ShapeDtypeStruct((m, n), out_dtype),
        grid=(nj + 1, ni),
        in_specs=in_specs,
        out_specs=pl.BlockSpec((tm, tn), lambda jp, i: (row_of(jp, i), col_of(jp))),
        scratch_shapes=[pltpu.VMEM((2, ktot, tn), BF16)],
        compiler_params=_cparams(("arbitrary", "arbitrary")),
        name="proj_matmul",
    )(*args)


def _wmm_t_kernel(a_ref, chunk_ref, o_ref, w16, *, ck):
    jp = pl.program_id(0)
    i = pl.program_id(1)
    w16[jp % 2, :, pl.ds(pl.multiple_of(i * ck, LANES), ck)] = chunk_ref[...].astype(BF16)

    @pl.when(jp > 0)
    def _():
        acc = lax.dot_general(a_ref[...], w16[(jp + 1) % 2], (((1,), (1,)), ((), ())),
                              preferred_element_type=F32)
        o_ref[...] = acc.astype(o_ref.dtype)


def _wmatmul_t(a, wt, *, w_lead, w_row_blk0=0, n, tm, tn, out_dtype):
    m, kdim = a.shape
    tm, tn = min(tm, m), min(tn, n)
    ni, nj = m // tm, n // tn
    ck = kdim // ni
    assert m % tm == 0 and n % tn == 0 and kdim % ni == 0 and ck % LANES == 0
    row_of = lambda jp, i: jnp.where(jp == 0, 0, i)
    col_of = lambda jp: jnp.maximum(jp - 1, 0)
    return pl.pallas_call(
        functools.partial(_wmm_t_kernel, ck=ck),
        out_shape=jax.ShapeDtypeStruct((m, n), out_dtype),
        grid=(nj + 1, ni),
        in_specs=[pl.BlockSpec((tm, kdim), lambda jp, i: (row_of(jp, i), 0)),
                  pl.BlockSpec((None, tn, ck), lambda jp, i: (w_lead, w_row_blk0 + jnp.minimum(jp, nj - 1), i))],
        out_specs=pl.BlockSpec((tm, tn), lambda jp, i: (row_of(jp, i), col_of(jp))),
        scratch_shapes=[pltpu.VMEM((2, tn, kdim), BF16)],
        compiler_params=_cparams(("arbitrary", "arbitrary")),
        name="proj_matmul_wt",
    )(a, wt)


def _gateup_kernel(h_ref, cg_ref, cu_ref, o_ref, wg16, wu16, *, ck):
    jp = pl.program_id(0)
    _stage_weight_chunk(cg_ref, wg16, ck)
    _stage_weight_chunk(cu_ref, wu16, ck)

    @pl.when(jp > 0)
    def _():
        slot = (jp + 1) % 2
        h = h_ref[...]
        g = jnp.dot(h, wg16[slot], preferred_element_type=F32)
        u = jnp.dot(h, wu16[slot], preferred_element_type=F32)
        o_ref[...] = (g * jax.nn.sigmoid(g) * u).astype(o_ref.dtype)


def _gateup(h, w_gate, w_up, layer, *, tm, tn):
    m, kdim = h.shape
    n = w_gate.shape[-1]
    tm, tn = min(tm, m), min(tn, n)
    ni, nj = m // tm, n // tn
    ck = kdim // ni
    assert m % tm == 0 and n % tn == 0 and kdim % ni == 0 and ck % 16 == 0
    wspec = pl.BlockSpec((None, ck, tn), lambda jp, i: (layer, i, jnp.minimum(jp, nj - 1)))
    return pl.pallas_call(
        functools.partial(_gateup_kernel, ck=ck),
        out_shape=jax.ShapeDtypeStruct((m, n), BF16),
        grid=(nj + 1, ni),
        in_specs=[pl.BlockSpec((tm, kdim), lambda jp, i: (jnp.where(jp == 0, 0, i), 0)), wspec, wspec],
        out_specs=pl.BlockSpec((tm, tn), lambda jp, i: (jnp.where(jp == 0, 0, i), jnp.maximum(jp - 1, 0))),
        scratch_shapes=[pltpu.VMEM((2, kdim, tn), BF16), pltpu.VMEM((2, kdim, tn), BF16)],
        compiler_params=_cparams(("arbitrary", "arbitrary")),
        name="ffn_gate_up",
    )(h, w_gate, w_up)


def _shift_cast_kernel(main_ref, next_ref, o_ref, *, shift):
    tr = main_ref.shape[0]
    o_ref[0:tr - shift, :] = main_ref[shift:, :].astype(o_ref.dtype)
    o_ref[tr - shift:, :] = next_ref[...].astype(o_ref.dtype)


def _shift_cast_rows(wt, lead, row0, nrows, *, tr=512, tc=1024):
    _, _, kdim = wt.shape
    base = (row0 // tr) * tr
    shift = row0 - base
    tc = min(tc, kdim)
    assert shift > 0 and shift % 16 == 0 and tr % shift == 0 and nrows % tr == 0 and kdim % tc == 0
    per = tr // shift
    return pl.pallas_call(
        functools.partial(_shift_cast_kernel, shift=shift),
        out_shape=jax.ShapeDtypeStruct((1, nrows, kdim), BF16),
        grid=(nrows // tr, kdim // tc),
        in_specs=[pl.BlockSpec((None, tr, tc), lambda i, j: (lead, base // tr + i, j)),
                  pl.BlockSpec((None, shift, tc), lambda i, j: (lead, (base // tr + i + 1) * per, j))],
        out_specs=pl.BlockSpec((None, tr, tc), lambda i, j: (0, i, j)),
        compiler_params=_cparams(("parallel", "parallel")),
        name="weight_shift_cast",
    )(wt, wt)


_CONV_HALO = 16


def _conv_kernel(prev_ref, cur_ref, next_ref, w_ref, o_ref, ext_ref, *, ts, cw, n_qk_blocks, n_q_blocks):
    i = pl.program_id(1)
    nblk = pl.num_programs(1)
    j = pl.program_id(2)
    pad = A_CONV_W // 2
    prev = prev_ref[0].astype(F32)
    nxt = next_ref[0].astype(F32)
    ext_ref[0:_CONV_HALO, :] = jnp.where(i > 0, prev, 0.0)
    ext_ref[_CONV_HALO:_CONV_HALO + ts, :] = cur_ref[0].astype(F32)
    ext_ref[_CONV_HALO + ts:, :] = jnp.where(i < nblk - 1, nxt, 0.0)
    w = w_ref[...]
    ext = ext_ref[...]
    nrow = ts + 2 * _CONV_HALO
    y = ext[_CONV_HALO:_CONV_HALO + ts, :] * w[pad:pad + 1, :]
    for t in range(A_CONV_W):
        if t != pad:
            rolled = pltpu.roll(ext, (pad - t) % nrow, 0)
            y = y + rolled[_CONV_HALO:_CONV_HALO + ts, :] * w[t:t + 1, :]
    y = y * jax.nn.sigmoid(y)
    is_qk = j < n_qk_blocks
    scale = jnp.where(j < n_q_blocks, HEAD_DIM ** -0.5, 1.0).astype(F32)
    for hh in range(cw // HEAD_DIM):
        yh = y[:, hh * HEAD_DIM:(hh + 1) * HEAD_DIM]
        ss = jnp.sum(yh * yh, axis=-1, keepdims=True)
        yn = yh * (lax.rsqrt(ss + L2_EPS) * scale)
        o_ref[0, :, hh * HEAD_DIM:(hh + 1) * HEAD_DIM] = jnp.where(is_qk, yn, yh).astype(o_ref.dtype)


def _conv_silu_l2(proj_a, conv_w, *, qk_width, q_width, n_ch, ts=512, cw=512):
    bsz, seq, _ = proj_a.shape
    ts = min(ts, seq)
    hb = ts // _CONV_HALO
    nhalo = seq // _CONV_HALO
    w8 = jnp.zeros((8, n_ch), F32).at[:A_CONV_W].set(conv_w.astype(F32))
    kern = functools.partial(_conv_kernel, ts=ts, cw=cw, n_qk_blocks=qk_width // cw, n_q_blocks=q_width // cw)
    return pl.pallas_call(
        kern,
        out_shape=jax.ShapeDtypeStruct((bsz, seq, n_ch), BF16),
        grid=(bsz, seq // ts, n_ch // cw),
        in_specs=[
            pl.BlockSpec((1, _CONV_HALO, cw), lambda b, i, j: (b, jnp.maximum(i * hb - 1, 0), j)),
            pl.BlockSpec((1, ts, cw), lambda b, i, j: (b, i, j)),
            pl.BlockSpec((1, _CONV_HALO, cw), lambda b, i, j: (b, jnp.minimum((i + 1) * hb, nhalo - 1), j)),
            pl.BlockSpec((8, cw), lambda b, i, j: (0, j)),
        ],
        out_specs=pl.BlockSpec((1, ts, cw), lambda b, i, j: (b, i, j)),
        scratch_shapes=[pltpu.VMEM((ts + 2 * _CONV_HALO, cw), F32)],
        compiler_params=_cparams(("parallel", "parallel", "parallel")),
        name="conv_silu_l2norm",
    )(proj_a, proj_a, proj_a, w8)


_INV_BASE = 8


def _split_bf16(x):
    hi = x.astype(BF16)
    lo = (x - hi.astype(F32)).astype(BF16)
    return hi, lo


def _dot3(a, b):
    a_hi, a_lo = a
    b_hi, b_lo = b
    return (jnp.dot(a_hi, b_hi, preferred_element_type=F32)
            + (jnp.dot(a_hi, b_lo, preferred_element_type=F32) + jnp.dot(a_lo, b_hi, preferred_element_type=F32)))


def _delta_kernel(q_ref, k_ref, v_ref, gt_ref, alog_ref, dtb_ref, o_ref, state_ref, *, hg, n_heads):
    c = A_CHUNK
    d = pl.program_id(0) % 2
    grp = pl.program_id(1)
    n = pl.program_id(2)

    @pl.when(n == 0)
    def _():
        state_ref[...] = jnp.zeros_like(state_ref)

    row = lax.broadcasted_iota(jnp.int32, (c, c), 0)
    col = lax.broadcasted_iota(jnp.int32, (c, c), 1)
    ahead = jnp.where(d == 0, row - col, col - row)
    incl = ahead >= 0
    strict = ahead > 0
    tri = jnp.where(incl, 1.0, 0.0).astype(F32)
    eye = jnp.where(row == col, 1.0, 0.0).astype(F32)

    graw = gt_ref[0]
    beta_all = jax.nn.sigmoid(graw)
    z = graw + dtb_ref[0]
    softplus = jnp.maximum(z, 0.0) + jnp.log(1.0 + jnp.exp(-jnp.abs(z)))
    g_all = -jnp.exp(alog_ref[0]) * softplus
    gc_all = jnp.dot(tri, g_all, precision=HI, preferred_element_type=F32)
    gct_all = lax.dot_general(g_all, tri, (((0,), (1,)), ((), ())), precision=HI,
                              preferred_element_type=F32)
    gtot_all = jnp.sum(g_all, axis=0, keepdims=True)
    lane = lax.broadcasted_iota(jnp.int32, (c, LANES), 1)
    subl = lax.broadcasted_iota(jnp.int32, (LANES, c), 0)
    contract_last = (((1,), (1,)), ((), ()))
    contract_first = (((0,), (0,)), ((), ()))

    heads = range(hg)
    sls = [slice(hh * HEAD_DIM, (hh + 1) * HEAD_DIM) for hh in heads]
    q16 = [q_ref[0, :, sl] for sl in sls]
    k16 = [k_ref[0, :, sl] for sl in sls]
    gcol, beta, gtot, decay = [], [], [], []
    for hh in heads:
        a_lane = n_heads + grp * hg + hh
        sel_a = lane == a_lane
        sel_b = lane == (grp * hg + hh)
        gcol.append(jnp.sum(jnp.where(sel_a, gc_all, 0.0), axis=1, keepdims=True))
        beta.append(jnp.sum(jnp.where(sel_b, beta_all, 0.0), axis=1, keepdims=True))
        gtot.append(jnp.sum(jnp.where(sel_a[0:1], gtot_all, 0.0), axis=1, keepdims=True))
        grow = jnp.sum(jnp.where(subl == a_lane, gct_all, 0.0), axis=0, keepdims=True)
        diff = gcol[hh] - grow
        decay.append(jnp.where(incl, jnp.exp(jnp.where(incl, diff, 0.0)), 0.0))
    kb = [k16[hh].astype(F32) * beta[hh] for hh in heads]
    kq = [lax.dot_general(jnp.concatenate([kb[hh].astype(BF16), q16[hh]], axis=0), k16[hh], contract_last,
                          preferred_element_type=F32) for hh in heads]
    attn = [(kq[hh][c:] * decay[hh]).astype(BF16) for hh in heads]
    a_mat = [jnp.where(strict, kq[hh][:c] * decay[hh], 0.0) for hh in heads]
    same_base = (row // _INV_BASE) == (col // _INV_BASE)
    x = [jnp.where(same_base, -a_mat[hh], 0.0) for hh in heads]
    t_mat = [eye + x[hh] for hh in heads]
    p = 2
    while p < _INV_BASE:
        xs = [_split_bf16(x[hh]) for hh in heads]
        x = [_dot3(xs[hh], xs[hh]) for hh in heads]
        t_mat = [t_mat[hh] + _dot3(_split_bf16(t_mat[hh]), _split_bf16(x[hh])) for hh in heads]
        p *= 2
    s = _INV_BASE
    while s < c:
        pair = ((row // (2 * s)) == (col // (2 * s))) & ((row // s) != (col // s))
        ts = [_split_bf16(t_mat[hh]) for hh in heads]
        tl = [_dot3(ts[hh], _split_bf16(jnp.where(pair, a_mat[hh], 0.0))) for hh in heads]
        t_mat = [t_mat[hh] - _dot3(_split_bf16(tl[hh]), ts[hh]) for hh in heads]
        s *= 2
    eg = [jnp.exp(gcol[hh]) for hh in heads]
    rhs = [jnp.concatenate([v_ref[0, :, sls[hh]].astype(F32) * beta[hh], kb[hh] * eg[hh]], axis=1).astype(BF16)
           for hh in heads]
    uw = [jnp.dot(t_mat[hh].astype(BF16), rhs[hh], preferred_element_type=F32) for hh in heads]
    state = [state_ref[hh] for hh in heads]
    ws = [jnp.dot(jnp.concatenate([uw[hh][:, HEAD_DIM:], q16[hh].astype(F32) * eg[hh]], axis=0).astype(BF16),
                  state[hh].astype(BF16), preferred_element_type=F32) for hh in heads]
    vn16 = [(uw[hh][:, :HEAD_DIM] - ws[hh][:c]).astype(BF16) for hh in heads]
    for hh in heads:
        o = ws[hh][c:] + jnp.dot(attn[hh], vn16[hh], preferred_element_type=F32)
        o_ref[0, 0, :, sls[hh]] = o.astype(o_ref.dtype)
    for hh in heads:
        kd = (k16[hh].astype(F32) * jnp.exp(gtot[hh] - gcol[hh])).astype(BF16)
        state_ref[hh] = state[hh] * jnp.exp(gtot[hh]) + lax.dot_general(kd, vn16[hh], contract_first,
                                                                       preferred_element_type=F32)


def _delta_scan(qkv, gates, alog_row, dtb_row, *, n_heads, hg=16):
    bsz, seq, _ = qkv.shape
    c = A_CHUNK
    nchunk = seq // c
    ngrp = n_heads // hg
    wblk = hg * HEAD_DIM

    def cidx(bd, n):
        return jnp.where(bd % 2 == 0, n, nchunk - 1 - n)

    kern = functools.partial(_delta_kernel, hg=hg, n_heads=n_heads)
    return pl.pallas_call(
        kern,
        out_shape=jax.ShapeDtypeStruct((2, bsz, seq, n_heads * HEAD_DIM), F32),
        grid=(bsz * 2, ngrp, nchunk),
        in_specs=[
            pl.BlockSpec((1, c, wblk), lambda bd, g, n: (bd // 2, cidx(bd, n), g)),
            pl.BlockSpec((1, c, wblk), lambda bd, g, n: (bd // 2, cidx(bd, n), ngrp + g)),
            pl.BlockSpec((1, c, wblk), lambda bd, g, n: (bd // 2, cidx(bd, n), 2 * ngrp + g)),
            pl.BlockSpec((1, c, LANES), lambda bd, g, n: (bd // 2, cidx(bd, n), bd % 2)),
            pl.BlockSpec((1, 1, LANES), lambda bd, g, n: (bd % 2, 0, 0)),
            pl.BlockSpec((1, 1, LANES), lambda bd, g, n: (bd % 2, 0, 0)),
        ],
        out_specs=pl.BlockSpec((1, 1, c, wblk), lambda bd, g, n: (bd % 2, bd // 2, cidx(bd, n), g)),
        scratch_shapes=[pltpu.VMEM((hg, HEAD_DIM, HEAD_DIM), F32)],
        compiler_params=_cparams(("parallel", "parallel", "arbitrary")),
        name="gated_delta_scan",
    )(qkv, qkv, qkv, gates, alog_row, dtb_row)


def _delta_out_kernel(of_ref, ob_ref, z_ref, w_ref, o_ref, *, cw):
    o = of_ref[0] + ob_ref[0]
    z = z_ref[...].astype(F32)
    w = w_ref[...]
    for hh in range(cw // HEAD_DIM):
        sl = slice(hh * HEAD_DIM, (hh + 1) * HEAD_DIM)
        oh = o[:, sl]
        ms = jnp.mean(oh * oh, axis=-1, keepdims=True)
        zh = z[:, sl]
        o_ref[:, sl] = (oh * lax.rsqrt(ms + NORM_EPS) * w * (zh * jax.nn.sigmoid(zh))).astype(o_ref.dtype)


def _delta_out(o2, proj_a, out_norm, *, z_col_block, tr=512, cw=512):
    _, t, vdim = o2.shape
    tr = min(tr, t)
    return pl.pallas_call(
        functools.partial(_delta_out_kernel, cw=cw),
        out_shape=jax.ShapeDtypeStruct((t, vdim), BF16),
        grid=(t // tr, vdim // cw),
        in_specs=[
            pl.BlockSpec((1, tr, cw), lambda i, j: (0, i, j)),
            pl.BlockSpec((1, tr, cw), lambda i, j: (1, i, j)),
            pl.BlockSpec((tr, cw), lambda i, j: (i, z_col_block + j)),
            pl.BlockSpec((1, HEAD_DIM), lambda i, j: (0, 0)),
        ],
        out_specs=pl.BlockSpec((tr, cw), lambda i, j: (i, j)),
        compiler_params=_cparams(("parallel", "parallel")),
        name="delta_out_norm_gate",
    )(o2, o2, proj_a, out_norm.reshape(1, HEAD_DIM).astype(F32))


def _rotate_pairs(x, half):
    lane = lax.broadcasted_iota(jnp.int32, x.shape, 1)
    first = (lane % (2 * half)) < half
    return jnp.where(first, pltpu.roll(x, LANES - half, 1), pltpu.roll(x, half, 1))


def _deinterleave_matrix(tr, dil, transpose=False):
    per = tr // dil
    i_out = lax.broadcasted_iota(jnp.int32, (tr, tr), 1 if transpose else 0)
    i_in = lax.broadcasted_iota(jnp.int32, (tr, tr), 0 if transpose else 1)
    src = (i_out % per) * dil + i_out // per
    return jnp.where(i_in == src, 1.0, 0.0).astype(BF16)


def _rope_deint_kernel(q_ref, k_ref, v_ref, cos_ref, sin_ref, o_ref, *, tr, dil, gw):
    cos = cos_ref[...]
    sin = sin_ref[...]
    per = tr // dil
    perm = _deinterleave_matrix(tr, dil) if dil > 1 else None

    def emit(y16, col0):
        if dil > 1:
            y16 = jnp.dot(perm, y16, preferred_element_type=F32).astype(BF16)
        for r in range(dil):
            o_ref[0, r, :, col0:col0 + HEAD_DIM] = y16[r * per:(r + 1) * per, :]

    for part, ref in enumerate((q_ref, k_ref)):
        for hh in range(gw // HEAD_DIM):
            sl = slice(hh * HEAD_DIM, (hh + 1) * HEAD_DIM)
            x = ref[:, sl].astype(F32)
            y = x * cos + _rotate_pairs(x, B_ROT_DIM // 2) * sin
            emit(y.astype(BF16), part * gw + hh * HEAD_DIM)
    for hh in range(gw // HEAD_DIM):
        emit(v_ref[:, hh * HEAD_DIM:(hh + 1) * HEAD_DIM], 2 * gw + hh * HEAD_DIM)


def _rope_deint(proj_b, cos_t, sin_t, gi, dil, *, bsz, seq, n_groups, tr=256):
    gw = B_HEADS_PER_GROUP * HEAD_DIM
    nsb = seq // tr
    per = tr // dil
    return pl.pallas_call(
        functools.partial(_rope_deint_kernel, tr=tr, dil=dil, gw=gw),
        out_shape=jax.ShapeDtypeStruct((bsz, dil, seq // dil, 3 * gw), BF16),
        grid=(bsz, nsb),
        in_specs=[pl.BlockSpec((tr, gw), lambda b, i: (b * nsb + i, gi)),
                  pl.BlockSpec((tr, gw), lambda b, i: (b * nsb + i, n_groups + gi)),
                  pl.BlockSpec((tr, gw), lambda b, i: (b * nsb + i, 2 * n_groups + gi)),
                  pl.BlockSpec((tr, HEAD_DIM), lambda b, i: (i, 0)),
                  pl.BlockSpec((tr, HEAD_DIM), lambda b, i: (i, 0))],
        out_specs=pl.BlockSpec((1, dil, per, 3 * gw), lambda b, i: (b, 0, i, 0)),
        compiler_params=_cparams(("parallel", "parallel")),
        name=f"rope_deinterleave_d{dil}",
    )(proj_b, proj_b, proj_b, cos_t, sin_t)


_BW = 64
_BQ = 128


def _dil_kernel(q_ref, kc_ref, kp_ref, kn_ref, vc_ref, vp_ref, vn_ref, o_ref, l_ref, kext, vext,
                *, qb, nh, seq_len):
    i = pl.program_id(2)
    base = i * qb
    kext[0:_BW, :] = kp_ref[0, 0]
    kext[_BW:_BW + qb, :] = kc_ref[0, 0]
    kext[_BW + qb:, :] = kn_ref[0, 0]
    vext[0:_BW, :] = vp_ref[0, 0]
    vext[_BW:_BW + qb, :] = vc_ref[0, 0]
    vext[_BW + qb:, :] = vn_ref[0, 0]
    scale = HEAD_DIM ** -0.5
    nkeys = _BQ + 2 * _BW
    qi = lax.broadcasted_iota(jnp.int32, (_BQ, nkeys), 0)
    kj = lax.broadcasted_iota(jnp.int32, (_BQ, nkeys), 1)
    lane = lax.broadcasted_iota(jnp.int32, (_BQ, LANES), 1)
    for sb in range(qb // _BQ):
        qpos = base + sb * _BQ + qi
        kpos = base + sb * _BQ - _BW + kj
        valid = (jnp.abs(kpos - qpos) <= _BW) & (kpos >= 0) & (kpos < seq_len)
        lse_blk = jnp.zeros((_BQ, LANES), F32)
        for hh in range(nh):
            sl = slice(hh * HEAD_DIM, (hh + 1) * HEAD_DIM)
            q = q_ref[0, 0, sb * _BQ:(sb + 1) * _BQ, sl]
            kx = kext[sb * _BQ:sb * _BQ + nkeys, sl]
            vx = vext[sb * _BQ:sb * _BQ + nkeys, sl]
            s = lax.dot_general(q, kx, (((1,), (1,)), ((), ())), preferred_element_type=F32) * scale
            s = jnp.where(valid, s, -1e30)
            m = jnp.max(s, axis=-1, keepdims=True)
            e = jnp.where(valid, jnp.exp(s - m), 0.0)
            l = jnp.sum(e, axis=-1, keepdims=True)
            p = (e / l).astype(BF16)
            o = jnp.dot(p, vx, preferred_element_type=F32)
            o_ref[0, 0, sb * _BQ:(sb + 1) * _BQ, sl] = o.astype(o_ref.dtype)
            lse_blk = jnp.where(lane == hh, m + jnp.log(l), lse_blk)
        l_ref[0, 0, sb * _BQ:(sb + 1) * _BQ, :] = lse_blk


def _dilated_group(grp_qkv):
    bsz, dil, ln, _ = grp_qkv.shape
    gw = B_HEADS_PER_GROUP * HEAD_DIM
    qb = min(512, ln)
    hb = qb // _BW
    nhalo = ln // _BW
    lo = lambda i: jnp.maximum(i * hb - 1, 0)
    hi = lambda i: jnp.minimum((i + 1) * hb, nhalo - 1)

    def cur(col):
        return pl.BlockSpec((1, 1, qb, gw), lambda b, r, i: (b, r, i, col))

    def halo(col, f):
        return pl.BlockSpec((1, 1, _BW, gw), lambda b, r, i: (b, r, f(i), col))

    return pl.pallas_call(
        functools.partial(_dil_kernel, qb=qb, nh=B_HEADS_PER_GROUP, seq_len=ln),
        out_shape=(jax.ShapeDtypeStruct((bsz, dil, ln, gw), BF16),
                   jax.ShapeDtypeStruct((bsz, dil, ln, LANES), F32)),
        grid=(bsz, dil, ln // qb),
        in_specs=[cur(0), cur(1), halo(1, lo), halo(1, hi), cur(2), halo(2, lo), halo(2, hi)],
        out_specs=(pl.BlockSpec((1, 1, qb, gw), lambda b, r, i: (b, r, i, 0)),
                   pl.BlockSpec((1, 1, qb, LANES), lambda b, r, i: (b, r, i, 0))),
        scratch_shapes=[pltpu.VMEM((qb + 2 * _BW, gw), BF16), pltpu.VMEM((qb + 2 * _BW, gw), BF16)],
        compiler_params=_cparams(("parallel", "parallel", "parallel")),
        name=f"dilated_attn_d{dil}",
    )(*([grp_qkv] * 7))


def _merge_kernel(*refs, tr, dils, nh):
    ng = len(dils)
    o_refs = refs[:ng]
    l_refs = refs[ng:2 * ng]
    out_ref = refs[2 * ng]
    outs, lses = [], []
    for g, dil in enumerate(dils):
        per = tr // dil
        if dil == 1:
            outs.append(o_refs[g][0, 0].astype(F32))
            lses.append(l_refs[g][0, 0])
            continue
        pt = _deinterleave_matrix(tr, dil, transpose=True)
        y = jnp.concatenate([o_refs[g][0, r] for r in range(dil)], axis=0)
        outs.append(jnp.dot(pt, y, preferred_element_type=F32))
        ls = jnp.concatenate([l_refs[g][0, r] for r in range(dil)], axis=0)
        h1 = ls.astype(BF16)
        r1 = ls - h1.astype(F32)
        h2 = r1.astype(BF16)
        h3 = (r1 - h2.astype(F32)).astype(BF16)
        lses.append(jnp.dot(pt, h1, preferred_element_type=F32)
                    + (jnp.dot(pt, h2, preferred_element_type=F32) + jnp.dot(pt, h3, preferred_element_type=F32)))
    mx = lses[0]
    for g in range(1, ng):
        mx = jnp.maximum(mx, lses[g])
    es = [jnp.exp(l - mx) for l in lses]
    den = es[0]
    for g in range(1, ng):
        den = den + es[g]
    ws = [e / den for e in es]
    lane = lax.broadcasted_iota(jnp.int32, (tr, LANES), 1)
    for hh in range(nh):
        sl = slice(hh * HEAD_DIM, (hh + 1) * HEAD_DIM)
        acc = None
        for g in range(ng):
            wcol = jnp.sum(jnp.where(lane == hh, ws[g], 0.0), axis=1, keepdims=True)
            term = outs[g][:, sl] * wcol
            acc = term if acc is None else acc + term
        out_ref[:, sl] = acc.astype(out_ref.dtype)


def _merge_groups(os_, ls_, *, bsz, seq, tr=256):
    gw = B_HEADS_PER_GROUP * HEAD_DIM
    dils = tuple(o.shape[1] for o in os_)
    nsb = seq // tr
    in_specs = ([pl.BlockSpec((1, d, tr // d, gw), lambda b, i: (b, 0, i, 0)) for d in dils]
                + [pl.BlockSpec((1, d, tr // d, LANES), lambda b, i: (b, 0, i, 0)) for d in dils])
    return pl.pallas_call(
        functools.partial(_merge_kernel, tr=tr, dils=dils, nh=B_HEADS_PER_GROUP),
        out_shape=jax.ShapeDtypeStruct((bsz * seq, gw), BF16),
        grid=(bsz, nsb),
        in_specs=in_specs,
        out_specs=pl.BlockSpec((tr, gw), lambda b, i: (b * nsb + i, 0)),
        compiler_params=_cparams(("parallel", "parallel")),
        name="dilated_merge",
    )(*os_, *ls_)


def _qknorm_rope_kernel(x_ref, wq_ref, wk_ref, cos_ref, sin_ref, o_ref, *, cw, n_q_blocks):
    j = pl.program_id(1)
    is_q = j < n_q_blocks
    w = jnp.where(is_q, wq_ref[...], wk_ref[...])
    scale = jnp.where(is_q, HEAD_DIM ** -0.5 * LOG2E, 1.0).astype(F32)
    cos = cos_ref[...]
    sin = sin_ref[...]
    for hh in range(cw // HEAD_DIM):
        sl = slice(hh * HEAD_DIM, (hh + 1) * HEAD_DIM)
        x = x_ref[:, sl].astype(F32)
        ms = jnp.mean(x * x, axis=-1, keepdims=True)
        y = x * lax.rsqrt(ms + NORM_EPS) * w
        y = y * cos + _rotate_pairs(y, HEAD_DIM // 4) * sin
        o_ref[:, sl] = (y * scale).astype(o_ref.dtype)


def _qknorm_rope(qkv, q_norm, k_norm, cos_t, sin_t, *, q_width, qk_width, seq, tr=512, cw=512):
    t = qkv.shape[0]
    tr = min(tr, seq)
    nsb = seq // tr
    return pl.pallas_call(
        functools.partial(_qknorm_rope_kernel, cw=cw, n_q_blocks=q_width // cw),
        out_shape=jax.ShapeDtypeStruct((t, qk_width), BF16),
        grid=(t // tr, qk_width // cw),
        in_specs=[pl.BlockSpec((tr, cw), lambda i, j: (i, j)),
                  pl.BlockSpec((1, HEAD_DIM), lambda i, j: (0, 0)),
                  pl.BlockSpec((1, HEAD_DIM), lambda i, j: (0, 0)),
                  pl.BlockSpec((tr, HEAD_DIM), lambda i, j: (i % nsb, 0)),
                  pl.BlockSpec((tr, HEAD_DIM), lambda i, j: (i % nsb, 0))],
        out_specs=pl.BlockSpec((tr, cw), lambda i, j: (i, j)),
        compiler_params=_cparams(("parallel", "parallel")),
        name="qk_norm_axial_rope",
    )(qkv, q_norm.reshape(1, HEAD_DIM).astype(F32), k_norm.reshape(1, HEAD_DIM).astype(F32), cos_t, sin_t)


def _flash_kernel(q_ref, k_ref, v_ref, o_ref, qs_ref, v1_ref, m_ref, acc_ref, *, tq, tk, rep, seq):
    @pl.when(pl.program_id(2) == 0)
    def _():
        v1_ref[:, :HEAD_DIM] = v_ref[0]
        v1_ref[:, HEAD_DIM:] = jnp.ones((seq, HEAD_DIM), BF16)

    for r in range(rep):
        qs_ref[r * tq:(r + 1) * tq, :] = q_ref[0, :, r * HEAD_DIM:(r + 1) * HEAD_DIM]
    m_ref[...] = jnp.full(m_ref.shape, -1e30, F32)
    acc_ref[...] = jnp.zeros(acc_ref.shape, F32)

    def body(t, carry):
        off = pl.multiple_of(t * tk, tk)
        kt = k_ref[0, pl.ds(off, tk), :]
        vt = v1_ref[pl.ds(off, tk), :]
        s = [lax.dot_general(qs_ref[r * tq:(r + 1) * tq, :], kt, (((1,), (1,)), ((), ())),
                             preferred_element_type=F32) for r in range(rep)]
        for r in range(rep):
            rows = slice(r * tq, (r + 1) * tq)
            m_old = m_ref[rows, :]
            m_new = jnp.maximum(m_old, jnp.max(s[r], axis=-1, keepdims=True))
            p = jnp.exp2(s[r] - m_new).astype(BF16)
            pv = jnp.dot(p, vt, preferred_element_type=F32)
            acc_ref[rows, :] = acc_ref[rows, :] * jnp.exp2(m_old - m_new) + pv
            m_ref[rows, :] = m_new
        return carry

    lax.fori_loop(0, seq // tk, body, 0)
    for r in range(rep):
        a = acc_ref[r * tq:(r + 1) * tq, :]
        o_ref[0, :, r * HEAD_DIM:(r + 1) * HEAD_DIM] = (a[:, :HEAD_DIM] / a[:, HEAD_DIM:]).astype(o_ref.dtype)


def _flash_attention(qk, qkv, *, bsz, seq, tq=512, tk=4096):
    qw = C_Q_HEADS * HEAD_DIM
    tq, tk = min(tq, seq), min(tk, seq)
    gq = C_REP * HEAD_DIM
    qk3 = qk.reshape(bsz, seq, qk.shape[-1])
    qkv3 = qkv.reshape(bsz, seq, qkv.shape[-1])
    k_blk0 = qw // HEAD_DIM
    v_blk0 = (qw + C_KV_HEADS * HEAD_DIM) // HEAD_DIM
    out = pl.pallas_call(
        functools.partial(_flash_kernel, tq=tq, tk=tk, rep=C_REP, seq=seq),
        out_shape=jax.ShapeDtypeStruct((bsz, seq, qw), BF16),
        grid=(bsz, C_KV_HEADS, seq // tq),
        in_specs=[pl.BlockSpec((1, tq, gq), lambda b, g, i: (b, i, g)),
                  pl.BlockSpec((1, seq, HEAD_DIM), lambda b, g, i: (b, 0, k_blk0 + g)),
                  pl.BlockSpec((1, seq, HEAD_DIM), lambda b, g, i: (b, 0, v_blk0 + g))],
        out_specs=pl.BlockSpec((1, tq, gq), lambda b, g, i: (b, i, g)),
        scratch_shapes=[pltpu.VMEM((C_REP * tq, HEAD_DIM), BF16),
                        pltpu.VMEM((seq, 2 * HEAD_DIM), BF16),
                        pltpu.VMEM((C_REP * tq, 1), F32),
                        pltpu.VMEM((C_REP * tq, 2 * HEAD_DIM), F32)],
        compiler_params=_cparams(("parallel", "parallel", "arbitrary")),
        name="gqa_flash_attention",
    )(qk3, qk3, qkv3)
    return out.reshape(bsz * seq, qw)


def _rope_cos_sin(pos, dim, theta):
    inv = 1.0 / (theta ** (jnp.arange(0, dim, 2, dtype=F32) / dim))
    ang = pos.astype(F32)[:, None] * inv[None, :]
    return jnp.cos(ang), jnp.sin(ang)


def _tables(seq):
    tok = jnp.arange(seq)
    cos_b, sin_b = _rope_cos_sin(tok, B_ROT_DIM, B_ROPE_THETA)
    ones = jnp.ones((seq, HEAD_DIM - B_ROT_DIM), F32)
    cb = jnp.concatenate([cos_b, cos_b, ones], axis=-1)
    sb = jnp.concatenate([-sin_b, sin_b, jnp.zeros_like(ones)], axis=-1)
    cos_r, sin_r = _rope_cos_sin(tok // GRID_W, HEAD_DIM // 2, C_ROPE_THETA)
    cos_c, sin_c = _rope_cos_sin(tok % GRID_W, HEAD_DIM // 2, C_ROPE_THETA)
    cc = jnp.concatenate([cos_r, cos_r, cos_c, cos_c], axis=-1)
    sc = jnp.concatenate([-sin_r, sin_r, -sin_c, sin_c], axis=-1)
    return cb, sb, cc, sc


def _ffn(x2d, norm_w, w_gate, w_up, w_down, layer):
    h = _rmsnorm(x2d, norm_w, BF16)
    act = _gateup(h, w_gate, w_up, layer, tm=2048, tn=256)
    dff = act.shape[1]
    nsplit = 2 if (dff // 2) % LANES == 0 else 1
    for kb in range(nsplit):
        x2d = _wmatmul([(act, kb, dff // nsplit)], w_down, w_lead=layer, w_row_blk=kb, n=x2d.shape[1],
                       tm=1024, tn=512, out_dtype=F32, residual=x2d)
    return x2d


def _mixer_ab(x2d, norm_w, w_in, conv_w, a_log, dt_bias, out_norm, w_out, j, cb, sb, *, bsz, seq):
    t, dm = x2d.shape
    n_heads = a_log.shape[-1]
    a_qk = n_heads * HEAD_DIM
    a_v = n_heads * HEAD_DIM
    a_qkv = 2 * a_qk + a_v
    ngate = 2 * n_heads
    b_w = (w_in.shape[-1] - a_qkv - a_v - 2 * ngate) // 3
    h = _rmsnorm(x2d, norm_w, BF16)

    w_in_t = jnp.swapaxes(w_in, 1, 2)
    g0 = a_qkv + a_v
    w_beta = w_in_t[j, g0:g0 + ngate].reshape(2, n_heads, dm)
    w_alpha = w_in_t[j, g0 + ngate:g0 + 2 * ngate].reshape(2, n_heads, dm)
    w_g = jnp.concatenate([w_beta, w_alpha, jnp.zeros((2, LANES - 2 * n_heads, dm), F32)], axis=1)
    w_g = w_g.reshape(1, 2 * LANES, dm)
    w_b = _shift_cast_rows(w_in_t, j, g0 + 2 * ngate, 3 * b_w)

    proj_a = _wmatmul_t(h, w_in_t, w_lead=j, n=a_qkv + a_v, tm=2048, tn=512, out_dtype=BF16)
    gates = _wmatmul_t(h, w_g, w_lead=0, n=2 * LANES, tm=2048, tn=2 * LANES, out_dtype=F32)
    proj_b = _wmatmul_t(h, w_b, w_lead=0, n=3 * b_w, tm=2048, tn=512, out_dtype=BF16)

    qkv = _conv_silu_l2(proj_a.reshape(bsz, seq, -1), conv_w, qk_width=2 * a_qk, q_width=a_qk, n_ch=a_qkv)
    pad = jnp.zeros((2, LANES - 2 * n_heads), F32)
    alog_row = jnp.concatenate([jnp.zeros((2, n_heads), F32), a_log.astype(F32), pad], axis=-1).reshape(2, 1, LANES)
    dtb_row = jnp.concatenate([jnp.zeros((2, n_heads), F32), dt_bias.astype(F32), pad], axis=-1).reshape(2, 1, LANES)
    o2 = _delta_scan(qkv, gates.reshape(bsz, seq, 2 * LANES), alog_row, dtb_row, n_heads=n_heads)
    o_a = _delta_out(o2.reshape(2, t, a_v), proj_a, out_norm, z_col_block=a_qkv // 512)

    n_groups = len(B_PATTERNS)
    os_, ls_ = [], []
    for gi, (_, dil) in enumerate(B_PATTERNS):
        grp = _rope_deint(proj_b, cb, sb, gi, dil, bsz=bsz, seq=seq, n_groups=n_groups)
        o_g, l_g = _dilated_group(grp)
        os_.append(o_g)
        ls_.append(l_g)
    o_b = _merge_groups(os_, ls_, bsz=bsz, seq=seq)

    return _wmatmul([(o_a, 0, a_v), (o_b, 0, o_b.shape[1])], w_out, w_lead=j, n=dm, tm=1024, tn=512,
                    out_dtype=F32, residual=x2d)


def _mixer_c(x2d, norm_w, w_qkv, q_norm, k_norm, w_out, j, cc, sc, *, bsz, seq):
    dm = x2d.shape[1]
    h = _rmsnorm(x2d, norm_w, BF16)
    qkv = _wmatmul([(h, 0, dm)], w_qkv, w_lead=j, n=w_qkv.shape[-1], tm=2048, tn=512, out_dtype=BF16)
    qw = C_Q_HEADS * HEAD_DIM
    kw = C_KV_HEADS * HEAD_DIM
    qk = _qknorm_rope(qkv, q_norm, k_norm, cc, sc, q_width=qw, qk_width=qw + kw, seq=seq)
    o = _flash_attention(qk, qkv, bsz=bsz, seq=seq)
    return _wmatmul([(o, 0, qw)], w_out, w_lead=j, n=dm, tm=1024, tn=512, out_dtype=F32, residual=x2d)


def kernel(x, norm_mix, norm_ffn, norm_final, ab_w_in, ab_conv_w, ab_a_log, ab_dt_bias, ab_out_norm, ab_w_out,
           c_w_qkv, c_q_norm, c_k_norm, c_w_out, ffn_w_gate, ffn_w_up, ffn_w_down):
    bsz, seq, dm = x.shape
    depth = norm_mix.shape[0]
    cb, sb, cc, sc = _tables(seq)
    x2d = x.reshape(bsz * seq, dm)
    for layer in range(depth):
        j = layer // 2
        if layer % 2 == 0:
            x2d = _mixer_ab(x2d, norm_mix[layer], ab_w_in, ab_conv_w[j], ab_a_log[j], ab_dt_bias[j],
                            ab_out_norm[j], ab_w_out, j, cb, sb, bsz=bsz, seq=seq)
        else:
            x2d = _mixer_c(x2d, norm_mix[layer], c_w_qkv, c_q_norm[j], c_k_norm[j], c_w_out, j,
                           cc, sc, bsz=bsz, seq=seq)
        x2d = _ffn(x2d, norm_ffn[layer], ffn_w_gate, ffn_w_up, ffn_w_down, layer)
    out = _rmsnorm(x2d, norm_final, x.dtype)
    return out.reshape(bsz, seq, dm)
```

```python
import functools

import jax
import jax.numpy as jnp
from jax import lax
from jax.experimental import pallas as pl
from jax.experimental.pallas import tpu as pltpu

HEAD_DIM = 128
NORM_EPS = 1e-6
L2_EPS = 1e-6
GRID_W = 64
A_CONV_W = 5
A_CHUNK = 64
B_PATTERNS = ((128, 1), (512, 4), (2048, 16))
B_HEADS_PER_GROUP = 8
B_ROT_DIM = HEAD_DIM // 4
B_ROPE_THETA = 500000.0
C_Q_HEADS = 32
C_KV_HEADS = 8
C_REP = C_Q_HEADS // C_KV_HEADS
C_ROPE_THETA = 10000.0
LOG2E = 1.4426950408889634

LANES = 128
VMEM_LIMIT = 56 * 1024 * 1024

BF16 = jnp.bfloat16
F32 = jnp.float32
HI = lax.Precision.HIGHEST


def _cparams(sem, vmem=VMEM_LIMIT):
    return pltpu.CompilerParams(dimension_semantics=sem, vmem_limit_bytes=vmem)


def _rmsnorm_kernel(x_ref, w_ref, o_ref):
    x = x_ref[...].astype(F32)
    ms = jnp.mean(x * x, axis=-1, keepdims=True)
    o_ref[...] = (x * lax.rsqrt(ms + NORM_EPS) * w_ref[...]).astype(o_ref.dtype)


def _rmsnorm(x2d, w, out_dtype, tr=256):
    m, d = x2d.shape
    return pl.pallas_call(
        _rmsnorm_kernel,
        out_shape=jax.ShapeDtypeStruct((m, d), out_dtype),
        grid=(m // tr,),
        in_specs=[pl.BlockSpec((tr, d), lambda i: (i, 0)),
                  pl.BlockSpec((1, d), lambda i: (0, 0))],
        out_specs=pl.BlockSpec((tr, d), lambda i: (i, 0)),
        compiler_params=_cparams(("parallel",)),
        name="rmsnorm",
    )(x2d, w.reshape(1, d).astype(F32))


def _stage_weight_chunk(chunk_ref, w16, ck):
    jp = pl.program_id(0)
    i = pl.program_id(1)
    w16[jp % 2, pl.ds(pl.multiple_of(i * ck, 16), ck), :] = chunk_ref[...].astype(BF16)


def _wmm_kernel(*refs, ksizes, has_res, ck):
    na = len(ksizes)
    a_refs = refs[:na]
    chunk_ref = refs[na]
    r_ref = refs[na + 1] if has_res else None
    o_ref = refs[na + 1 + int(has_res)]
    w16 = refs[na + 2 + int(has_res)]
    jp = pl.program_id(0)
    _stage_weight_chunk(chunk_ref, w16, ck)

    @pl.when(jp > 0)
    def _():
        slot = (jp + 1) % 2
        off = 0
        acc = None
        for a_ref, ks in zip(a_refs, ksizes):
            part = jnp.dot(a_ref[...], w16[slot, off:off + ks, :], preferred_element_type=F32)
            acc = part if acc is None else acc + part
            off += ks
        if has_res:
            acc = acc + r_ref[...]
        o_ref[...] = acc.astype(o_ref.dtype)


def _wmatmul(a_ops, w, *, w_lead, w_row_blk=0, w_col_blk0=0, n, tm, tn, out_dtype, residual=None):
    m = a_ops[0][0].shape[0]
    ksizes = tuple(k for _, _, k in a_ops)
    ktot = sum(ksizes)
    tm, tn = min(tm, m), min(tn, n)
    ni, nj = m // tm, n // tn
    ck = ktot // ni
    assert m % tm == 0 and n % tn == 0 and ktot % ni == 0 and ck % 16 == 0
    row_of = lambda jp, i: jnp.where(jp == 0, 0, i)
    col_of = lambda jp: jnp.maximum(jp - 1, 0)
    in_specs = [pl.BlockSpec((tm, ks), functools.partial(lambda jp, i, blk: (row_of(jp, i), blk), blk=blk))
                for _, blk, ks in a_ops]
    in_specs.append(pl.BlockSpec((None, ck, tn),
                                 lambda jp, i: (w_lead, w_row_blk * ni + i, w_col_blk0 + jnp.minimum(jp, nj - 1))))
    args = [a for a, _, _ in a_ops] + [w]
    if residual is not None:
        in_specs.append(pl.BlockSpec((tm, tn), lambda jp, i: (row_of(jp, i), col_of(jp))))
        args.append(residual)
    return pl.pallas_call(
        functools.partial(_wmm_kernel, ksizes=ksizes, has_res=residual is not None, ck=ck),
        out_shape=jax.ShapeDtypeStruct((m, n), out_dtype),
        grid=(nj + 1, ni),
        in_specs=in_specs,
        out_specs=pl.BlockSpec((tm, tn), lambda jp, i: (row_of(jp, i), col_of(jp))),
        scratch_shapes=[pltpu.VMEM((2, ktot, tn), BF16)],
        compiler_params=_cparams(("arbitrary", "arbitrary")),
        name="proj_matmul",
    )(*args)


def _wmm_t_kernel(a_ref, chunk_ref, o_ref, w16, *, ck):
    jp = pl.program_id(0)
    i = pl.program_id(1)
    w16[jp % 2, :, pl.ds(pl.multiple_of(i * ck, LANES), ck)] = chunk_ref[...].astype(BF16)

    @pl.when(jp > 0)
    def _():
        acc = lax.dot_general(a_ref[...], w16[(jp + 1) % 2], (((1,), (1,)), ((), ())),
                              preferred_element_type=F32)
        o_ref[...] = acc.astype(o_ref.dtype)


def _wmatmul_t(a, wt, *, w_lead, w_row_blk0=0, n, tm, tn, out_dtype):
    m, kdim = a.shape
    tm, tn = min(tm, m), min(tn, n)
    ni, nj = m // tm, n // tn
    ck = kdim // ni
    assert m % tm == 0 and n % tn == 0 and kdim % ni == 0 and ck % LANES == 0
    row_of = lambda jp, i: jnp.where(jp == 0, 0, i)
    col_of = lambda jp: jnp.maximum(jp - 1, 0)
    return pl.pallas_call(
        functools.partial(_wmm_t_kernel, ck=ck),
        out_shape=jax.ShapeDtypeStruct((m, n), out_dtype),
        grid=(nj + 1, ni),
        in_specs=[pl.BlockSpec((tm, kdim), lambda jp, i: (row_of(jp, i), 0)),
                  pl.BlockSpec((None, tn, ck), lambda jp, i: (w_lead, w_row_blk0 + jnp.minimum(jp, nj - 1), i))],
        out_specs=pl.BlockSpec((tm, tn), lambda jp, i: (row_of(jp, i), col_of(jp))),
        scratch_shapes=[pltpu.VMEM((2, tn, kdim), BF16)],
        compiler_params=_cparams(("arbitrary", "arbitrary")),
        name="proj_matmul_wt",
    )(a, wt)


def _gateup_kernel(h_ref, cg_ref, cu_ref, o_ref, wg16, wu16, *, ck):
    jp = pl.program_id(0)
    _stage_weight_chunk(cg_ref, wg16, ck)
    _stage_weight_chunk(cu_ref, wu16, ck)

    @pl.when(jp > 0)
    def _():
        slot = (jp + 1) % 2
        h = h_ref[...]
        g = jnp.dot(h, wg16[slot], preferred_element_type=F32)
        u = jnp.dot(h, wu16[slot], preferred_element_type=F32)
        o_ref[...] = (g * jax.nn.sigmoid(g) * u).astype(o_ref.dtype)


def _gateup(h, w_gate, w_up, layer, *, tm, tn):
    m, kdim = h.shape
    n = w_gate.shape[-1]
    tm, tn = min(tm, m), min(tn, n)
    ni, nj = m // tm, n // tn
    ck = kdim // ni
    assert m % tm == 0 and n % tn == 0 and kdim % ni == 0 and ck % 16 == 0
    wspec = pl.BlockSpec((None, ck, tn), lambda jp, i: (layer, i, jnp.minimum(jp, nj - 1)))
    return pl.pallas_call(
        functools.partial(_gateup_kernel, ck=ck),
        out_shape=jax.ShapeDtypeStruct((m, n), BF16),
        grid=(nj + 1, ni),
        in_specs=[pl.BlockSpec((tm, kdim), lambda jp, i: (jnp.where(jp == 0, 0, i), 0)), wspec, wspec],
        out_specs=pl.BlockSpec((tm, tn), lambda jp, i: (jnp.where(jp == 0, 0, i), jnp.maximum(jp - 1, 0))),
        scratch_shapes=[pltpu.VMEM((2, kdim, tn), BF16), pltpu.VMEM((2, kdim, tn), BF16)],
        compiler_params=_cparams(("arbitrary", "arbitrary")),
        name="ffn_gate_up",
    )(h, w_gate, w_up)


def _shift_cast_kernel(main_ref, next_ref, o_ref, *, shift):
    tr = main_ref.shape[0]
    o_ref[0:tr - shift, :] = main_ref[shift:, :].astype(o_ref.dtype)
    o_ref[tr - shift:, :] = next_ref[...].astype(o_ref.dtype)


def _shift_cast_rows(wt, lead, row0, nrows, *, tr=512, tc=1024):
    _, _, kdim = wt.shape
    base = (row0 // tr) * tr
    shift = row0 - base
    tc = min(tc, kdim)
    assert shift > 0 and shift % 16 == 0 and tr % shift == 0 and nrows % tr == 0 and kdim % tc == 0
    per = tr // shift
    return pl.pallas_call(
        functools.partial(_shift_cast_kernel, shift=shift),
        out_shape=jax.ShapeDtypeStruct((1, nrows, kdim), BF16),
        grid=(nrows // tr, kdim // tc),
        in_specs=[pl.BlockSpec((None, tr, tc), lambda i, j: (lead, base // tr + i, j)),
                  pl.BlockSpec((None, shift, tc), lambda i, j: (lead, (base // tr + i + 1) * per, j))],
        out_specs=pl.BlockSpec((None, tr, tc), lambda i, j: (0, i, j)),
        compiler_params=_cparams(("parallel", "parallel")),
        name="weight_shift_cast",
    )(wt, wt)


_CONV_HALO = 16


def _conv_kernel(prev_ref, cur_ref, next_ref, w_ref, o_ref, ext_ref, *, ts, cw, n_qk_blocks, n_q_blocks):
    i = pl.program_id(1)
    nblk = pl.num_programs(1)
    j = pl.program_id(2)
    pad = A_CONV_W // 2
    prev = prev_ref[0].astype(F32)
    nxt = next_ref[0].astype(F32)
    ext_ref[0:_CONV_HALO, :] = jnp.where(i > 0, prev, 0.0)
    ext_ref[_CONV_HALO:_CONV_HALO + ts, :] = cur_ref[0].astype(F32)
    ext_ref[_CONV_HALO + ts:, :] = jnp.where(i < nblk - 1, nxt, 0.0)
    w = w_ref[...]
    ext = ext_ref[...]
    nrow = ts + 2 * _CONV_HALO
    y = ext[_CONV_HALO:_CONV_HALO + ts, :] * w[pad:pad + 1, :]
    for t in range(A_CONV_W):
        if t != pad:
            rolled = pltpu.roll(ext, (pad - t) % nrow, 0)
            y = y + rolled[_CONV_HALO:_CONV_HALO + ts, :] * w[t:t + 1, :]
    y = y * jax.nn.sigmoid(y)
    is_qk = j < n_qk_blocks
    scale = jnp.where(j < n_q_blocks, HEAD_DIM ** -0.5, 1.0).astype(F32)
    for hh in range(cw // HEAD_DIM):
        yh = y[:, hh * HEAD_DIM:(hh + 1) * HEAD_DIM]
        ss = jnp.sum(yh * yh, axis=-1, keepdims=True)
        yn = yh * (lax.rsqrt(ss + L2_EPS) * scale)
        o_ref[0, :, hh * HEAD_DIM:(hh + 1) * HEAD_DIM] = jnp.where(is_qk, yn, yh).astype(o_ref.dtype)


def _conv_silu_l2(proj_a, conv_w, *, qk_width, q_width, n_ch, ts=512, cw=512):
    bsz, seq, _ = proj_a.shape
    ts = min(ts, seq)
    hb = ts // _CONV_HALO
    nhalo = seq // _CONV_HALO
    w8 = jnp.zeros((8, n_ch), F32).at[:A_CONV_W].set(conv_w.astype(F32))
    kern = functools.partial(_conv_kernel, ts=ts, cw=cw, n_qk_blocks=qk_width // cw, n_q_blocks=q_width // cw)
    return pl.pallas_call(
        kern,
        out_shape=jax.ShapeDtypeStruct((bsz, seq, n_ch), BF16),
        grid=(bsz, seq // ts, n_ch // cw),
        in_specs=[
            pl.BlockSpec((1, _CONV_HALO, cw), lambda b, i, j: (b, jnp.maximum(i * hb - 1, 0), j)),
            pl.BlockSpec((1, ts, cw), lambda b, i, j: (b, i, j)),
            pl.BlockSpec((1, _CONV_HALO, cw), lambda b, i, j: (b, jnp.minimum((i + 1) * hb, nhalo - 1), j)),
            pl.BlockSpec((8, cw), lambda b, i, j: (0, j)),
        ],
        out_specs=pl.BlockSpec((1, ts, cw), lambda b, i, j: (b, i, j)),
        scratch_shapes=[pltpu.VMEM((ts + 2 * _CONV_HALO, cw), F32)],
        compiler_params=_cparams(("parallel", "parallel", "parallel")),
        name="conv_silu_l2norm",
    )(proj_a, proj_a, proj_a, w8)


_INV_BASE = 8


def _split_bf16(x):
    hi = x.astype(BF16)
    lo = (x - hi.astype(F32)).astype(BF16)
    return hi, lo


def _dot3(a, b):
    a_hi, a_lo = a
    b_hi, b_lo = b
    return (jnp.dot(a_hi, b_hi, preferred_element_type=F32)
            + (jnp.dot(a_hi, b_lo, preferred_element_type=F32) + jnp.dot(a_lo, b_hi, preferred_element_type=F32)))


def _delta_kernel(q_ref, k_ref, v_ref, gt_ref, alog_ref, dtb_ref, o_ref, state_ref, *, hg, n_heads):
    c = A_CHUNK
    d = pl.program_id(0) % 2
    grp = pl.program_id(1)
    n = pl.program_id(2)

    @pl.when(n == 0)
    def _():
        state_ref[...] = jnp.zeros_like(state_ref)

    row = lax.broadcasted_iota(jnp.int32, (c, c), 0)
    col = lax.broadcasted_iota(jnp.int32, (c, c), 1)
    ahead = jnp.where(d == 0, row - col, col - row)
    incl = ahead >= 0
    strict = ahead > 0
    tri = jnp.where(incl, 1.0, 0.0).astype(F32)
    eye = jnp.where(row == col, 1.0, 0.0).astype(F32)

    graw = gt_ref[0]
    beta_all = jax.nn.sigmoid(graw)
    z = graw + dtb_ref[0]
    softplus = jnp.maximum(z, 0.0) + jnp.log(1.0 + jnp.exp(-jnp.abs(z)))
    g_all = -jnp.exp(alog_ref[0]) * softplus
    gc_all = jnp.dot(tri, g_all, precision=HI, preferred_element_type=F32)
    gct_all = lax.dot_general(g_all, tri, (((0,), (1,)), ((), ())), precision=HI,
                              preferred_element_type=F32)
    gtot_all = jnp.sum(g_all, axis=0, keepdims=True)
    lane = lax.broadcasted_iota(jnp.int32, (c, LANES), 1)
    subl = lax.broadcasted_iota(jnp.int32, (LANES, c), 0)
    contract_last = (((1,), (1,)), ((), ()))
    contract_first = (((0,), (0,)), ((), ()))

    heads = range(hg)
    sls = [slice(hh * HEAD_DIM, (hh + 1) * HEAD_DIM) for hh in heads]
    q16 = [q_ref[0, :, sl] for sl in sls]
    k16 = [k_ref[0, :, sl] for sl in sls]
    gcol, beta, gtot, decay = [], [], [], []
    for hh in heads:
        a_lane = n_heads + grp * hg + hh
        sel_a = lane == a_lane
        sel_b = lane == (grp * hg + hh)
        gcol.append(jnp.sum(jnp.where(sel_a, gc_all, 0.0), axis=1, keepdims=True))
        beta.append(jnp.sum(jnp.where(sel_b, beta_all, 0.0), axis=1, keepdims=True))
        gtot.append(jnp.sum(jnp.where(sel_a[0:1], gtot_all, 0.0), axis=1, keepdims=True))
        grow = jnp.sum(jnp.where(subl == a_lane, gct_all, 0.0), axis=0, keepdims=True)
        diff = gcol[hh] - grow
        decay.append(jnp.where(incl, jnp.exp(jnp.where(incl, diff, 0.0)), 0.0))
    kb = [k16[hh].astype(F32) * beta[hh] for hh in heads]
    kq = [lax.dot_general(jnp.concatenate([kb[hh].astype(BF16), q16[hh]], axis=0), k16[hh], contract_last,
                          preferred_element_type=F32) for hh in heads]
    attn = [(kq[hh][c:] * decay[hh]).astype(BF16) for hh in heads]
    a_mat = [jnp.where(strict, kq[hh][:c] * decay[hh], 0.0) for hh in heads]
    same_base = (row // _INV_BASE) == (col // _INV_BASE)
    x = [jnp.where(same_base, -a_mat[hh], 0.0) for hh in heads]
    t_mat = [eye + x[hh] for hh in heads]
    p = 2
    while p < _INV_BASE:
        xs = [_split_bf16(x[hh]) for hh in heads]
        x = [_dot3(xs[hh], xs[hh]) for hh in heads]
        t_mat = [t_mat[hh] + _dot3(_split_bf16(t_mat[hh]), _split_bf16(x[hh])) for hh in heads]
        p *= 2
    s = _INV_BASE
    while s < c:
        pair = ((row // (2 * s)) == (col // (2 * s))) & ((row // s) != (col // s))
        ts = [_split_bf16(t_mat[hh]) for hh in heads]
        tl = [_dot3(ts[hh], _split_bf16(jnp.where(pair, a_mat[hh], 0.0))) for hh in heads]
        t_mat = [t_mat[hh] - _dot3(_split_bf16(tl[hh]), ts[hh]) for hh in heads]
        s *= 2
    eg = [jnp.exp(gcol[hh]) for hh in heads]
    rhs = [jnp.concatenate([v_ref[0, :, sls[hh]].astype(F32) * beta[hh], kb[hh] * eg[hh]], axis=1).astype(BF16)
           for hh in heads]
    uw = [jnp.dot(t_mat[hh].astype(BF16), rhs[hh], preferred_element_type=F32) for hh in heads]
    state = [state_ref[hh] for hh in heads]
    ws = [jnp.dot(jnp.concatenate([uw[hh][:, HEAD_DIM:], q16[hh].astype(F32) * eg[hh]], axis=0).astype(BF16),
                  state[hh].astype(BF16), preferred_element_type=F32) for hh in heads]
    vn16 = [(uw[hh][:, :HEAD_DIM] - ws[hh][:c]).astype(BF16) for hh in heads]
    for hh in heads:
        o = ws[hh][c:] + jnp.dot(attn[hh], vn16[hh], preferred_element_type=F32)
        o_ref[0, 0, :, sls[hh]] = o.astype(o_ref.dtype)
    for hh in heads:
        kd = (k16[hh].astype(F32) * jnp.exp(gtot[hh] - gcol[hh])).astype(BF16)
        state_ref[hh] = state[hh] * jnp.exp(gtot[hh]) + lax.dot_general(kd, vn16[hh], contract_first,
                                                                       preferred_element_type=F32)


def _delta_scan(qkv, gates, alog_row, dtb_row, *, n_heads, hg=16):
    bsz, seq, _ = qkv.shape
    c = A_CHUNK
    nchunk = seq // c
    ngrp = n_heads // hg
    wblk = hg * HEAD_DIM

    def cidx(bd, n):
        return jnp.where(bd % 2 == 0, n, nchunk - 1 - n)

    kern = functools.partial(_delta_kernel, hg=hg, n_heads=n_heads)
    return pl.pallas_call(
        kern,
        out_shape=jax.ShapeDtypeStruct((2, bsz, seq, n_heads * HEAD_DIM), F32),
        grid=(bsz * 2, ngrp, nchunk),
        in_specs=[
            pl.BlockSpec((1, c, wblk), lambda bd, g, n: (bd // 2, cidx(bd, n), g)),
            pl.BlockSpec((1, c, wblk), lambda bd, g, n: (bd // 2, cidx(bd, n), ngrp + g)),
            pl.BlockSpec((1, c, wblk), lambda bd, g, n: (bd // 2, cidx(bd, n), 2 * ngrp + g)),
            pl.BlockSpec((1, c, LANES), lambda bd, g, n: (bd // 2, cidx(bd, n), bd % 2)),
            pl.BlockSpec((1, 1, LANES), lambda bd, g, n: (bd % 2, 0, 0)),
            pl.BlockSpec((1, 1, LANES), lambda bd, g, n: (bd % 2, 0, 0)),
        ],
        out_specs=pl.BlockSpec((1, 1, c, wblk), lambda bd, g, n: (bd % 2, bd // 2, cidx(bd, n), g)),
        scratch_shapes=[pltpu.VMEM((hg, HEAD_DIM, HEAD_DIM), F32)],
        compiler_params=_cparams(("parallel", "parallel", "arbitrary")),
        name="gated_delta_scan",
    )(qkv, qkv, qkv, gates, alog_row, dtb_row)


def _delta_out_kernel(of_ref, ob_ref, z_ref, w_ref, o_ref, *, cw):
    o = of_ref[0] + ob_ref[0]
    z = z_ref[...].astype(F32)
    w = w_ref[...]
    for hh in range(cw // HEAD_DIM):
        sl = slice(hh * HEAD_DIM, (hh + 1) * HEAD_DIM)
        oh = o[:, sl]
        ms = jnp.mean(oh * oh, axis=-1, keepdims=True)
        zh = z[:, sl]
        o_ref[:, sl] = (oh * lax.rsqrt(ms + NORM_EPS) * w * (zh * jax.nn.sigmoid(zh))).astype(o_ref.dtype)


def _delta_out(o2, proj_a, out_norm, *, z_col_block, tr=512, cw=512):
    _, t, vdim = o2.shape
    tr = min(tr, t)
    return pl.pallas_call(
        functools.partial(_delta_out_kernel, cw=cw),
        out_shape=jax.ShapeDtypeStruct((t, vdim), BF16),
        grid=(t // tr, vdim // cw),
        in_specs=[
            pl.BlockSpec((1, tr, cw), lambda i, j: (0, i, j)),
            pl.BlockSpec((1, tr, cw), lambda i, j: (1, i, j)),
            pl.BlockSpec((tr, cw), lambda i, j: (i, z_col_block + j)),
            pl.BlockSpec((1, HEAD_DIM), lambda i, j: (0, 0)),
        ],
        out_specs=pl.BlockSpec((tr, cw), lambda i, j: (i, j)),
        compiler_params=_cparams(("parallel", "parallel")),
        name="delta_out_norm_gate",
    )(o2, o2, proj_a, out_norm.reshape(1, HEAD_DIM).astype(F32))


def _pair_swap_matrix(half):
    src = lax.broadcasted_iota(jnp.int32, (LANES, LANES), 0)
    dst = lax.broadcasted_iota(jnp.int32, (LANES, LANES), 1)
    partner = jnp.where((dst % (2 * half)) < half, dst + half, dst - half)
    return jnp.where(src == partner, 1.0, 0.0).astype(BF16)


def _rotate_pairs(x, swap):
    hi, lo = _split_bf16(x)
    return jnp.dot(hi, swap, preferred_element_type=F32) + jnp.dot(lo, swap, preferred_element_type=F32)


def _deinterleave_matrix(tr, dil, transpose=False):
    per = tr // dil
    i_out = lax.broadcasted_iota(jnp.int32, (tr, tr), 1 if transpose else 0)
    i_in = lax.broadcasted_iota(jnp.int32, (tr, tr), 0 if transpose else 1)
    src = (i_out % per) * dil + i_out // per
    return jnp.where(i_in == src, 1.0, 0.0).astype(BF16)


def _rope_deint_kernel(q_ref, k_ref, v_ref, cos_ref, sin_ref, o_ref, *, tr, dil, gw):
    cos = cos_ref[...]
    sin = sin_ref[...]
    per = tr // dil
    perm = _deinterleave_matrix(tr, dil) if dil > 1 else None

    def emit(y16, col0):
        if dil > 1:
            y16 = jnp.dot(perm, y16, preferred_element_type=F32).astype(BF16)
        for r in range(dil):
            o_ref[0, r, :, col0:col0 + HEAD_DIM] = y16[r * per:(r + 1) * per, :]

    half = B_ROT_DIM // 2
    swap = _pair_swap_matrix(half) if dil == 1 else None
    lane = lax.broadcasted_iota(jnp.int32, (tr, HEAD_DIM), 1)
    first = (lane % (2 * half)) < half
    for part, ref in enumerate((q_ref, k_ref)):
        for hh in range(gw // HEAD_DIM):
            sl = slice(hh * HEAD_DIM, (hh + 1) * HEAD_DIM)
            x16 = ref[:, sl]
            x = x16.astype(F32)
            if dil == 1:
                rot = jnp.dot(x16, swap, preferred_element_type=F32)
            else:
                rot = jnp.where(first, pltpu.roll(x, LANES - half, 1), pltpu.roll(x, half, 1))
            y = x * cos + rot * sin
            emit(y.astype(BF16), part * gw + hh * HEAD_DIM)
    for hh in range(gw // HEAD_DIM):
        emit(v_ref[:, hh * HEAD_DIM:(hh + 1) * HEAD_DIM], 2 * gw + hh * HEAD_DIM)


def _rope_deint(proj_b, cos_t, sin_t, gi, dil, *, bsz, seq, n_groups, tr=256):
    gw = B_HEADS_PER_GROUP * HEAD_DIM
    nsb = seq // tr
    per = tr // dil
    return pl.pallas_call(
        functools.partial(_rope_deint_kernel, tr=tr, dil=dil, gw=gw),
        out_shape=jax.ShapeDtypeStruct((bsz, dil, seq // dil, 3 * gw), BF16),
        grid=(bsz, nsb),
        in_specs=[pl.BlockSpec((tr, gw), lambda b, i: (b * nsb + i, gi)),
                  pl.BlockSpec((tr, gw), lambda b, i: (b * nsb + i, n_groups + gi)),
                  pl.BlockSpec((tr, gw), lambda b, i: (b * nsb + i, 2 * n_groups + gi)),
                  pl.BlockSpec((tr, HEAD_DIM), lambda b, i: (i, 0)),
                  pl.BlockSpec((tr, HEAD_DIM), lambda b, i: (i, 0))],
        out_specs=pl.BlockSpec((1, dil, per, 3 * gw), lambda b, i: (b, 0, i, 0)),
        compiler_params=_cparams(("parallel", "parallel")),
        name=f"rope_deinterleave_d{dil}",
    )(proj_b, proj_b, proj_b, cos_t, sin_t)


_BW = 64
_BQ = 128


def _dil_kernel(q_ref, kc_ref, kp_ref, kn_ref, vc_ref, vp_ref, vn_ref, o_ref, l_ref, kext, vext,
                *, qb, nh, seq_len):
    i = pl.program_id(2)
    base = i * qb
    kext[0:_BW, :] = kp_ref[0, 0]
    kext[_BW:_BW + qb, :] = kc_ref[0, 0]
    kext[_BW + qb:, :] = kn_ref[0, 0]
    vext[0:_BW, :] = vp_ref[0, 0]
    vext[_BW:_BW + qb, :] = vc_ref[0, 0]
    vext[_BW + qb:, :] = vn_ref[0, 0]
    scale = HEAD_DIM ** -0.5
    nkeys = _BQ + 2 * _BW
    qi = lax.broadcasted_iota(jnp.int32, (_BQ, nkeys), 0)
    kj = lax.broadcasted_iota(jnp.int32, (_BQ, nkeys), 1)
    lane = lax.broadcasted_iota(jnp.int32, (_BQ, LANES), 1)
    for sb in range(qb // _BQ):
        qpos = base + sb * _BQ + qi
        kpos = base + sb * _BQ - _BW + kj
        valid = (jnp.abs(kpos - qpos) <= _BW) & (kpos >= 0) & (kpos < seq_len)
        lse_blk = jnp.zeros((_BQ, LANES), F32)
        for hh in range(nh):
            sl = slice(hh * HEAD_DIM, (hh + 1) * HEAD_DIM)
            q = q_ref[0, 0, sb * _BQ:(sb + 1) * _BQ, sl]
            kx = kext[sb * _BQ:sb * _BQ + nkeys, sl]
            vx = vext[sb * _BQ:sb * _BQ + nkeys, sl]
            s = lax.dot_general(q, kx, (((1,), (1,)), ((), ())), preferred_element_type=F32) * scale
            s = jnp.where(valid, s, -1e30)
            m = jnp.max(s, axis=-1, keepdims=True)
            e = jnp.where(valid, jnp.exp(s - m), 0.0)
            l = jnp.sum(e, axis=-1, keepdims=True)
            p = (e / l).astype(BF16)
            o = jnp.dot(p, vx, preferred_element_type=F32)
            o_ref[0, 0, sb * _BQ:(sb + 1) * _BQ, sl] = o.astype(o_ref.dtype)
            lse_blk = jnp.where(lane == hh, m + jnp.log(l), lse_blk)
        l_ref[0, 0, sb * _BQ:(sb + 1) * _BQ, :] = lse_blk


def _dilated_group(grp_qkv):
    bsz, dil, ln, _ = grp_qkv.shape
    gw = B_HEADS_PER_GROUP * HEAD_DIM
    qb = min(512, ln)
    hb = qb // _BW
    nhalo = ln // _BW
    lo = lambda i: jnp.maximum(i * hb - 1, 0)
    hi = lambda i: jnp.minimum((i + 1) * hb, nhalo - 1)

    def cur(col):
        return pl.BlockSpec((1, 1, qb, gw), lambda b, r, i: (b, r, i, col))

    def halo(col, f):
        return pl.BlockSpec((1, 1, _BW, gw), lambda b, r, i: (b, r, f(i), col))

    return pl.pallas_call(
        functools.partial(_dil_kernel, qb=qb, nh=B_HEADS_PER_GROUP, seq_len=ln),
        out_shape=(jax.ShapeDtypeStruct((bsz, dil, ln, gw), BF16),
                   jax.ShapeDtypeStruct((bsz, dil, ln, LANES), F32)),
        grid=(bsz, dil, ln // qb),
        in_specs=[cur(0), cur(1), halo(1, lo), halo(1, hi), cur(2), halo(2, lo), halo(2, hi)],
        out_specs=(pl.BlockSpec((1, 1, qb, gw), lambda b, r, i: (b, r, i, 0)),
                   pl.BlockSpec((1, 1, qb, LANES), lambda b, r, i: (b, r, i, 0))),
        scratch_shapes=[pltpu.VMEM((qb + 2 * _BW, gw), BF16), pltpu.VMEM((qb + 2 * _BW, gw), BF16)],
        compiler_params=_cparams(("parallel", "parallel", "parallel")),
        name=f"dilated_attn_d{dil}",
    )(*([grp_qkv] * 7))


def _merge_kernel(*refs, tr, dils, nh):
    ng = len(dils)
    o_refs = refs[:ng]
    l_refs = refs[ng:2 * ng]
    out_ref = refs[2 * ng]
    outs, lses = [], []
    for g, dil in enumerate(dils):
        per = tr // dil
        if dil == 1:
            outs.append(o_refs[g][0, 0].astype(F32))
            lses.append(l_refs[g][0, 0])
            continue
        pt = _deinterleave_matrix(tr, dil, transpose=True)
        y = jnp.concatenate([o_refs[g][0, r] for r in range(dil)], axis=0)
        outs.append(jnp.dot(pt, y, preferred_element_type=F32))
        ls = jnp.concatenate([l_refs[g][0, r] for r in range(dil)], axis=0)
        h1 = ls.astype(BF16)
        r1 = ls - h1.astype(F32)
        h2 = r1.astype(BF16)
        h3 = (r1 - h2.astype(F32)).astype(BF16)
        lses.append(jnp.dot(pt, h1, preferred_element_type=F32)
                    + (jnp.dot(pt, h2, preferred_element_type=F32) + jnp.dot(pt, h3, preferred_element_type=F32)))
    mx = lses[0]
    for g in range(1, ng):
        mx = jnp.maximum(mx, lses[g])
    es = [jnp.exp(l - mx) for l in lses]
    den = es[0]
    for g in range(1, ng):
        den = den + es[g]
    ws = [e / den for e in es]
    lane = lax.broadcasted_iota(jnp.int32, (tr, LANES), 1)
    for hh in range(nh):
        sl = slice(hh * HEAD_DIM, (hh + 1) * HEAD_DIM)
        acc = None
        for g in range(ng):
            wcol = jnp.sum(jnp.where(lane == hh, ws[g], 0.0), axis=1, keepdims=True)
            term = outs[g][:, sl] * wcol
            acc = term if acc is None else acc + term
        out_ref[:, sl] = acc.astype(out_ref.dtype)


def _merge_groups(os_, ls_, *, bsz, seq, tr=256):
    gw = B_HEADS_PER_GROUP * HEAD_DIM
    dils = tuple(o.shape[1] for o in os_)
    nsb = seq // tr
    in_specs = ([pl.BlockSpec((1, d, tr // d, gw), lambda b, i: (b, 0, i, 0)) for d in dils]
                + [pl.BlockSpec((1, d, tr // d, LANES), lambda b, i: (b, 0, i, 0)) for d in dils])
    return pl.pallas_call(
        functools.partial(_merge_kernel, tr=tr, dils=dils, nh=B_HEADS_PER_GROUP),
        out_shape=jax.ShapeDtypeStruct((bsz * seq, gw), BF16),
        grid=(bsz, nsb),
        in_specs=in_specs,
        out_specs=pl.BlockSpec((tr, gw), lambda b, i: (b * nsb + i, 0)),
        compiler_params=_cparams(("parallel", "parallel")),
        name="dilated_merge",
    )(*os_, *ls_)


def _qknorm_rope_kernel(x_ref, wq_ref, wk_ref, cos_ref, sin_ref, o_ref, *, cw, n_q_blocks):
    j = pl.program_id(1)
    is_q = j < n_q_blocks
    w = jnp.where(is_q, wq_ref[...], wk_ref[...])
    scale = jnp.where(is_q, HEAD_DIM ** -0.5 * LOG2E, 1.0).astype(F32)
    cos = cos_ref[...]
    sin = sin_ref[...]
    swap = _pair_swap_matrix(HEAD_DIM // 4)
    for hh in range(cw // HEAD_DIM):
        sl = slice(hh * HEAD_DIM, (hh + 1) * HEAD_DIM)
        x = x_ref[:, sl].astype(F32)
        ms = jnp.mean(x * x, axis=-1, keepdims=True)
        y = x * lax.rsqrt(ms + NORM_EPS) * w
        y = y * cos + _rotate_pairs(y, swap) * sin
        o_ref[:, sl] = (y * scale).astype(o_ref.dtype)


def _qknorm_rope(qkv, q_norm, k_norm, cos_t, sin_t, *, q_width, qk_width, seq, tr=512, cw=512):
    t = qkv.shape[0]
    tr = min(tr, seq)
    nsb = seq // tr
    return pl.pallas_call(
        functools.partial(_qknorm_rope_kernel, cw=cw, n_q_blocks=q_width // cw),
        out_shape=jax.ShapeDtypeStruct((t, qk_width), BF16),
        grid=(t // tr, qk_width // cw),
        in_specs=[pl.BlockSpec((tr, cw), lambda i, j: (i, j)),
                  pl.BlockSpec((1, HEAD_DIM), lambda i, j: (0, 0)),
                  pl.BlockSpec((1, HEAD_DIM), lambda i, j: (0, 0)),
                  pl.BlockSpec((tr, HEAD_DIM), lambda i, j: (i % nsb, 0)),
                  pl.BlockSpec((tr, HEAD_DIM), lambda i, j: (i % nsb, 0))],
        out_specs=pl.BlockSpec((tr, cw), lambda i, j: (i, j)),
        compiler_params=_cparams(("parallel", "parallel")),
        name="qk_norm_axial_rope",
    )(qkv, q_norm.reshape(1, HEAD_DIM).astype(F32), k_norm.reshape(1, HEAD_DIM).astype(F32), cos_t, sin_t)


def _flash_kernel(q_ref, k_ref, v_ref, o_ref, qs_ref, v1_ref, m_ref, acc_ref, *, tq, tk, rep, seq):
    @pl.when(pl.program_id(2) == 0)
    def _():
        v1_ref[:, :HEAD_DIM] = v_ref[0]
        v1_ref[:, HEAD_DIM:] = jnp.ones((seq, HEAD_DIM), BF16)

    for r in range(rep):
        qs_ref[r * tq:(r + 1) * tq, :] = q_ref[0, :, r * HEAD_DIM:(r + 1) * HEAD_DIM]
    m_ref[...] = jnp.full(m_ref.shape, -1e30, F32)
    acc_ref[...] = jnp.zeros(acc_ref.shape, F32)

    def body(t, carry):
        off = pl.multiple_of(t * tk, tk)
        kt = k_ref[0, pl.ds(off, tk), :]
        vt = v1_ref[pl.ds(off, tk), :]
        s = [lax.dot_general(qs_ref[r * tq:(r + 1) * tq, :], kt, (((1,), (1,)), ((), ())),
                             preferred_element_type=F32) for r in range(rep)]
        for r in range(rep):
            rows = slice(r * tq, (r + 1) * tq)
            m_old = m_ref[rows, :]
            m_new = jnp.maximum(m_old, jnp.max(s[r], axis=-1, keepdims=True))
            p = jnp.exp2(s[r] - m_new).astype(BF16)
            pv = jnp.dot(p, vt, preferred_element_type=F32)
            acc_ref[rows, :] = acc_ref[rows, :] * jnp.exp2(m_old - m_new) + pv
            m_ref[rows, :] = m_new
        return carry

    lax.fori_loop(0, seq // tk, body, 0)
    for r in range(rep):
        a = acc_ref[r * tq:(r + 1) * tq, :]
        o_ref[0, :, r * HEAD_DIM:(r + 1) * HEAD_DIM] = (a[:, :HEAD_DIM] / a[:, HEAD_DIM:]).astype(o_ref.dtype)


def _flash_attention(qk, qkv, *, bsz, seq, tq=512, tk=4096):
    qw = C_Q_HEADS * HEAD_DIM
    tq, tk = min(tq, seq), min(tk, seq)
    gq = C_REP * HEAD_DIM
    qk3 = qk.reshape(bsz, seq, qk.shape[-1])
    qkv3 = qkv.reshape(bsz, seq, qkv.shape[-1])
    k_blk0 = qw // HEAD_DIM
    v_blk0 = (qw + C_KV_HEADS * HEAD_DIM) // HEAD_DIM
    out = pl.pallas_call(
        functools.partial(_flash_kernel, tq=tq, tk=tk, rep=C_REP, seq=seq),
        out_shape=jax.ShapeDtypeStruct((bsz, seq, qw), BF16),
        grid=(bsz, C_KV_HEADS, seq // tq),
        in_specs=[pl.BlockSpec((1, tq, gq), lambda b, g, i: (b, i, g)),
                  pl.BlockSpec((1, seq, HEAD_DIM), lambda b, g, i: (b, 0, k_blk0 + g)),
                  pl.BlockSpec((1, seq, HEAD_DIM), lambda b, g, i: (b, 0, v_blk0 + g))],
        out_specs=pl.BlockSpec((1, tq, gq), lambda b, g, i: (b, i, g)),
        scratch_shapes=[pltpu.VMEM((C_REP * tq, HEAD_DIM), BF16),
                        pltpu.VMEM((seq, 2 * HEAD_DIM), BF16),
                        pltpu.VMEM((C_REP * tq, 1), F32),
                        pltpu.VMEM((C_REP * tq, 2 * HEAD_DIM), F32)],
        compiler_params=_cparams(("parallel", "parallel", "arbitrary")),
        name="gqa_flash_attention",
    )(qk3, qk3, qkv3)
    return out.reshape(bsz * seq, qw)


def _rope_cos_sin(pos, dim, theta):
    inv = 1.0 / (theta ** (jnp.arange(0, dim, 2, dtype=F32) / dim))
    ang = pos.astype(F32)[:, None] * inv[None, :]
    return jnp.cos(ang), jnp.sin(ang)


def _tables(seq):
    tok = jnp.arange(seq)
    cos_b, sin_b = _rope_cos_sin(tok, B_ROT_DIM, B_ROPE_THETA)
    ones = jnp.ones((seq, HEAD_DIM - B_ROT_DIM), F32)
    cb = jnp.concatenate([cos_b, cos_b, ones], axis=-1)
    sb = jnp.concatenate([-sin_b, sin_b, jnp.zeros_like(ones)], axis=-1)
    cos_r, sin_r = _rope_cos_sin(tok // GRID_W, HEAD_DIM // 2, C_ROPE_THETA)
    cos_c, sin_c = _rope_cos_sin(tok % GRID_W, HEAD_DIM // 2, C_ROPE_THETA)
    cc = jnp.concatenate([cos_r, cos_r, cos_c, cos_c], axis=-1)
    sc = jnp.concatenate([-sin_r, sin_r, -sin_c, sin_c], axis=-1)
    return cb, sb, cc, sc


def _ffn(x2d, norm_w, w_gate, w_up, w_down, layer):
    h = _rmsnorm(x2d, norm_w, BF16)
    act = _gateup(h, w_gate, w_up, layer, tm=2048, tn=256)
    dff = act.shape[1]
    nsplit = 2 if (dff // 2) % LANES == 0 else 1
    for kb in range(nsplit):
        x2d = _wmatmul([(act, kb, dff // nsplit)], w_down, w_lead=layer, w_row_blk=kb, n=x2d.shape[1],
                       tm=1024, tn=512, out_dtype=F32, residual=x2d)
    return x2d


def _mixer_ab(x2d, norm_w, w_in, conv_w, a_log, dt_bias, out_norm, w_out, j, cb, sb, *, bsz, seq):
    t, dm = x2d.shape
    n_heads = a_log.shape[-1]
    a_qk = n_heads * HEAD_DIM
    a_v = n_heads * HEAD_DIM
    a_qkv = 2 * a_qk + a_v
    ngate = 2 * n_heads
    b_w = (w_in.shape[-1] - a_qkv - a_v - 2 * ngate) // 3
    h = _rmsnorm(x2d, norm_w, BF16)

    w_in_t = jnp.swapaxes(w_in, 1, 2)
    g0 = a_qkv + a_v
    w_beta = w_in_t[j, g0:g0 + ngate].reshape(2, n_heads, dm)
    w_alpha = w_in_t[j, g0 + ngate:g0 + 2 * ngate].reshape(2, n_heads, dm)
    w_g = jnp.concatenate([w_beta, w_alpha, jnp.zeros((2, LANES - 2 * n_heads, dm), F32)], axis=1)
    w_g = w_g.reshape(1, 2 * LANES, dm)
    w_b = _shift_cast_rows(w_in_t, j, g0 + 2 * ngate, 3 * b_w)

    proj_a = _wmatmul_t(h, w_in_t, w_lead=j, n=a_qkv + a_v, tm=2048, tn=512, out_dtype=BF16)
    gates = _wmatmul_t(h, w_g, w_lead=0, n=2 * LANES, tm=2048, tn=2 * LANES, out_dtype=F32)
    proj_b = _wmatmul_t(h, w_b, w_lead=0, n=3 * b_w, tm=2048, tn=512, out_dtype=BF16)

    qkv = _conv_silu_l2(proj_a.reshape(bsz, seq, -1), conv_w, qk_width=2 * a_qk, q_width=a_qk, n_ch=a_qkv)
    pad = jnp.zeros((2, LANES - 2 * n_heads), F32)
    alog_row = jnp.concatenate([jnp.zeros((2, n_heads), F32), a_log.astype(F32), pad], axis=-1).reshape(2, 1, LANES)
    dtb_row = jnp.concatenate([jnp.zeros((2, n_heads), F32), dt_bias.astype(F32), pad], axis=-1).reshape(2, 1, LANES)
    o2 = _delta_scan(qkv, gates.reshape(bsz, seq, 2 * LANES), alog_row, dtb_row, n_heads=n_heads)
    o_a = _delta_out(o2.reshape(2, t, a_v), proj_a, out_norm, z_col_block=a_qkv // 512)

    n_groups = len(B_PATTERNS)
    os_, ls_ = [], []
    for gi, (_, dil) in enumerate(B_PATTERNS):
        grp = _rope_deint(proj_b, cb, sb, gi, dil, bsz=bsz, seq=seq, n_groups=n_groups)
        o_g, l_g = _dilated_group(grp)
        os_.append(o_g)
        ls_.append(l_g)
    o_b = _merge_groups(os_, ls_, bsz=bsz, seq=seq)

    return _wmatmul([(o_a, 0, a_v), (o_b, 0, o_b.shape[1])], w_out, w_lead=j, n=dm, tm=1024, tn=512,
                    out_dtype=F32, residual=x2d)


def _mixer_c(x2d, norm_w, w_qkv, q_norm, k_norm, w_out, j, cc, sc, *, bsz, seq):
    dm = x2d.shape[1]
    h = _rmsnorm(x2d, norm_w, BF16)
    qkv = _wmatmul([(h, 0, dm)], w_qkv, w_lead=j, n=w_qkv.shape[-1], tm=2048, tn=512, out_dtype=BF16)
    qw = C_Q_HEADS * HEAD_DIM
    kw = C_KV_HEADS * HEAD_DIM
    qk = _qknorm_rope(qkv, q_norm, k_norm, cc, sc, q_width=qw, qk_width=qw + kw, seq=seq)
    o = _flash_attention(qk, qkv, bsz=bsz, seq=seq)
    return _wmatmul([(o, 0, qw)], w_out, w_lead=j, n=dm, tm=1024, tn=512, out_dtype=F32, residual=x2d)


def kernel(x, norm_mix, norm_ffn, norm_final, ab_w_in, ab_conv_w, ab_a_log, ab_dt_bias, ab_out_norm, ab_w_out,
           c_w_qkv, c_q_norm, c_k_norm, c_w_out, ffn_w_gate, ffn_w_up, ffn_w_down):
    bsz, seq, dm = x.shape
    depth = norm_mix.shape[0]
    cb, sb, cc, sc = _tables(seq)
    x2d = x.reshape(bsz * seq, dm)
    for layer in range(depth):
        j = layer // 2
        if layer % 2 == 0:
            x2d = _mixer_ab(x2d, norm_mix[layer], ab_w_in, ab_conv_w[j], ab_a_log[j], ab_dt_bias[j],
                            ab_out_norm[j], ab_w_out, j, cb, sb, bsz=bsz, seq=seq)
        else:
            x2d = _mixer_c(x2d, norm_mix[layer], c_w_qkv, c_q_norm[j], c_k_norm[j], c_w_out, j,
                           cc, sc, bsz=bsz, seq=seq)
        x2d = _ffn(x2d, norm_ffn[layer], ffn_w_gate, ffn_w_up, ffn_w_down, layer)
    out = _rmsnorm(x2d, norm_final, x.dtype)
    return out.reshape(bsz, seq, dm)
```

```python
import functools

import jax
import jax.numpy as jnp
from jax import lax
from jax.experimental import pallas as pl
from jax.experimental.pallas import tpu as pltpu

HEAD_DIM = 128
NORM_EPS = 1e-6
L2_EPS = 1e-6
GRID_W = 64
A_CONV_W = 5
A_CHUNK = 64
B_PATTERNS = ((128, 1), (512, 4), (2048, 16))
B_HEADS_PER_GROUP = 8
B_ROT_DIM = HEAD_DIM // 4
B_ROPE_THETA = 500000.0
C_Q_HEADS = 32
C_KV_HEADS = 8
C_REP = C_Q_HEADS // C_KV_HEADS
C_ROPE_THETA = 10000.0
LOG2E = 1.4426950408889634

LANES = 128
VMEM_LIMIT = 56 * 1024 * 1024

BF16 = jnp.bfloat16
F32 = jnp.float32
HI = lax.Precision.HIGHEST


def _cparams(sem, vmem=VMEM_LIMIT):
    return pltpu.CompilerParams(dimension_semantics=sem, vmem_limit_bytes=vmem)


def _rmsnorm_kernel(x_ref, w_ref, o_ref):
    x = x_ref[...].astype(F32)
    ms = jnp.mean(x * x, axis=-1, keepdims=True)
    o_ref[...] = (x * lax.rsqrt(ms + NORM_EPS) * w_ref[...]).astype(o_ref.dtype)


def _rmsnorm(x2d, w, out_dtype, tr=512):
    m, d = x2d.shape
    return pl.pallas_call(
        _rmsnorm_kernel,
        out_shape=jax.ShapeDtypeStruct((m, d), out_dtype),
        grid=(m // tr,),
        in_specs=[pl.BlockSpec((tr, d), lambda i: (i, 0)),
                  pl.BlockSpec((1, d), lambda i: (0, 0))],
        out_specs=pl.BlockSpec((tr, d), lambda i: (i, 0)),
        compiler_params=_cparams(("parallel",)),
        name="rmsnorm",
    )(x2d, w.reshape(1, d).astype(F32))


def _stage_weight_chunk(chunk_ref, w16, ck):
    jp = pl.program_id(0)
    i = pl.program_id(1)
    w16[jp % 2, pl.ds(pl.multiple_of(i * ck, 16), ck), :] = chunk_ref[...].astype(BF16)


def _wmm_kernel(*refs, ksizes, has_res, ck):
    na = len(ksizes)
    a_refs = refs[:na]
    chunk_ref = refs[na]
    r_ref = refs[na + 1] if has_res else None
    o_ref = refs[na + 1 + int(has_res)]
    w16 = refs[na + 2 + int(has_res)]
    jp = pl.program_id(0)
    _stage_weight_chunk(chunk_ref, w16, ck)

    @pl.when(jp > 0)
    def _():
        slot = (jp + 1) % 2
        off = 0
        acc = None
        for a_ref, ks in zip(a_refs, ksizes):
            part = jnp.dot(a_ref[...], w16[slot, off:off + ks, :], preferred_element_type=F32)
            acc = part if acc is None else acc + part
            off += ks
        if has_res:
            acc = acc + r_ref[...]
        o_ref[...] = acc.astype(o_ref.dtype)


def _wmatmul(a_ops, w, *, w_lead, w_row_blk=0, w_col_blk0=0, n, tm, tn, out_dtype, residual=None):
    m = a_ops[0][0].shape[0]
    ksizes = tuple(k for _, _, k in a_ops)
    ktot = sum(ksizes)
    tm, tn = min(tm, m), min(tn, n)
    ni, nj = m // tm, n // tn
    ck = ktot // ni
    assert m % tm == 0 and n % tn == 0 and ktot % ni == 0 and ck % 16 == 0
    row_of = lambda jp, i: jnp.where(jp == 0, 0, i)
    col_of = lambda jp: jnp.maximum(jp - 1, 0)
    in_specs = [pl.BlockSpec((tm, ks), functools.partial(lambda jp, i, blk: (row_of(jp, i), blk), blk=blk))
                for _, blk, ks in a_ops]
    in_specs.append(pl.BlockSpec((None, ck, tn),
                                 lambda jp, i: (w_lead, w_row_blk * ni + i, w_col_blk0 + jnp.minimum(jp, nj - 1))))
    args = [a for a, _, _ in a_ops] + [w]
    if residual is not None:
        in_specs.append(pl.BlockSpec((tm, tn), lambda jp, i: (row_of(jp, i), col_of(jp))))
        args.append(residual)
    return pl.pallas_call(
        functools.partial(_wmm_kernel, ksizes=ksizes, has_res=residual is not None, ck=ck),
        out_shape=jax.ShapeDtypeStruct((m, n), out_dtype),
        grid=(nj + 1, ni),
        in_specs=in_specs,
        out_specs=pl.BlockSpec((tm, tn), lambda jp, i: (row_of(jp, i), col_of(jp))),
        scratch_shapes=[pltpu.VMEM((2, ktot, tn), BF16)],
        compiler_params=_cparams(("arbitrary", "arbitrary")),
        name="proj_matmul",
    )(*args)


def _wmm_t_kernel(a_ref, chunk_ref, o_ref, w16, *, ck):
    jp = pl.program_id(0)
    i = pl.program_id(1)
    w16[jp % 2, :, pl.ds(pl.multiple_of(i * ck, LANES), ck)] = chunk_ref[...].astype(BF16)

    @pl.when(jp > 0)
    def _():
        acc = lax.dot_general(a_ref[...], w16[(jp + 1) % 2], (((1,), (1,)), ((), ())),
                              preferred_element_type=F32)
        o_ref[...] = acc.astype(o_ref.dtype)


def _wmatmul_t(a, wt, *, w_lead, w_row_blk0=0, n, tm, tn, out_dtype):
    m, kdim = a.shape
    tm, tn = min(tm, m), min(tn, n)
    ni, nj = m // tm, n // tn
    ck = kdim // ni
    assert m % tm == 0 and n % tn == 0 and kdim % ni == 0 and ck % LANES == 0
    row_of = lambda jp, i: jnp.where(jp == 0, 0, i)
    col_of = lambda jp: jnp.maximum(jp - 1, 0)
    return pl.pallas_call(
        functools.partial(_wmm_t_kernel, ck=ck),
        out_shape=jax.ShapeDtypeStruct((m, n), out_dtype),
        grid=(nj + 1, ni),
        in_specs=[pl.BlockSpec((tm, kdim), lambda jp, i: (row_of(jp, i), 0)),
                  pl.BlockSpec((None, tn, ck), lambda jp, i: (w_lead, w_row_blk0 + jnp.minimum(jp, nj - 1), i))],
        out_specs=pl.BlockSpec((tm, tn), lambda jp, i: (row_of(jp, i), col_of(jp))),
        scratch_shapes=[pltpu.VMEM((2, tn, kdim), BF16)],
        compiler_params=_cparams(("arbitrary", "arbitrary")),
        name="proj_matmul_wt",
    )(a, wt)


def _gateup_kernel(h_ref, cg_ref, cu_ref, o_ref, wg16, wu16, *, ck):
    jp = pl.program_id(0)
    _stage_weight_chunk(cg_ref, wg16, ck)
    _stage_weight_chunk(cu_ref, wu16, ck)

    @pl.when(jp > 0)
    def _():
        slot = (jp + 1) % 2
        h = h_ref[...]
        g = jnp.dot(h, wg16[slot], preferred_element_type=F32)
        u = jnp.dot(h, wu16[slot], preferred_element_type=F32)
        o_ref[...] = (g * jax.nn.sigmoid(g) * u).astype(o_ref.dtype)


def _gateup(h, w_gate, w_up, layer, *, tm, tn):
    m, kdim = h.shape
    n = w_gate.shape[-1]
    tm, tn = min(tm, m), min(tn, n)
    ni, nj = m // tm, n // tn
    ck = kdim // ni
    assert m % tm == 0 and n % tn == 0 and kdim % ni == 0 and ck % 16 == 0
    wspec = pl.BlockSpec((None, ck, tn), lambda jp, i: (layer, i, jnp.minimum(jp, nj - 1)))
    return pl.pallas_call(
        functools.partial(_gateup_kernel, ck=ck),
        out_shape=jax.ShapeDtypeStruct((m, n), BF16),
        grid=(nj + 1, ni),
        in_specs=[pl.BlockSpec((tm, kdim), lambda jp, i: (jnp.where(jp == 0, 0, i), 0)), wspec, wspec],
        out_specs=pl.BlockSpec((tm, tn), lambda jp, i: (jnp.where(jp == 0, 0, i), jnp.maximum(jp - 1, 0))),
        scratch_shapes=[pltpu.VMEM((2, kdim, tn), BF16), pltpu.VMEM((2, kdim, tn), BF16)],
        compiler_params=_cparams(("arbitrary", "arbitrary")),
        name="ffn_gate_up",
    )(h, w_gate, w_up)


def _shift_cast_kernel(main_ref, next_ref, o_ref, *, shift):
    tr = main_ref.shape[0]
    o_ref[0:tr - shift, :] = main_ref[shift:, :].astype(o_ref.dtype)
    o_ref[tr - shift:, :] = next_ref[...].astype(o_ref.dtype)


def _shift_cast_rows(wt, lead, row0, nrows, *, tr=512, tc=1024):
    _, _, kdim = wt.shape
    base = (row0 // tr) * tr
    shift = row0 - base
    tc = min(tc, kdim)
    assert shift > 0 and shift % 16 == 0 and tr % shift == 0 and nrows % tr == 0 and kdim % tc == 0
    per = tr // shift
    return pl.pallas_call(
        functools.partial(_shift_cast_kernel, shift=shift),
        out_shape=jax.ShapeDtypeStruct((1, nrows, kdim), BF16),
        grid=(nrows // tr, kdim // tc),
        in_specs=[pl.BlockSpec((None, tr, tc), lambda i, j: (lead, base // tr + i, j)),
                  pl.BlockSpec((None, shift, tc), lambda i, j: (lead, (base // tr + i + 1) * per, j))],
        out_specs=pl.BlockSpec((None, tr, tc), lambda i, j: (0, i, j)),
        compiler_params=_cparams(("parallel", "parallel")),
        name="weight_shift_cast",
    )(wt, wt)


_CONV_HALO = 16


def _conv_kernel(prev_ref, cur_ref, next_ref, w_ref, o_ref, ext_ref, *, ts, cw, n_qk_blocks, n_q_blocks):
    i = pl.program_id(1)
    nblk = pl.num_programs(1)
    j = pl.program_id(2)
    pad = A_CONV_W // 2
    prev = prev_ref[0].astype(F32)
    nxt = next_ref[0].astype(F32)
    ext_ref[0:_CONV_HALO, :] = jnp.where(i > 0, prev, 0.0)
    ext_ref[_CONV_HALO:_CONV_HALO + ts, :] = cur_ref[0].astype(F32)
    ext_ref[_CONV_HALO + ts:, :] = jnp.where(i < nblk - 1, nxt, 0.0)
    w = w_ref[...]
    ext = ext_ref[...]
    nrow = ts + 2 * _CONV_HALO
    y = ext[_CONV_HALO:_CONV_HALO + ts, :] * w[pad:pad + 1, :]
    for t in range(A_CONV_W):
        if t != pad:
            rolled = pltpu.roll(ext, (pad - t) % nrow, 0)
            y = y + rolled[_CONV_HALO:_CONV_HALO + ts, :] * w[t:t + 1, :]
    y = y * jax.nn.sigmoid(y)
    is_qk = j < n_qk_blocks
    scale = jnp.where(j < n_q_blocks, HEAD_DIM ** -0.5, 1.0).astype(F32)
    for hh in range(cw // HEAD_DIM):
        yh = y[:, hh * HEAD_DIM:(hh + 1) * HEAD_DIM]
        ss = jnp.sum(yh * yh, axis=-1, keepdims=True)
        yn = yh * (lax.rsqrt(ss + L2_EPS) * scale)
        o_ref[0, :, hh * HEAD_DIM:(hh + 1) * HEAD_DIM] = jnp.where(is_qk, yn, yh).astype(o_ref.dtype)


def _conv_silu_l2(proj_a, conv_w, *, qk_width, q_width, n_ch, ts=1024, cw=512):
    bsz, seq, _ = proj_a.shape
    ts = min(ts, seq)
    hb = ts // _CONV_HALO
    nhalo = seq // _CONV_HALO
    w8 = jnp.zeros((8, n_ch), F32).at[:A_CONV_W].set(conv_w.astype(F32))
    kern = functools.partial(_conv_kernel, ts=ts, cw=cw, n_qk_blocks=qk_width // cw, n_q_blocks=q_width // cw)
    return pl.pallas_call(
        kern,
        out_shape=jax.ShapeDtypeStruct((bsz, seq, n_ch), BF16),
        grid=(bsz, seq // ts, n_ch // cw),
        in_specs=[
            pl.BlockSpec((1, _CONV_HALO, cw), lambda b, i, j: (b, jnp.maximum(i * hb - 1, 0), j)),
            pl.BlockSpec((1, ts, cw), lambda b, i, j: (b, i, j)),
            pl.BlockSpec((1, _CONV_HALO, cw), lambda b, i, j: (b, jnp.minimum((i + 1) * hb, nhalo - 1), j)),
            pl.BlockSpec((8, cw), lambda b, i, j: (0, j)),
        ],
        out_specs=pl.BlockSpec((1, ts, cw), lambda b, i, j: (b, i, j)),
        scratch_shapes=[pltpu.VMEM((ts + 2 * _CONV_HALO, cw), F32)],
        compiler_params=_cparams(("parallel", "parallel", "parallel")),
        name="conv_silu_l2norm",
    )(proj_a, proj_a, proj_a, w8)


_INV_BASE = 8


def _split_bf16(x):
    hi = x.astype(BF16)
    lo = (x - hi.astype(F32)).astype(BF16)
    return hi, lo


def _dot3(a, b):
    a_hi, a_lo = a
    b_hi, b_lo = b
    return (jnp.dot(a_hi, b_hi, preferred_element_type=F32)
            + (jnp.dot(a_hi, b_lo, preferred_element_type=F32) + jnp.dot(a_lo, b_hi, preferred_element_type=F32)))


def _delta_kernel(q_ref, k_ref, v_ref, gt_ref, alog_ref, dtb_ref, o_ref, state_ref, *, hg, n_heads):
    c = A_CHUNK
    d = pl.program_id(0) % 2
    grp = pl.program_id(1)
    n = pl.program_id(2)

    @pl.when(n == 0)
    def _():
        state_ref[...] = jnp.zeros_like(state_ref)

    row = lax.broadcasted_iota(jnp.int32, (c, c), 0)
    col = lax.broadcasted_iota(jnp.int32, (c, c), 1)
    ahead = jnp.where(d == 0, row - col, col - row)
    incl = ahead >= 0
    strict = ahead > 0
    tri = jnp.where(incl, 1.0, 0.0).astype(F32)
    eye = jnp.where(row == col, 1.0, 0.0).astype(F32)

    graw = gt_ref[0]
    beta_all = jax.nn.sigmoid(graw)
    z = graw + dtb_ref[0]
    softplus = jnp.maximum(z, 0.0) + jnp.log(1.0 + jnp.exp(-jnp.abs(z)))
    g_all = -jnp.exp(alog_ref[0]) * softplus
    gc_all = jnp.dot(tri, g_all, precision=HI, preferred_element_type=F32)
    gct_all = lax.dot_general(g_all, tri, (((0,), (1,)), ((), ())), precision=HI,
                              preferred_element_type=F32)
    gtot_all = jnp.sum(g_all, axis=0, keepdims=True)
    lane = lax.broadcasted_iota(jnp.int32, (c, LANES), 1)
    subl = lax.broadcasted_iota(jnp.int32, (LANES, c), 0)
    contract_last = (((1,), (1,)), ((), ()))
    contract_first = (((0,), (0,)), ((), ()))

    heads = range(hg)
    sls = [slice(hh * HEAD_DIM, (hh + 1) * HEAD_DIM) for hh in heads]
    q16 = [q_ref[0, :, sl] for sl in sls]
    k16 = [k_ref[0, :, sl] for sl in sls]
    gcol, beta, gtot, decay = [], [], [], []
    for hh in heads:
        a_lane = n_heads + grp * hg + hh
        sel_a = lane == a_lane
        sel_b = lane == (grp * hg + hh)
        gcol.append(jnp.sum(jnp.where(sel_a, gc_all, 0.0), axis=1, keepdims=True))
        beta.append(jnp.sum(jnp.where(sel_b, beta_all, 0.0), axis=1, keepdims=True))
        gtot.append(jnp.sum(jnp.where(sel_a[0:1], gtot_all, 0.0), axis=1, keepdims=True))
        grow = jnp.sum(jnp.where(subl == a_lane, gct_all, 0.0), axis=0, keepdims=True)
        diff = gcol[hh] - grow
        decay.append(jnp.where(incl, jnp.exp(jnp.where(incl, diff, 0.0)), 0.0))
    kb = [k16[hh].astype(F32) * beta[hh] for hh in heads]
    kq = [lax.dot_general(jnp.concatenate([kb[hh].astype(BF16), q16[hh]], axis=0), k16[hh], contract_last,
                          preferred_element_type=F32) for hh in heads]
    attn = [(kq[hh][c:] * decay[hh]).astype(BF16) for hh in heads]
    a_mat = [jnp.where(strict, kq[hh][:c] * decay[hh], 0.0) for hh in heads]
    same_base = (row // _INV_BASE) == (col // _INV_BASE)
    x = [jnp.where(same_base, -a_mat[hh], 0.0) for hh in heads]
    t_mat = [eye + x[hh] for hh in heads]
    p = 2
    while p < _INV_BASE:
        xs = [_split_bf16(x[hh]) for hh in heads]
        x = [_dot3(xs[hh], xs[hh]) for hh in heads]
        t_mat = [t_mat[hh] + _dot3(_split_bf16(t_mat[hh]), _split_bf16(x[hh])) for hh in heads]
        p *= 2
    s = _INV_BASE
    while s < c:
        pair = ((row // (2 * s)) == (col // (2 * s))) & ((row // s) != (col // s))
        ts = [_split_bf16(t_mat[hh]) for hh in heads]
        tl = [_dot3(ts[hh], _split_bf16(jnp.where(pair, a_mat[hh], 0.0))) for hh in heads]
        t_mat = [t_mat[hh] - _dot3(_split_bf16(tl[hh]), ts[hh]) for hh in heads]
        s *= 2
    eg = [jnp.exp(gcol[hh]) for hh in heads]
    rhs = [jnp.concatenate([v_ref[0, :, sls[hh]].astype(F32) * beta[hh], kb[hh] * eg[hh]], axis=1).astype(BF16)
           for hh in heads]
    uw = [jnp.dot(t_mat[hh].astype(BF16), rhs[hh], preferred_element_type=F32) for hh in heads]
    state = [state_ref[hh] for hh in heads]
    ws = [jnp.dot(jnp.concatenate([uw[hh][:, HEAD_DIM:], q16[hh].astype(F32) * eg[hh]], axis=0).astype(BF16),
                  state[hh].astype(BF16), preferred_element_type=F32) for hh in heads]
    vn16 = [(uw[hh][:, :HEAD_DIM] - ws[hh][:c]).astype(BF16) for hh in heads]
    for hh in heads:
        o = ws[hh][c:] + jnp.dot(attn[hh], vn16[hh], preferred_element_type=F32)
        o_ref[0, 0, :, sls[hh]] = o.astype(o_ref.dtype)
    for hh in heads:
        kd = (k16[hh].astype(F32) * jnp.exp(gtot[hh] - gcol[hh])).astype(BF16)
        state_ref[hh] = state[hh] * jnp.exp(gtot[hh]) + lax.dot_general(kd, vn16[hh], contract_first,
                                                                       preferred_element_type=F32)


def _delta_scan(qkv, gates, alog_row, dtb_row, *, n_heads, hg=16):
    bsz, seq, _ = qkv.shape
    c = A_CHUNK
    nchunk = seq // c
    ngrp = n_heads // hg
    wblk = hg * HEAD_DIM

    def cidx(bd, n):
        return jnp.where(bd % 2 == 0, n, nchunk - 1 - n)

    kern = functools.partial(_delta_kernel, hg=hg, n_heads=n_heads)
    return pl.pallas_call(
        kern,
        out_shape=jax.ShapeDtypeStruct((2, bsz, seq, n_heads * HEAD_DIM), F32),
        grid=(bsz * 2, ngrp, nchunk),
        in_specs=[
            pl.BlockSpec((1, c, wblk), lambda bd, g, n: (bd // 2, cidx(bd, n), g)),
            pl.BlockSpec((1, c, wblk), lambda bd, g, n: (bd // 2, cidx(bd, n), ngrp + g)),
            pl.BlockSpec((1, c, wblk), lambda bd, g, n: (bd // 2, cidx(bd, n), 2 * ngrp + g)),
            pl.BlockSpec((1, c, LANES), lambda bd, g, n: (bd // 2, cidx(bd, n), bd % 2)),
            pl.BlockSpec((1, 1, LANES), lambda bd, g, n: (bd % 2, 0, 0)),
            pl.BlockSpec((1, 1, LANES), lambda bd, g, n: (bd % 2, 0, 0)),
        ],
        out_specs=pl.BlockSpec((1, 1, c, wblk), lambda bd, g, n: (bd % 2, bd // 2, cidx(bd, n), g)),
        scratch_shapes=[pltpu.VMEM((hg, HEAD_DIM, HEAD_DIM), F32)],
        compiler_params=_cparams(("parallel", "parallel", "arbitrary")),
        name="gated_delta_scan",
    )(qkv, qkv, qkv, gates, alog_row, dtb_row)


def _delta_out_kernel(of_ref, ob_ref, z_ref, w_ref, o_ref, *, cw):
    o = of_ref[0] + ob_ref[0]
    z = z_ref[...].astype(F32)
    w = w_ref[...]
    for hh in range(cw // HEAD_DIM):
        sl = slice(hh * HEAD_DIM, (hh + 1) * HEAD_DIM)
        oh = o[:, sl]
        ms = jnp.mean(oh * oh, axis=-1, keepdims=True)
        zh = z[:, sl]
        o_ref[:, sl] = (oh * lax.rsqrt(ms + NORM_EPS) * w * (zh * jax.nn.sigmoid(zh))).astype(o_ref.dtype)


def _delta_out(o2, proj_a, out_norm, *, z_col_block, tr=1024, cw=512):
    _, t, vdim = o2.shape
    tr = min(tr, t)
    return pl.pallas_call(
        functools.partial(_delta_out_kernel, cw=cw),
        out_shape=jax.ShapeDtypeStruct((t, vdim), BF16),
        grid=(t // tr, vdim // cw),
        in_specs=[
            pl.BlockSpec((1, tr, cw), lambda i, j: (0, i, j)),
            pl.BlockSpec((1, tr, cw), lambda i, j: (1, i, j)),
            pl.BlockSpec((tr, cw), lambda i, j: (i, z_col_block + j)),
            pl.BlockSpec((1, HEAD_DIM), lambda i, j: (0, 0)),
        ],
        out_specs=pl.BlockSpec((tr, cw), lambda i, j: (i, j)),
        compiler_params=_cparams(("parallel", "parallel")),
        name="delta_out_norm_gate",
    )(o2, o2, proj_a, out_norm.reshape(1, HEAD_DIM).astype(F32))


def _pair_swap_matrix(half):
    src = lax.broadcasted_iota(jnp.int32, (LANES, LANES), 0)
    dst = lax.broadcasted_iota(jnp.int32, (LANES, LANES), 1)
    partner = jnp.where((dst % (2 * half)) < half, dst + half, dst - half)
    return jnp.where(src == partner, 1.0, 0.0).astype(BF16)


def _rotate_pairs(x, swap):
    hi, lo = _split_bf16(x)
    return jnp.dot(hi, swap, preferred_element_type=F32) + jnp.dot(lo, swap, preferred_element_type=F32)


def _deinterleave_matrix(tr, dil, transpose=False):
    per = tr // dil
    i_out = lax.broadcasted_iota(jnp.int32, (tr, tr), 1 if transpose else 0)
    i_in = lax.broadcasted_iota(jnp.int32, (tr, tr), 0 if transpose else 1)
    src = (i_out % per) * dil + i_out // per
    return jnp.where(i_in == src, 1.0, 0.0).astype(BF16)


def _rope_deint_kernel(q_ref, k_ref, v_ref, cos_ref, sin_ref, o_ref, *, tr, dil, gw):
    cos = cos_ref[...]
    sin = sin_ref[...]
    per = tr // dil
    perm = _deinterleave_matrix(tr, dil) if dil > 1 else None

    def emit(y16, col0):
        if dil > 1:
            y16 = jnp.dot(perm, y16, preferred_element_type=F32).astype(BF16)
        for r in range(dil):
            o_ref[0, r, :, col0:col0 + HEAD_DIM] = y16[r * per:(r + 1) * per, :]

    half = B_ROT_DIM // 2
    swap = _pair_swap_matrix(half) if dil == 1 else None
    lane = lax.broadcasted_iota(jnp.int32, (tr, HEAD_DIM), 1)
    first = (lane % (2 * half)) < half
    for part, ref in enumerate((q_ref, k_ref)):
        for hh in range(gw // HEAD_DIM):
            sl = slice(hh * HEAD_DIM, (hh + 1) * HEAD_DIM)
            x16 = ref[:, sl]
            x = x16.astype(F32)
            if dil == 1:
                rot = jnp.dot(x16, swap, preferred_element_type=F32)
            else:
                rot = jnp.where(first, pltpu.roll(x, LANES - half, 1), pltpu.roll(x, half, 1))
            y = x * cos + rot * sin
            emit(y.astype(BF16), part * gw + hh * HEAD_DIM)
    for hh in range(gw // HEAD_DIM):
        emit(v_ref[:, hh * HEAD_DIM:(hh + 1) * HEAD_DIM], 2 * gw + hh * HEAD_DIM)


def _rope_deint(proj_b, cos_t, sin_t, gi, dil, *, bsz, seq, n_groups, tr=256):
    gw = B_HEADS_PER_GROUP * HEAD_DIM
    nsb = seq // tr
    per = tr // dil
    return pl.pallas_call(
        functools.partial(_rope_deint_kernel, tr=tr, dil=dil, gw=gw),
        out_shape=jax.ShapeDtypeStruct((bsz, dil, seq // dil, 3 * gw), BF16),
        grid=(bsz, nsb),
        in_specs=[pl.BlockSpec((tr, gw), lambda b, i: (b * nsb + i, gi)),
                  pl.BlockSpec((tr, gw), lambda b, i: (b * nsb + i, n_groups + gi)),
                  pl.BlockSpec((tr, gw), lambda b, i: (b * nsb + i, 2 * n_groups + gi)),
                  pl.BlockSpec((tr, HEAD_DIM), lambda b, i: (i, 0)),
                  pl.BlockSpec((tr, HEAD_DIM), lambda b, i: (i, 0))],
        out_specs=pl.BlockSpec((1, dil, per, 3 * gw), lambda b, i: (b, 0, i, 0)),
        compiler_params=_cparams(("parallel", "parallel")),
        name=f"rope_deinterleave_d{dil}",
    )(proj_b, proj_b, proj_b, cos_t, sin_t)


_BW = 64
_BQ = 128


def _dil_kernel(q_ref, kc_ref, kp_ref, kn_ref, vc_ref, vp_ref, vn_ref, o_ref, l_ref, kext, vext,
                *, qb, nh, seq_len):
    i = pl.program_id(2)
    base = i * qb
    kext[0:_BW, :] = kp_ref[0, 0]
    kext[_BW:_BW + qb, :] = kc_ref[0, 0]
    kext[_BW + qb:, :] = kn_ref[0, 0]
    vext[0:_BW, :] = vp_ref[0, 0]
    vext[_BW:_BW + qb, :] = vc_ref[0, 0]
    vext[_BW + qb:, :] = vn_ref[0, 0]
    scale = HEAD_DIM ** -0.5
    nkeys = _BQ + 2 * _BW
    qi = lax.broadcasted_iota(jnp.int32, (_BQ, nkeys), 0)
    kj = lax.broadcasted_iota(jnp.int32, (_BQ, nkeys), 1)
    lane = lax.broadcasted_iota(jnp.int32, (_BQ, LANES), 1)
    for sb in range(qb // _BQ):
        qpos = base + sb * _BQ + qi
        kpos = base + sb * _BQ - _BW + kj
        valid = (jnp.abs(kpos - qpos) <= _BW) & (kpos >= 0) & (kpos < seq_len)
        lse_blk = jnp.zeros((_BQ, LANES), F32)
        for hh in range(nh):
            sl = slice(hh * HEAD_DIM, (hh + 1) * HEAD_DIM)
            q = q_ref[0, 0, sb * _BQ:(sb + 1) * _BQ, sl]
            kx = kext[sb * _BQ:sb * _BQ + nkeys, sl]
            vx = vext[sb * _BQ:sb * _BQ + nkeys, sl]
            s = lax.dot_general(q, kx, (((1,), (1,)), ((), ())), preferred_element_type=F32) * scale
            s = jnp.where(valid, s, -1e30)
            m = jnp.max(s, axis=-1, keepdims=True)
            e = jnp.where(valid, jnp.exp(s - m), 0.0)
            l = jnp.sum(e, axis=-1, keepdims=True)
            p = (e / l).astype(BF16)
            o = jnp.dot(p, vx, preferred_element_type=F32)
            o_ref[0, 0, sb * _BQ:(sb + 1) * _BQ, sl] = o.astype(o_ref.dtype)
            lse_blk = jnp.where(lane == hh, m + jnp.log(l), lse_blk)
        l_ref[0, 0, sb * _BQ:(sb + 1) * _BQ, :] = lse_blk


def _dilated_group(grp_qkv):
    bsz, dil, ln, _ = grp_qkv.shape
    gw = B_HEADS_PER_GROUP * HEAD_DIM
    qb = min(512, ln)
    hb = qb // _BW
    nhalo = ln // _BW
    lo = lambda i: jnp.maximum(i * hb - 1, 0)
    hi = lambda i: jnp.minimum((i + 1) * hb, nhalo - 1)

    def cur(col):
        return pl.BlockSpec((1, 1, qb, gw), lambda b, r, i: (b, r, i, col))

    def halo(col, f):
        return pl.BlockSpec((1, 1, _BW, gw), lambda b, r, i: (b, r, f(i), col))

    return pl.pallas_call(
        functools.partial(_dil_kernel, qb=qb, nh=B_HEADS_PER_GROUP, seq_len=ln),
        out_shape=(jax.ShapeDtypeStruct((bsz, dil, ln, gw), BF16),
                   jax.ShapeDtypeStruct((bsz, dil, ln, LANES), F32)),
        grid=(bsz, dil, ln // qb),
        in_specs=[cur(0), cur(1), halo(1, lo), halo(1, hi), cur(2), halo(2, lo), halo(2, hi)],
        out_specs=(pl.BlockSpec((1, 1, qb, gw), lambda b, r, i: (b, r, i, 0)),
                   pl.BlockSpec((1, 1, qb, LANES), lambda b, r, i: (b, r, i, 0))),
        scratch_shapes=[pltpu.VMEM((qb + 2 * _BW, gw), BF16), pltpu.VMEM((qb + 2 * _BW, gw), BF16)],
        compiler_params=_cparams(("parallel", "parallel", "parallel")),
        name=f"dilated_attn_d{dil}",
    )(*([grp_qkv] * 7))


def _merge_kernel(*refs, tr, dils, nh):
    ng = len(dils)
    o_refs = refs[:ng]
    l_refs = refs[ng:2 * ng]
    out_ref = refs[2 * ng]
    outs, lses = [], []
    for g, dil in enumerate(dils):
        per = tr // dil
        if dil == 1:
            outs.append(o_refs[g][0, 0].astype(F32))
            lses.append(l_refs[g][0, 0])
            continue
        pt = _deinterleave_matrix(tr, dil, transpose=True)
        y = jnp.concatenate([o_refs[g][0, r] for r in range(dil)], axis=0)
        outs.append(jnp.dot(pt, y, preferred_element_type=F32))
        ls = jnp.concatenate([l_refs[g][0, r] for r in range(dil)], axis=0)
        h1 = ls.astype(BF16)
        r1 = ls - h1.astype(F32)
        h2 = r1.astype(BF16)
        h3 = (r1 - h2.astype(F32)).astype(BF16)
        lses.append(jnp.dot(pt, h1, preferred_element_type=F32)
                    + (jnp.dot(pt, h2, preferred_element_type=F32) + jnp.dot(pt, h3, preferred_element_type=F32)))
    mx = lses[0]
    for g in range(1, ng):
        mx = jnp.maximum(mx, lses[g])
    es = [jnp.exp(l - mx) for l in lses]
    den = es[0]
    for g in range(1, ng):
        den = den + es[g]
    ws = [e / den for e in es]
    lane = lax.broadcasted_iota(jnp.int32, (tr, LANES), 1)
    for hh in range(nh):
        sl = slice(hh * HEAD_DIM, (hh + 1) * HEAD_DIM)
        acc = None
        for g in range(ng):
            wcol = jnp.sum(jnp.where(lane == hh, ws[g], 0.0), axis=1, keepdims=True)
            term = outs[g][:, sl] * wcol
            acc = term if acc is None else acc + term
        out_ref[:, sl] = acc.astype(out_ref.dtype)


def _merge_groups(os_, ls_, *, bsz, seq, tr=256):
    gw = B_HEADS_PER_GROUP * HEAD_DIM
    dils = tuple(o.shape[1] for o in os_)
    nsb = seq // tr
    in_specs = ([pl.BlockSpec((1, d, tr // d, gw), lambda b, i: (b, 0, i, 0)) for d in dils]
                + [pl.BlockSpec((1, d, tr // d, LANES), lambda b, i: (b, 0, i, 0)) for d in dils])
    return pl.pallas_call(
        functools.partial(_merge_kernel, tr=tr, dils=dils, nh=B_HEADS_PER_GROUP),
        out_shape=jax.ShapeDtypeStruct((bsz * seq, gw), BF16),
        grid=(bsz, nsb),
        in_specs=in_specs,
        out_specs=pl.BlockSpec((tr, gw), lambda b, i: (b * nsb + i, 0)),
        compiler_params=_cparams(("parallel", "parallel")),
        name="dilated_merge",
    )(*os_, *ls_)


def _qknorm_rope_kernel(x_ref, wq_ref, wk_ref, cos_ref, sin_ref, o_ref, *, cw, n_q_blocks):
    j = pl.program_id(1)
    is_q = j < n_q_blocks
    w = jnp.where(is_q, wq_ref[...], wk_ref[...])
    scale = jnp.where(is_q, HEAD_DIM ** -0.5 * LOG2E, 1.0).astype(F32)
    cos = cos_ref[...]
    sin = sin_ref[...]
    swap = _pair_swap_matrix(HEAD_DIM // 4)
    for hh in range(cw // HEAD_DIM):
        sl = slice(hh * HEAD_DIM, (hh + 1) * HEAD_DIM)
        x = x_ref[:, sl].astype(F32)
        ms = jnp.mean(x * x, axis=-1, keepdims=True)
        y = x * lax.rsqrt(ms + NORM_EPS) * w
        y = y * cos + _rotate_pairs(y, swap) * sin
        o_ref[:, sl] = (y * scale).astype(o_ref.dtype)


def _qknorm_rope(qkv, q_norm, k_norm, cos_t, sin_t, *, q_width, qk_width, seq, tr=1024, cw=512):
    t = qkv.shape[0]
    tr = min(tr, seq)
    nsb = seq // tr
    return pl.pallas_call(
        functools.partial(_qknorm_rope_kernel, cw=cw, n_q_blocks=q_width // cw),
        out_shape=jax.ShapeDtypeStruct((t, qk_width), BF16),
        grid=(t // tr, qk_width // cw),
        in_specs=[pl.BlockSpec((tr, cw), lambda i, j: (i, j)),
                  pl.BlockSpec((1, HEAD_DIM), lambda i, j: (0, 0)),
                  pl.BlockSpec((1, HEAD_DIM), lambda i, j: (0, 0)),
                  pl.BlockSpec((tr, HEAD_DIM), lambda i, j: (i % nsb, 0)),
                  pl.BlockSpec((tr, HEAD_DIM), lambda i, j: (i % nsb, 0))],
        out_specs=pl.BlockSpec((tr, cw), lambda i, j: (i, j)),
        compiler_params=_cparams(("parallel", "parallel")),
        name="qk_norm_axial_rope",
    )(qkv, q_norm.reshape(1, HEAD_DIM).astype(F32), k_norm.reshape(1, HEAD_DIM).astype(F32), cos_t, sin_t)


def _flash_kernel(q_ref, k_ref, v_ref, o_ref, qs_ref, v1_ref, m_ref, acc_ref, *, tq, tk, rep, seq):
    @pl.when(pl.program_id(2) == 0)
    def _():
        v1_ref[:, :HEAD_DIM] = v_ref[0]
        v1_ref[:, HEAD_DIM:] = jnp.ones((seq, HEAD_DIM), BF16)

    for r in range(rep):
        qs_ref[r * tq:(r + 1) * tq, :] = q_ref[0, :, r * HEAD_DIM:(r + 1) * HEAD_DIM]
    m_ref[...] = jnp.full(m_ref.shape, -1e30, F32)
    acc_ref[...] = jnp.zeros(acc_ref.shape, F32)

    def body(t, carry):
        off = pl.multiple_of(t * tk, tk)
        kt = k_ref[0, pl.ds(off, tk), :]
        vt = v1_ref[pl.ds(off, tk), :]
        s = [lax.dot_general(qs_ref[r * tq:(r + 1) * tq, :], kt, (((1,), (1,)), ((), ())),
                             preferred_element_type=F32) for r in range(rep)]
        for r in range(rep):
            rows = slice(r * tq, (r + 1) * tq)
            m_old = m_ref[rows, :]
            m_new = jnp.maximum(m_old, jnp.max(s[r], axis=-1, keepdims=True))
            p = jnp.exp2(s[r] - m_new).astype(BF16)
            pv = jnp.dot(p, vt, preferred_element_type=F32)
            acc_ref[rows, :] = acc_ref[rows, :] * jnp.exp2(m_old - m_new) + pv
            m_ref[rows, :] = m_new
        return carry

    lax.fori_loop(0, seq // tk, body, 0)
    for r in range(rep):
        a = acc_ref[r * tq:(r + 1) * tq, :]
        o_ref[0, :, r * HEAD_DIM:(r + 1) * HEAD_DIM] = (a[:, :HEAD_DIM] / a[:, HEAD_DIM:]).astype(o_ref.dtype)


def _flash_attention(qk, qkv, *, bsz, seq, tq=512, tk=4096):
    qw = C_Q_HEADS * HEAD_DIM
    tq, tk = min(tq, seq), min(tk, seq)
    gq = C_REP * HEAD_DIM
    qk3 = qk.reshape(bsz, seq, qk.shape[-1])
    qkv3 = qkv.reshape(bsz, seq, qkv.shape[-1])
    k_blk0 = qw // HEAD_DIM
    v_blk0 = (qw + C_KV_HEADS * HEAD_DIM) // HEAD_DIM
    out = pl.pallas_call(
        functools.partial(_flash_kernel, tq=tq, tk=tk, rep=C_REP, seq=seq),
        out_shape=jax.ShapeDtypeStruct((bsz, seq, qw), BF16),
        grid=(bsz, C_KV_HEADS, seq // tq),
        in_specs=[pl.BlockSpec((1, tq, gq), lambda b, g, i: (b, i, g)),
                  pl.BlockSpec((1, seq, HEAD_DIM), lambda b, g, i: (b, 0, k_blk0 + g)),
                  pl.BlockSpec((1, seq, HEAD_DIM), lambda b, g, i: (b, 0, v_blk0 + g))],
        out_specs=pl.BlockSpec((1, tq, gq), lambda b, g, i: (b, i, g)),
        scratch_shapes=[pltpu.VMEM((C_REP * tq, HEAD_DIM), BF16),
                        pltpu.VMEM((seq, 2 * HEAD_DIM), BF16),
                        pltpu.VMEM((C_REP * tq, 1), F32),
                        pltpu.VMEM((C_REP * tq, 2 * HEAD_DIM), F32)],
        compiler_params=_cparams(("parallel", "parallel", "arbitrary")),
        name="gqa_flash_attention",
    )(qk3, qk3, qkv3)
    return out.reshape(bsz * seq, qw)


def _rope_cos_sin(pos, dim, theta):
    inv = 1.0 / (theta ** (jnp.arange(0, dim, 2, dtype=F32) / dim))
    ang = pos.astype(F32)[:, None] * inv[None, :]
    return jnp.cos(ang), jnp.sin(ang)


def _tables(seq):
    tok = jnp.arange(seq)
    cos_b, sin_b = _rope_cos_sin(tok, B_ROT_DIM, B_ROPE_THETA)
    ones = jnp.ones((seq, HEAD_DIM - B_ROT_DIM), F32)
    cb = jnp.concatenate([cos_b, cos_b, ones], axis=-1)
    sb = jnp.concatenate([-sin_b, sin_b, jnp.zeros_like(ones)], axis=-1)
    cos_r, sin_r = _rope_cos_sin(tok // GRID_W, HEAD_DIM // 2, C_ROPE_THETA)
    cos_c, sin_c = _rope_cos_sin(tok % GRID_W, HEAD_DIM // 2, C_ROPE_THETA)
    cc = jnp.concatenate([cos_r, cos_r, cos_c, cos_c], axis=-1)
    sc = jnp.concatenate([-sin_r, sin_r, -sin_c, sin_c], axis=-1)
    return cb, sb, cc, sc


def _ffn(x2d, norm_w, w_gate, w_up, w_down, layer):
    h = _rmsnorm(x2d, norm_w, BF16)
    act = _gateup(h, w_gate, w_up, layer, tm=2048, tn=256)
    dff = act.shape[1]
    nsplit = 2 if (dff // 2) % LANES == 0 else 1
    for kb in range(nsplit):
        x2d = _wmatmul([(act, kb, dff // nsplit)], w_down, w_lead=layer, w_row_blk=kb, n=x2d.shape[1],
                       tm=1024, tn=512, out_dtype=F32, residual=x2d)
    return x2d


def _mixer_ab(x2d, norm_w, w_in, conv_w, a_log, dt_bias, out_norm, w_out, j, cb, sb, *, bsz, seq):
    t, dm = x2d.shape
    n_heads = a_log.shape[-1]
    a_qk = n_heads * HEAD_DIM
    a_v = n_heads * HEAD_DIM
    a_qkv = 2 * a_qk + a_v
    ngate = 2 * n_heads
    b_w = (w_in.shape[-1] - a_qkv - a_v - 2 * ngate) // 3
    h = _rmsnorm(x2d, norm_w, BF16)

    w_in_t = jnp.swapaxes(w_in, 1, 2)
    g0 = a_qkv + a_v
    w_beta = w_in_t[j, g0:g0 + ngate].reshape(2, n_heads, dm)
    w_alpha = w_in_t[j, g0 + ngate:g0 + 2 * ngate].reshape(2, n_heads, dm)
    w_g = jnp.concatenate([w_beta, w_alpha, jnp.zeros((2, LANES - 2 * n_heads, dm), F32)], axis=1)
    w_g = w_g.reshape(1, 2 * LANES, dm)
    w_b = _shift_cast_rows(w_in_t, j, g0 + 2 * ngate, 3 * b_w)

    proj_a = _wmatmul_t(h, w_in_t, w_lead=j, n=a_qkv + a_v, tm=2048, tn=512, out_dtype=BF16)
    gates = _wmatmul_t(h, w_g, w_lead=0, n=2 * LANES, tm=2048, tn=2 * LANES, out_dtype=F32)
    proj_b = _wmatmul_t(h, w_b, w_lead=0, n=3 * b_w, tm=2048, tn=512, out_dtype=BF16)

    qkv = _conv_silu_l2(proj_a.reshape(bsz, seq, -1), conv_w, qk_width=2 * a_qk, q_width=a_qk, n_ch=a_qkv)
    pad = jnp.zeros((2, LANES - 2 * n_heads), F32)
    alog_row = jnp.concatenate([jnp.zeros((2, n_heads), F32), a_log.astype(F32), pad], axis=-1).reshape(2, 1, LANES)
    dtb_row = jnp.concatenate([jnp.zeros((2, n_heads), F32), dt_bias.astype(F32), pad], axis=-1).reshape(2, 1, LANES)
    o2 = _delta_scan(qkv, gates.reshape(bsz, seq, 2 * LANES), alog_row, dtb_row, n_heads=n_heads)
    o_a = _delta_out(o2.reshape(2, t, a_v), proj_a, out_norm, z_col_block=a_qkv // 512)

    n_groups = len(B_PATTERNS)
    os_, ls_ = [], []
    for gi, (_, dil) in enumerate(B_PATTERNS):
        grp = _rope_deint(proj_b, cb, sb, gi, dil, bsz=bsz, seq=seq, n_groups=n_groups)
        o_g, l_g = _dilated_group(grp)
        os_.append(o_g)
        ls_.append(l_g)
    o_b = _merge_groups(os_, ls_, bsz=bsz, seq=seq)

    return _wmatmul([(o_a, 0, a_v), (o_b, 0, o_b.shape[1])], w_out, w_lead=j, n=dm, tm=1024, tn=512,
                    out_dtype=F32, residual=x2d)


def _mixer_c(x2d, norm_w, w_qkv, q_norm, k_norm, w_out, j, cc, sc, *, bsz, seq):
    dm = x2d.shape[1]
    h = _rmsnorm(x2d, norm_w, BF16)
    qkv = _wmatmul([(h, 0, dm)], w_qkv, w_lead=j, n=w_qkv.shape[-1], tm=2048, tn=512, out_dtype=BF16)
    qw = C_Q_HEADS * HEAD_DIM
    kw = C_KV_HEADS * HEAD_DIM
    qk = _qknorm_rope(qkv, q_norm, k_norm, cc, sc, q_width=qw, qk_width=qw + kw, seq=seq)
    o = _flash_attention(qk, qkv, bsz=bsz, seq=seq)
    return _wmatmul([(o, 0, qw)], w_out, w_lead=j, n=dm, tm=1024, tn=512, out_dtype=F32, residual=x2d)


def kernel(x, norm_mix, norm_ffn, norm_final, ab_w_in, ab_conv_w, ab_a_log, ab_dt_bias, ab_out_norm, ab_w_out,
           c_w_qkv, c_q_norm, c_k_norm, c_w_out, ffn_w_gate, ffn_w_up, ffn_w_down):
    bsz, seq, dm = x.shape
    depth = norm_mix.shape[0]
    cb, sb, cc, sc = _tables(seq)
    x2d = x.reshape(bsz * seq, dm)
    for layer in range(depth):
        j = layer // 2
        if layer % 2 == 0:
            x2d = _mixer_ab(x2d, norm_mix[layer], ab_w_in, ab_conv_w[j], ab_a_log[j], ab_dt_bias[j],
                            ab_out_norm[j], ab_w_out, j, cb, sb, bsz=bsz, seq=seq)
        else:
            x2d = _mixer_c(x2d, norm_mix[layer], c_w_qkv, c_q_norm[j], c_k_norm[j], c_w_out, j,
                           cc, sc, bsz=bsz, seq=seq)
        x2d = _ffn(x2d, norm_ffn[layer], ffn_w_gate, ffn_w_up, ffn_w_down, layer)
    out = _rmsnorm(x2d, norm_final, x.dtype)
    return out.reshape(bsz, seq, dm)
```

```python
import functools

import jax
import jax.numpy as jnp
from jax import lax
from jax.experimental import pallas as pl
from jax.experimental.pallas import tpu as pltpu

HEAD_DIM = 128
NORM_EPS = 1e-6
L2_EPS = 1e-6
GRID_W = 64
A_CONV_W = 5
A_CHUNK = 64
B_PATTERNS = ((128, 1), (512, 4), (2048, 16))
B_HEADS_PER_GROUP = 8
B_ROT_DIM = HEAD_DIM // 4
B_ROPE_THETA = 500000.0
C_Q_HEADS = 32
C_KV_HEADS = 8
C_REP = C_Q_HEADS // C_KV_HEADS
C_ROPE_THETA = 10000.0
LOG2E = 1.4426950408889634

LANES = 128
VMEM_LIMIT = 56 * 1024 * 1024

BF16 = jnp.bfloat16
F32 = jnp.float32
HI = lax.Precision.HIGHEST


def _cparams(sem, vmem=VMEM_LIMIT):
    return pltpu.CompilerParams(dimension_semantics=sem, vmem_limit_bytes=vmem)


def _rmsnorm_kernel(x_ref, w_ref, o_ref):
    x = x_ref[...].astype(F32)
    ms = jnp.mean(x * x, axis=-1, keepdims=True)
    o_ref[...] = (x * lax.rsqrt(ms + NORM_EPS) * w_ref[...]).astype(o_ref.dtype)


def _rmsnorm(x2d, w, out_dtype, tr=512):
    m, d = x2d.shape
    return pl.pallas_call(
        _rmsnorm_kernel,
        out_shape=jax.ShapeDtypeStruct((m, d), out_dtype),
        grid=(m // tr,),
        in_specs=[pl.BlockSpec((tr, d), lambda i: (i, 0)),
                  pl.BlockSpec((1, d), lambda i: (0, 0))],
        out_specs=pl.BlockSpec((tr, d), lambda i: (i, 0)),
        compiler_params=_cparams(("parallel",)),
        name="rmsnorm",
    )(x2d, w.reshape(1, d).astype(F32))


def _stage_weight_chunk(chunk_ref, w16, ck):
    jp = pl.program_id(0)
    i = pl.program_id(1)
    w16[jp % 2, pl.ds(pl.multiple_of(i * ck, 16), ck), :] = chunk_ref[...].astype(BF16)


def _wmm_kernel(*refs, ksizes, has_res, ck):
    na = len(ksizes)
    a_refs = refs[:na]
    chunk_ref = refs[na]
    r_ref = refs[na + 1] if has_res else None
    o_ref = refs[na + 1 + int(has_res)]
    w16 = refs[na + 2 + int(has_res)]
    jp = pl.program_id(0)
    _stage_weight_chunk(chunk_ref, w16, ck)

    @pl.when(jp > 0)
    def _():
        slot = (jp + 1) % 2
        off = 0
        acc = None
        for a_ref, ks in zip(a_refs, ksizes):
            part = jnp.dot(a_ref[...], w16[slot, off:off + ks, :], preferred_element_type=F32)
            acc = part if acc is None else acc + part
            off += ks
        if has_res:
            acc = acc + r_ref[...]
        o_ref[...] = acc.astype(o_ref.dtype)


def _wmatmul(a_ops, w, *, w_lead, w_row_blk=0, w_col_blk0=0, n, tm, tn, out_dtype, residual=None):
    m = a_ops[0][0].shape[0]
    ksizes = tuple(k for _, _, k in a_ops)
    ktot = sum(ksizes)
    tm, tn = min(tm, m), min(tn, n)
    ni, nj = m // tm, n // tn
    ck = ktot // ni
    assert m % tm == 0 and n % tn == 0 and ktot % ni == 0 and ck % 16 == 0
    row_of = lambda jp, i: jnp.where(jp == 0, 0, i)
    col_of = lambda jp: jnp.maximum(jp - 1, 0)
    in_specs = [pl.BlockSpec((tm, ks), functools.partial(lambda jp, i, blk: (row_of(jp, i), blk), blk=blk))
                for _, blk, ks in a_ops]
    in_specs.append(pl.BlockSpec((None, ck, tn),
                                 lambda jp, i: (w_lead, w_row_blk * ni + i, w_col_blk0 + jnp.minimum(jp, nj - 1))))
    args = [a for a, _, _ in a_ops] + [w]
    if residual is not None:
        in_specs.append(pl.BlockSpec((tm, tn), lambda jp, i: (row_of(jp, i), col_of(jp))))
        args.append(residual)
    return pl.pallas_call(
        functools.partial(_wmm_kernel, ksizes=ksizes, has_res=residual is not None, ck=ck),
        out_shape=jax.ShapeDtypeStruct((m, n), out_dtype),
        grid=(nj + 1, ni),
        in_specs=in_specs,
        out_specs=pl.BlockSpec((tm, tn), lambda jp, i: (row_of(jp, i), col_of(jp))),
        scratch_shapes=[pltpu.VMEM((2, ktot, tn), BF16)],
        compiler_params=_cparams(("arbitrary", "arbitrary")),
        name="proj_matmul",
    )(*args)


def _wmm_t_kernel(a_ref, chunk_ref, o_ref, w16, *, ck):
    jp = pl.program_id(0)
    i = pl.program_id(1)
    w16[jp % 2, :, pl.ds(pl.multiple_of(i * ck, LANES), ck)] = chunk_ref[...].astype(BF16)

    @pl.when(jp > 0)
    def _():
        acc = lax.dot_general(a_ref[...], w16[(jp + 1) % 2], (((1,), (1,)), ((), ())),
                              preferred_element_type=F32)
        o_ref[...] = acc.astype(o_ref.dtype)


def _wmatmul_t(a, wt, *, w_lead, w_row_blk0=0, n, tm, tn, out_dtype):
    m, kdim = a.shape
    tm, tn = min(tm, m), min(tn, n)
    ni, nj = m // tm, n // tn
    ck = kdim // ni
    assert m % tm == 0 and n % tn == 0 and kdim % ni == 0 and ck % LANES == 0
    row_of = lambda jp, i: jnp.where(jp == 0, 0, i)
    col_of = lambda jp: jnp.maximum(jp - 1, 0)
    return pl.pallas_call(
        functools.partial(_wmm_t_kernel, ck=ck),
        out_shape=jax.ShapeDtypeStruct((m, n), out_dtype),
        grid=(nj + 1, ni),
        in_specs=[pl.BlockSpec((tm, kdim), lambda jp, i: (row_of(jp, i), 0)),
                  pl.BlockSpec((None, tn, ck), lambda jp, i: (w_lead, w_row_blk0 + jnp.minimum(jp, nj - 1), i))],
        out_specs=pl.BlockSpec((tm, tn), lambda jp, i: (row_of(jp, i), col_of(jp))),
        scratch_shapes=[pltpu.VMEM((2, tn, kdim), BF16)],
        compiler_params=_cparams(("arbitrary", "arbitrary")),
        name="proj_matmul_wt",
    )(a, wt)


def _gateup_kernel(h_ref, cg_ref, cu_ref, o_ref, wg16, wu16, *, ck):
    jp = pl.program_id(0)
    _stage_weight_chunk(cg_ref, wg16, ck)
    _stage_weight_chunk(cu_ref, wu16, ck)

    @pl.when(jp > 0)
    def _():
        slot = (jp + 1) % 2
        h = h_ref[...]
        g = jnp.dot(h, wg16[slot], preferred_element_type=F32)
        u = jnp.dot(h, wu16[slot], preferred_element_type=F32)
        o_ref[...] = (g * jax.nn.sigmoid(g) * u).astype(o_ref.dtype)


def _gateup(h, w_gate, w_up, layer, *, tm, tn):
    m, kdim = h.shape
    n = w_gate.shape[-1]
    tm, tn = min(tm, m), min(tn, n)
    ni, nj = m // tm, n // tn
    ck = kdim // ni
    assert m % tm == 0 and n % tn == 0 and kdim % ni == 0 and ck % 16 == 0
    wspec = pl.BlockSpec((None, ck, tn), lambda jp, i: (layer, i, jnp.minimum(jp, nj - 1)))
    return pl.pallas_call(
        functools.partial(_gateup_kernel, ck=ck),
        out_shape=jax.ShapeDtypeStruct((m, n), BF16),
        grid=(nj + 1, ni),
        in_specs=[pl.BlockSpec((tm, kdim), lambda jp, i: (jnp.where(jp == 0, 0, i), 0)), wspec, wspec],
        out_specs=pl.BlockSpec((tm, tn), lambda jp, i: (jnp.where(jp == 0, 0, i), jnp.maximum(jp - 1, 0))),
        scratch_shapes=[pltpu.VMEM((2, kdim, tn), BF16), pltpu.VMEM((2, kdim, tn), BF16)],
        compiler_params=_cparams(("arbitrary", "arbitrary")),
        name="ffn_gate_up",
    )(h, w_gate, w_up)


def _shift_cast_kernel(main_ref, next_ref, o_ref, *, shift):
    tr = main_ref.shape[0]
    o_ref[0:tr - shift, :] = main_ref[shift:, :].astype(o_ref.dtype)
    o_ref[tr - shift:, :] = next_ref[...].astype(o_ref.dtype)


def _shift_cast_rows(wt, lead, row0, nrows, *, tr=512, tc=1024):
    _, _, kdim = wt.shape
    base = (row0 // tr) * tr
    shift = row0 - base
    tc = min(tc, kdim)
    assert shift > 0 and shift % 16 == 0 and tr % shift == 0 and nrows % tr == 0 and kdim % tc == 0
    per = tr // shift
    return pl.pallas_call(
        functools.partial(_shift_cast_kernel, shift=shift),
        out_shape=jax.ShapeDtypeStruct((1, nrows, kdim), BF16),
        grid=(nrows // tr, kdim // tc),
        in_specs=[pl.BlockSpec((None, tr, tc), lambda i, j: (lead, base // tr + i, j)),
                  pl.BlockSpec((None, shift, tc), lambda i, j: (lead, (base // tr + i + 1) * per, j))],
        out_specs=pl.BlockSpec((None, tr, tc), lambda i, j: (0, i, j)),
        compiler_params=_cparams(("parallel", "parallel")),
        name="weight_shift_cast",
    )(wt, wt)


_CONV_HALO = 16


def _conv_kernel(prev_ref, cur_ref, next_ref, w_ref, o_ref, ext_ref, *, ts, cw, n_qk_blocks, n_q_blocks):
    i = pl.program_id(1)
    nblk = pl.num_programs(1)
    j = pl.program_id(2)
    pad = A_CONV_W // 2
    prev = prev_ref[0].astype(F32)
    nxt = next_ref[0].astype(F32)
    ext_ref[0:_CONV_HALO, :] = jnp.where(i > 0, prev, 0.0)
    ext_ref[_CONV_HALO:_CONV_HALO + ts, :] = cur_ref[0].astype(F32)
    ext_ref[_CONV_HALO + ts:, :] = jnp.where(i < nblk - 1, nxt, 0.0)
    w = w_ref[...]
    ext = ext_ref[...]
    nrow = ts + 2 * _CONV_HALO
    y = ext[_CONV_HALO:_CONV_HALO + ts, :] * w[pad:pad + 1, :]
    for t in range(A_CONV_W):
        if t != pad:
            rolled = pltpu.roll(ext, (pad - t) % nrow, 0)
            y = y + rolled[_CONV_HALO:_CONV_HALO + ts, :] * w[t:t + 1, :]
    y = y * jax.nn.sigmoid(y)
    is_qk = j < n_qk_blocks
    scale = jnp.where(j < n_q_blocks, HEAD_DIM ** -0.5, 1.0).astype(F32)
    for hh in range(cw // HEAD_DIM):
        yh = y[:, hh * HEAD_DIM:(hh + 1) * HEAD_DIM]
        ss = jnp.sum(yh * yh, axis=-1, keepdims=True)
        yn = yh * (lax.rsqrt(ss + L2_EPS) * scale)
        o_ref[0, :, hh * HEAD_DIM:(hh + 1) * HEAD_DIM] = jnp.where(is_qk, yn, yh).astype(o_ref.dtype)


def _conv_silu_l2(proj_a, conv_w, *, qk_width, q_width, n_ch, ts=1024, cw=512):
    bsz, seq, _ = proj_a.shape
    ts = min(ts, seq)
    hb = ts // _CONV_HALO
    nhalo = seq // _CONV_HALO
    w8 = jnp.zeros((8, n_ch), F32).at[:A_CONV_W].set(conv_w.astype(F32))
    kern = functools.partial(_conv_kernel, ts=ts, cw=cw, n_qk_blocks=qk_width // cw, n_q_blocks=q_width // cw)
    return pl.pallas_call(
        kern,
        out_shape=jax.ShapeDtypeStruct((bsz, seq, n_ch), BF16),
        grid=(bsz, seq // ts, n_ch // cw),
        in_specs=[
            pl.BlockSpec((1, _CONV_HALO, cw), lambda b, i, j: (b, jnp.maximum(i * hb - 1, 0), j)),
            pl.BlockSpec((1, ts, cw), lambda b, i, j: (b, i, j)),
            pl.BlockSpec((1, _CONV_HALO, cw), lambda b, i, j: (b, jnp.minimum((i + 1) * hb, nhalo - 1), j)),
            pl.BlockSpec((8, cw), lambda b, i, j: (0, j)),
        ],
        out_specs=pl.BlockSpec((1, ts, cw), lambda b, i, j: (b, i, j)),
        scratch_shapes=[pltpu.VMEM((ts + 2 * _CONV_HALO, cw), F32)],
        compiler_params=_cparams(("parallel", "parallel", "parallel")),
        name="conv_silu_l2norm",
    )(proj_a, proj_a, proj_a, w8)


_INV_BASE = 8


def _split_bf16(x):
    hi = x.astype(BF16)
    lo = (x - hi.astype(F32)).astype(BF16)
    return hi, lo


def _dot3(a, b):
    a_hi, a_lo = a
    b_hi, b_lo = b
    return (jnp.dot(a_hi, b_hi, preferred_element_type=F32)
            + (jnp.dot(a_hi, b_lo, preferred_element_type=F32) + jnp.dot(a_lo, b_hi, preferred_element_type=F32)))


def _delta_kernel(q_ref, k_ref, v_ref, gt_ref, alog_ref, dtb_ref, o_ref, state_ref, *, hg, n_heads):
    c = A_CHUNK
    d = pl.program_id(0) % 2
    grp = pl.program_id(1)
    n = pl.program_id(2)

    @pl.when(n == 0)
    def _():
        state_ref[...] = jnp.zeros_like(state_ref)

    row = lax.broadcasted_iota(jnp.int32, (c, c), 0)
    col = lax.broadcasted_iota(jnp.int32, (c, c), 1)
    ahead = jnp.where(d == 0, row - col, col - row)
    incl = ahead >= 0
    strict = ahead > 0
    tri = jnp.where(incl, 1.0, 0.0).astype(F32)
    eye = jnp.where(row == col, 1.0, 0.0).astype(F32)

    graw = gt_ref[0]
    beta_all = jax.nn.sigmoid(graw)
    z = graw + dtb_ref[0]
    softplus = jnp.maximum(z, 0.0) + jnp.log(1.0 + jnp.exp(-jnp.abs(z)))
    g_all = -jnp.exp(alog_ref[0]) * softplus
    gc_all = jnp.dot(tri, g_all, precision=HI, preferred_element_type=F32)
    gct_all = lax.dot_general(g_all, tri, (((0,), (1,)), ((), ())), precision=HI,
                              preferred_element_type=F32)
    gtot_all = jnp.sum(g_all, axis=0, keepdims=True)
    lane = lax.broadcasted_iota(jnp.int32, (c, LANES), 1)
    subl = lax.broadcasted_iota(jnp.int32, (LANES, c), 0)
    contract_last = (((1,), (1,)), ((), ()))
    contract_first = (((0,), (0,)), ((), ()))

    heads = range(hg)
    sls = [slice(hh * HEAD_DIM, (hh + 1) * HEAD_DIM) for hh in heads]
    q16 = [q_ref[0, :, sl] for sl in sls]
    k16 = [k_ref[0, :, sl] for sl in sls]
    gcol, beta, gtot, decay = [], [], [], []
    for hh in heads:
        a_lane = n_heads + grp * hg + hh
        sel_a = lane == a_lane
        sel_b = lane == (grp * hg + hh)
        gcol.append(jnp.sum(jnp.where(sel_a, gc_all, 0.0), axis=1, keepdims=True))
        beta.append(jnp.sum(jnp.where(sel_b, beta_all, 0.0), axis=1, keepdims=True))
        gtot.append(jnp.sum(jnp.where(sel_a[0:1], gtot_all, 0.0), axis=1, keepdims=True))
        grow = jnp.sum(jnp.where(subl == a_lane, gct_all, 0.0), axis=0, keepdims=True)
        diff = gcol[hh] - grow
        decay.append(jnp.where(incl, jnp.exp(jnp.where(incl, diff, 0.0)), 0.0))
    kb = [k16[hh].astype(F32) * beta[hh] for hh in heads]
    kq = [lax.dot_general(jnp.concatenate([kb[hh].astype(BF16), q16[hh]], axis=0), k16[hh], contract_last,
                          preferred_element_type=F32) for hh in heads]
    attn = [(kq[hh][c:] * decay[hh]).astype(BF16) for hh in heads]
    a_mat = [jnp.where(strict, kq[hh][:c] * decay[hh], 0.0) for hh in heads]
    same_base = (row // _INV_BASE) == (col // _INV_BASE)
    x = [jnp.where(same_base, -a_mat[hh], 0.0) for hh in heads]
    t_mat = [eye + x[hh] for hh in heads]
    p = 2
    while p < _INV_BASE:
        xs = [_split_bf16(x[hh]) for hh in heads]
        x = [_dot3(xs[hh], xs[hh]) for hh in heads]
        t_mat = [t_mat[hh] + _dot3(_split_bf16(t_mat[hh]), _split_bf16(x[hh])) for hh in heads]
        p *= 2
    s = _INV_BASE
    while s < c:
        pair = ((row // (2 * s)) == (col // (2 * s))) & ((row // s) != (col // s))
        ts = [_split_bf16(t_mat[hh]) for hh in heads]
        tl = [_dot3(ts[hh], _split_bf16(jnp.where(pair, a_mat[hh], 0.0))) for hh in heads]
        t_mat = [t_mat[hh] - _dot3(_split_bf16(tl[hh]), ts[hh]) for hh in heads]
        s *= 2
    eg = [jnp.exp(gcol[hh]) for hh in heads]
    rhs = [jnp.concatenate([v_ref[0, :, sls[hh]].astype(F32) * beta[hh], kb[hh] * eg[hh]], axis=1).astype(BF16)
           for hh in heads]
    uw = [jnp.dot(t_mat[hh].astype(BF16), rhs[hh], preferred_element_type=F32) for hh in heads]
    state = [state_ref[hh] for hh in heads]
    ws = [jnp.dot(jnp.concatenate([uw[hh][:, HEAD_DIM:], q16[hh].astype(F32) * eg[hh]], axis=0).astype(BF16),
                  state[hh].astype(BF16), preferred_element_type=F32) for hh in heads]
    vn16 = [(uw[hh][:, :HEAD_DIM] - ws[hh][:c]).astype(BF16) for hh in heads]
    for hh in heads:
        o = ws[hh][c:] + jnp.dot(attn[hh], vn16[hh], preferred_element_type=F32)
        o_ref[0, 0, :, sls[hh]] = o.astype(o_ref.dtype)
    for hh in heads:
        kd = (k16[hh].astype(F32) * jnp.exp(gtot[hh] - gcol[hh])).astype(BF16)
        state_ref[hh] = state[hh] * jnp.exp(gtot[hh]) + lax.dot_general(kd, vn16[hh], contract_first,
                                                                       preferred_element_type=F32)


def _delta_scan(qkv, gates, alog_row, dtb_row, *, n_heads, hg=16):
    bsz, seq, _ = qkv.shape
    c = A_CHUNK
    nchunk = seq // c
    ngrp = n_heads // hg
    wblk = hg * HEAD_DIM

    def cidx(bd, n):
        return jnp.where(bd % 2 == 0, n, nchunk - 1 - n)

    kern = functools.partial(_delta_kernel, hg=hg, n_heads=n_heads)
    return pl.pallas_call(
        kern,
        out_shape=jax.ShapeDtypeStruct((2, bsz, seq, n_heads * HEAD_DIM), F32),
        grid=(bsz * 2, ngrp, nchunk),
        in_specs=[
            pl.BlockSpec((1, c, wblk), lambda bd, g, n: (bd // 2, cidx(bd, n), g)),
            pl.BlockSpec((1, c, wblk), lambda bd, g, n: (bd // 2, cidx(bd, n), ngrp + g)),
            pl.BlockSpec((1, c, wblk), lambda bd, g, n: (bd // 2, cidx(bd, n), 2 * ngrp + g)),
            pl.BlockSpec((1, c, LANES), lambda bd, g, n: (bd // 2, cidx(bd, n), bd % 2)),
            pl.BlockSpec((1, 1, LANES), lambda bd, g, n: (bd % 2, 0, 0)),
            pl.BlockSpec((1, 1, LANES), lambda bd, g, n: (bd % 2, 0, 0)),
        ],
        out_specs=pl.BlockSpec((1, 1, c, wblk), lambda bd, g, n: (bd % 2, bd // 2, cidx(bd, n), g)),
        scratch_shapes=[pltpu.VMEM((hg, HEAD_DIM, HEAD_DIM), F32)],
        compiler_params=_cparams(("parallel", "parallel", "arbitrary")),
        name="gated_delta_scan",
    )(qkv, qkv, qkv, gates, alog_row, dtb_row)


def _delta_out_kernel(of_ref, ob_ref, z_ref, w_ref, o_ref, *, cw):
    o = of_ref[0] + ob_ref[0]
    z = z_ref[...].astype(F32)
    w = w_ref[...]
    for hh in range(cw // HEAD_DIM):
        sl = slice(hh * HEAD_DIM, (hh + 1) * HEAD_DIM)
        oh = o[:, sl]
        ms = jnp.mean(oh * oh, axis=-1, keepdims=True)
        zh = z[:, sl]
        o_ref[:, sl] = (oh * lax.rsqrt(ms + NORM_EPS) * w * (zh * jax.nn.sigmoid(zh))).astype(o_ref.dtype)


def _delta_out(o2, proj_a, out_norm, *, z_col_block, tr=1024, cw=512):
    _, t, vdim = o2.shape
    tr = min(tr, t)
    return pl.pallas_call(
        functools.partial(_delta_out_kernel, cw=cw),
        out_shape=jax.ShapeDtypeStruct((t, vdim), BF16),
        grid=(t // tr, vdim // cw),
        in_specs=[
            pl.BlockSpec((1, tr, cw), lambda i, j: (0, i, j)),
            pl.BlockSpec((1, tr, cw), lambda i, j: (1, i, j)),
            pl.BlockSpec((tr, cw), lambda i, j: (i, z_col_block + j)),
            pl.BlockSpec((1, HEAD_DIM), lambda i, j: (0, 0)),
        ],
        out_specs=pl.BlockSpec((tr, cw), lambda i, j: (i, j)),
        compiler_params=_cparams(("parallel", "parallel")),
        name="delta_out_norm_gate",
    )(o2, o2, proj_a, out_norm.reshape(1, HEAD_DIM).astype(F32))


def _pair_swap_matrix(half):
    src = lax.broadcasted_iota(jnp.int32, (LANES, LANES), 0)
    dst = lax.broadcasted_iota(jnp.int32, (LANES, LANES), 1)
    partner = jnp.where((dst % (2 * half)) < half, dst + half, dst - half)
    return jnp.where(src == partner, 1.0, 0.0).astype(BF16)


def _rotate_pairs(x, swap):
    hi, lo = _split_bf16(x)
    return jnp.dot(hi, swap, preferred_element_type=F32) + jnp.dot(lo, swap, preferred_element_type=F32)


def _deinterleave_matrix(tr, dil, transpose=False):
    per = tr // dil
    i_out = lax.broadcasted_iota(jnp.int32, (tr, tr), 1 if transpose else 0)
    i_in = lax.broadcasted_iota(jnp.int32, (tr, tr), 0 if transpose else 1)
    src = (i_out % per) * dil + i_out // per
    return jnp.where(i_in == src, 1.0, 0.0).astype(BF16)


def _rope_deint_kernel(q_ref, k_ref, v_ref, cos_ref, sin_ref, o_ref, *, tr, dil, gw):
    cos = cos_ref[...]
    sin = sin_ref[...]
    per = tr // dil
    perm = _deinterleave_matrix(tr, dil) if dil > 1 else None

    def emit(y16, col0):
        if dil > 1:
            y16 = jnp.dot(perm, y16, preferred_element_type=F32).astype(BF16)
        for r in range(dil):
            o_ref[0, r, :, col0:col0 + HEAD_DIM] = y16[r * per:(r + 1) * per, :]

    half = B_ROT_DIM // 2
    swap = _pair_swap_matrix(half) if dil == 1 else None
    lane = lax.broadcasted_iota(jnp.int32, (tr, HEAD_DIM), 1)
    first = (lane % (2 * half)) < half
    for part, ref in enumerate((q_ref, k_ref)):
        for hh in range(gw // HEAD_DIM):
            sl = slice(hh * HEAD_DIM, (hh + 1) * HEAD_DIM)
            x16 = ref[:, sl]
            x = x16.astype(F32)
            if dil == 1:
                rot = jnp.dot(x16, swap, preferred_element_type=F32)
            else:
                rot = jnp.where(first, pltpu.roll(x, LANES - half, 1), pltpu.roll(x, half, 1))
            y = x * cos + rot * sin
            emit(y.astype(BF16), part * gw + hh * HEAD_DIM)
    for hh in range(gw // HEAD_DIM):
        emit(v_ref[:, hh * HEAD_DIM:(hh + 1) * HEAD_DIM], 2 * gw + hh * HEAD_DIM)


def _rope_deint(proj_b, cos_t, sin_t, gi, dil, *, bsz, seq, n_groups, tr=256):
    gw = B_HEADS_PER_GROUP * HEAD_DIM
    nsb = seq // tr
    per = tr // dil
    return pl.pallas_call(
        functools.partial(_rope_deint_kernel, tr=tr, dil=dil, gw=gw),
        out_shape=jax.ShapeDtypeStruct((bsz, dil, seq // dil, 3 * gw), BF16),
        grid=(bsz, nsb),
        in_specs=[pl.BlockSpec((tr, gw), lambda b, i: (b * nsb + i, gi)),
                  pl.BlockSpec((tr, gw), lambda b, i: (b * nsb + i, n_groups + gi)),
                  pl.BlockSpec((tr, gw), lambda b, i: (b * nsb + i, 2 * n_groups + gi)),
                  pl.BlockSpec((tr, HEAD_DIM), lambda b, i: (i, 0)),
                  pl.BlockSpec((tr, HEAD_DIM), lambda b, i: (i, 0))],
        out_specs=pl.BlockSpec((1, dil, per, 3 * gw), lambda b, i: (b, 0, i, 0)),
        compiler_params=_cparams(("parallel", "parallel")),
        name=f"rope_deinterleave_d{dil}",
    )(proj_b, proj_b, proj_b, cos_t, sin_t)


_BW = 64
_BQ = 128


def _dil_kernel(q_ref, kc_ref, kp_ref, kn_ref, vc_ref, vp_ref, vn_ref, o_ref, l_ref, kext, vext,
                *, qb, nh, seq_len):
    i = pl.program_id(2)
    base = i * qb
    kext[0:_BW, :] = kp_ref[0, 0]
    kext[_BW:_BW + qb, :] = kc_ref[0, 0]
    kext[_BW + qb:, :] = kn_ref[0, 0]
    vext[0:_BW, :] = vp_ref[0, 0]
    vext[_BW:_BW + qb, :] = vc_ref[0, 0]
    vext[_BW + qb:, :] = vn_ref[0, 0]
    scale = HEAD_DIM ** -0.5
    nkeys = _BQ + 2 * _BW
    qi = lax.broadcasted_iota(jnp.int32, (_BQ, nkeys), 0)
    kj = lax.broadcasted_iota(jnp.int32, (_BQ, nkeys), 1)
    lane = lax.broadcasted_iota(jnp.int32, (_BQ, LANES), 1)
    for sb in range(qb // _BQ):
        qpos = base + sb * _BQ + qi
        kpos = base + sb * _BQ - _BW + kj
        valid = (jnp.abs(kpos - qpos) <= _BW) & (kpos >= 0) & (kpos < seq_len)
        lse_blk = jnp.zeros((_BQ, LANES), F32)
        for hh in range(nh):
            sl = slice(hh * HEAD_DIM, (hh + 1) * HEAD_DIM)
            q = q_ref[0, 0, sb * _BQ:(sb + 1) * _BQ, sl]
            kx = kext[sb * _BQ:sb * _BQ + nkeys, sl]
            vx = vext[sb * _BQ:sb * _BQ + nkeys, sl]
            s = lax.dot_general(q, kx, (((1,), (1,)), ((), ())), preferred_element_type=F32) * scale
            s = jnp.where(valid, s, -1e30)
            m = jnp.max(s, axis=-1, keepdims=True)
            e = jnp.where(valid, jnp.exp(s - m), 0.0)
            l = jnp.sum(e, axis=-1, keepdims=True)
            p = (e / l).astype(BF16)
            o = jnp.dot(p, vx, preferred_element_type=F32)
            o_ref[0, 0, sb * _BQ:(sb + 1) * _BQ, sl] = o.astype(o_ref.dtype)
            lse_blk = jnp.where(lane == hh, m + jnp.log(l), lse_blk)
        l_ref[0, 0, sb * _BQ:(sb + 1) * _BQ, :] = lse_blk


def _dilated_group(grp_qkv):
    bsz, dil, ln, _ = grp_qkv.shape
    gw = B_HEADS_PER_GROUP * HEAD_DIM
    qb = min(512, ln)
    hb = qb // _BW
    nhalo = ln // _BW
    lo = lambda i: jnp.maximum(i * hb - 1, 0)
    hi = lambda i: jnp.minimum((i + 1) * hb, nhalo - 1)

    def cur(col):
        return pl.BlockSpec((1, 1, qb, gw), lambda b, r, i: (b, r, i, col))

    def halo(col, f):
        return pl.BlockSpec((1, 1, _BW, gw), lambda b, r, i: (b, r, f(i), col))

    return pl.pallas_call(
        functools.partial(_dil_kernel, qb=qb, nh=B_HEADS_PER_GROUP, seq_len=ln),
        out_shape=(jax.ShapeDtypeStruct((bsz, dil, ln, gw), BF16),
                   jax.ShapeDtypeStruct((bsz, dil, ln, LANES), F32)),
        grid=(bsz, dil, ln // qb),
        in_specs=[cur(0), cur(1), halo(1, lo), halo(1, hi), cur(2), halo(2, lo), halo(2, hi)],
        out_specs=(pl.BlockSpec((1, 1, qb, gw), lambda b, r, i: (b, r, i, 0)),
                   pl.BlockSpec((1, 1, qb, LANES), lambda b, r, i: (b, r, i, 0))),
        scratch_shapes=[pltpu.VMEM((qb + 2 * _BW, gw), BF16), pltpu.VMEM((qb + 2 * _BW, gw), BF16)],
        compiler_params=_cparams(("parallel", "parallel", "parallel")),
        name=f"dilated_attn_d{dil}",
    )(*([grp_qkv] * 7))


def _merge_kernel(*refs, tr, dils, nh):
    ng = len(dils)
    o_refs = refs[:ng]
    l_refs = refs[ng:2 * ng]
    out_ref = refs[2 * ng]
    outs, lses = [], []
    for g, dil in enumerate(dils):
        per = tr // dil
        if dil == 1:
            outs.append(o_refs[g][0, 0].astype(F32))
            lses.append(l_refs[g][0, 0])
            continue
        pt = _deinterleave_matrix(tr, dil, transpose=True)
        y = jnp.concatenate([o_refs[g][0, r] for r in range(dil)], axis=0)
        outs.append(jnp.dot(pt, y, preferred_element_type=F32))
        ls = jnp.concatenate([l_refs[g][0, r] for r in range(dil)], axis=0)
        h1 = ls.astype(BF16)
        r1 = ls - h1.astype(F32)
        h2 = r1.astype(BF16)
        h3 = (r1 - h2.astype(F32)).astype(BF16)
        lses.append(jnp.dot(pt, h1, preferred_element_type=F32)
                    + (jnp.dot(pt, h2, preferred_element_type=F32) + jnp.dot(pt, h3, preferred_element_type=F32)))
    mx = lses[0]
    for g in range(1, ng):
        mx = jnp.maximum(mx, lses[g])
    es = [jnp.exp(l - mx) for l in lses]
    den = es[0]
    for g in range(1, ng):
        den = den + es[g]
    ws = [e / den for e in es]
    lane = lax.broadcasted_iota(jnp.int32, (tr, LANES), 1)
    for hh in range(nh):
        sl = slice(hh * HEAD_DIM, (hh + 1) * HEAD_DIM)
        acc = None
        for g in range(ng):
            wcol = jnp.sum(jnp.where(lane == hh, ws[g], 0.0), axis=1, keepdims=True)
            term = outs[g][:, sl] * wcol
            acc = term if acc is None else acc + term
        out_ref[:, sl] = acc.astype(out_ref.dtype)


def _merge_groups(os_, ls_, *, bsz, seq, tr=256):
    gw = B_HEADS_PER_GROUP * HEAD_DIM
    dils = tuple(o.shape[1] for o in os_)
    nsb = seq // tr
    in_specs = ([pl.BlockSpec((1, d, tr // d, gw), lambda b, i: (b, 0, i, 0)) for d in dils]
                + [pl.BlockSpec((1, d, tr // d, LANES), lambda b, i: (b, 0, i, 0)) for d in dils])
    return pl.pallas_call(
        functools.partial(_merge_kernel, tr=tr, dils=dils, nh=B_HEADS_PER_GROUP),
        out_shape=jax.ShapeDtypeStruct((bsz * seq, gw), BF16),
        grid=(bsz, nsb),
        in_specs=in_specs,
        out_specs=pl.BlockSpec((tr, gw), lambda b, i: (b * nsb + i, 0)),
        compiler_params=_cparams(("parallel", "parallel")),
        name="dilated_merge",
    )(*os_, *ls_)


def _qknorm_rope_kernel(x_ref, wq_ref, wk_ref, cos_ref, sin_ref, o_ref, *, cw, n_q_blocks):
    j = pl.program_id(1)
    is_q = j < n_q_blocks
    w = jnp.where(is_q, wq_ref[...], wk_ref[...])
    scale = jnp.where(is_q, HEAD_DIM ** -0.5 * LOG2E, 1.0).astype(F32)
    cos = cos_ref[...]
    sin = sin_ref[...]
    swap = _pair_swap_matrix(HEAD_DIM // 4)
    for hh in range(cw // HEAD_DIM):
        sl = slice(hh * HEAD_DIM, (hh + 1) * HEAD_DIM)
        x = x_ref[:, sl].astype(F32)
        ms = jnp.mean(x * x, axis=-1, keepdims=True)
        y = x * lax.rsqrt(ms + NORM_EPS) * w
        y = y * cos + _rotate_pairs(y, swap) * sin
        o_ref[:, sl] = (y * scale).astype(o_ref.dtype)


def _qknorm_rope(qkv, q_norm, k_norm, cos_t, sin_t, *, q_width, qk_width, seq, tr=1024, cw=512):
    t = qkv.shape[0]
    tr = min(tr, seq)
    nsb = seq // tr
    return pl.pallas_call(
        functools.partial(_qknorm_rope_kernel, cw=cw, n_q_blocks=q_width // cw),
        out_shape=jax.ShapeDtypeStruct((t, qk_width), BF16),
        grid=(t // tr, qk_width // cw),
        in_specs=[pl.BlockSpec((tr, cw), lambda i, j: (i, j)),
                  pl.BlockSpec((1, HEAD_DIM), lambda i, j: (0, 0)),
                  pl.BlockSpec((1, HEAD_DIM), lambda i, j: (0, 0)),
                  pl.BlockSpec((tr, HEAD_DIM), lambda i, j: (i % nsb, 0)),
                  pl.BlockSpec((tr, HEAD_DIM), lambda i, j: (i % nsb, 0))],
        out_specs=pl.BlockSpec((tr, cw), lambda i, j: (i, j)),
        compiler_params=_cparams(("parallel", "parallel")),
        name="qk_norm_axial_rope",
    )(qkv, q_norm.reshape(1, HEAD_DIM).astype(F32), k_norm.reshape(1, HEAD_DIM).astype(F32), cos_t, sin_t)


def _flash_kernel(q_ref, k_ref, v_ref, o_ref, qs_ref, v1_ref, m_ref, acc_ref, *, tq, tk, rep, seq):
    @pl.when(pl.program_id(2) == 0)
    def _():
        v1_ref[:, :HEAD_DIM] = v_ref[0]
        v1_ref[:, HEAD_DIM:] = jnp.ones((seq, HEAD_DIM), BF16)

    if tk == seq:
        kt = k_ref[0]
        vt = v1_ref[...]
        s = [lax.dot_general(q_ref[0, :, r * HEAD_DIM:(r + 1) * HEAD_DIM], kt, (((1,), (1,)), ((), ())),
                             preferred_element_type=F32) for r in range(rep)]
        for r in range(rep):
            p = jnp.exp2(s[r] - jnp.max(s[r], axis=-1, keepdims=True)).astype(BF16)
            pv = jnp.dot(p, vt, preferred_element_type=F32)
            o_ref[0, :, r * HEAD_DIM:(r + 1) * HEAD_DIM] = (pv[:, :HEAD_DIM] / pv[:, HEAD_DIM:]).astype(o_ref.dtype)
        return

    for r in range(rep):
        qs_ref[r * tq:(r + 1) * tq, :] = q_ref[0, :, r * HEAD_DIM:(r + 1) * HEAD_DIM]
    m_ref[...] = jnp.full(m_ref.shape, -1e30, F32)
    acc_ref[...] = jnp.zeros(acc_ref.shape, F32)

    def body(t, carry):
        off = pl.multiple_of(t * tk, tk)
        kt = k_ref[0, pl.ds(off, tk), :]
        vt = v1_ref[pl.ds(off, tk), :]
        s = [lax.dot_general(qs_ref[r * tq:(r + 1) * tq, :], kt, (((1,), (1,)), ((), ())),
                             preferred_element_type=F32) for r in range(rep)]
        for r in range(rep):
            rows = slice(r * tq, (r + 1) * tq)
            m_old = m_ref[rows, :]
            m_new = jnp.maximum(m_old, jnp.max(s[r], axis=-1, keepdims=True))
            p = jnp.exp2(s[r] - m_new).astype(BF16)
            pv = jnp.dot(p, vt, preferred_element_type=F32)
            acc_ref[rows, :] = acc_ref[rows, :] * jnp.exp2(m_old - m_new) + pv
            m_ref[rows, :] = m_new
        return carry

    lax.fori_loop(0, seq // tk, body, 0)
    for r in range(rep):
        a = acc_ref[r * tq:(r + 1) * tq, :]
        o_ref[0, :, r * HEAD_DIM:(r + 1) * HEAD_DIM] = (a[:, :HEAD_DIM] / a[:, HEAD_DIM:]).astype(o_ref.dtype)


def _flash_attention(qk, qkv, *, bsz, seq, tq=512, tk=4096):
    qw = C_Q_HEADS * HEAD_DIM
    tq, tk = min(tq, seq), min(tk, seq)
    gq = C_REP * HEAD_DIM
    qk3 = qk.reshape(bsz, seq, qk.shape[-1])
    qkv3 = qkv.reshape(bsz, seq, qkv.shape[-1])
    k_blk0 = qw // HEAD_DIM
    v_blk0 = (qw + C_KV_HEADS * HEAD_DIM) // HEAD_DIM
    out = pl.pallas_call(
        functools.partial(_flash_kernel, tq=tq, tk=tk, rep=C_REP, seq=seq),
        out_shape=jax.ShapeDtypeStruct((bsz, seq, qw), BF16),
        grid=(bsz, C_KV_HEADS, seq // tq),
        in_specs=[pl.BlockSpec((1, tq, gq), lambda b, g, i: (b, i, g)),
                  pl.BlockSpec((1, seq, HEAD_DIM), lambda b, g, i: (b, 0, k_blk0 + g)),
                  pl.BlockSpec((1, seq, HEAD_DIM), lambda b, g, i: (b, 0, v_blk0 + g))],
        out_specs=pl.BlockSpec((1, tq, gq), lambda b, g, i: (b, i, g)),
        scratch_shapes=[pltpu.VMEM((C_REP * tq, HEAD_DIM), BF16),
                        pltpu.VMEM((seq, 2 * HEAD_DIM), BF16),
                        pltpu.VMEM((C_REP * tq, 1), F32),
                        pltpu.VMEM((C_REP * tq, 2 * HEAD_DIM), F32)],
        compiler_params=_cparams(("parallel", "parallel", "arbitrary")),
        name="gqa_flash_attention",
    )(qk3, qk3, qkv3)
    return out.reshape(bsz * seq, qw)


def _rope_cos_sin(pos, dim, theta):
    inv = 1.0 / (theta ** (jnp.arange(0, dim, 2, dtype=F32) / dim))
    ang = pos.astype(F32)[:, None] * inv[None, :]
    return jnp.cos(ang), jnp.sin(ang)


def _tables(seq):
    tok = jnp.arange(seq)
    cos_b, sin_b = _rope_cos_sin(tok, B_ROT_DIM, B_ROPE_THETA)
    ones = jnp.ones((seq, HEAD_DIM - B_ROT_DIM), F32)
    cb = jnp.concatenate([cos_b, cos_b, ones], axis=-1)
    sb = jnp.concatenate([-sin_b, sin_b, jnp.zeros_like(ones)], axis=-1)
    cos_r, sin_r = _rope_cos_sin(tok // GRID_W, HEAD_DIM // 2, C_ROPE_THETA)
    cos_c, sin_c = _rope_cos_sin(tok % GRID_W, HEAD_DIM // 2, C_ROPE_THETA)
    cc = jnp.concatenate([cos_r, cos_r, cos_c, cos_c], axis=-1)
    sc = jnp.concatenate([-sin_r, sin_r, -sin_c, sin_c], axis=-1)
    return cb, sb, cc, sc


def _ffn(x2d, norm_w, w_gate, w_up, w_down, layer):
    h = _rmsnorm(x2d, norm_w, BF16)
    act = _gateup(h, w_gate, w_up, layer, tm=2048, tn=256)
    dff = act.shape[1]
    nsplit = 2 if (dff // 2) % LANES == 0 else 1
    for kb in range(nsplit):
        x2d = _wmatmul([(act, kb, dff // nsplit)], w_down, w_lead=layer, w_row_blk=kb, n=x2d.shape[1],
                       tm=1024, tn=512, out_dtype=F32, residual=x2d)
    return x2d


def _mixer_ab(x2d, norm_w, w_in, conv_w, a_log, dt_bias, out_norm, w_out, j, cb, sb, *, bsz, seq):
    t, dm = x2d.shape
    n_heads = a_log.shape[-1]
    a_qk = n_heads * HEAD_DIM
    a_v = n_heads * HEAD_DIM
    a_qkv = 2 * a_qk + a_v
    ngate = 2 * n_heads
    b_w = (w_in.shape[-1] - a_qkv - a_v - 2 * ngate) // 3
    h = _rmsnorm(x2d, norm_w, BF16)

    w_in_t = jnp.swapaxes(w_in, 1, 2)
    g0 = a_qkv + a_v
    w_beta = w_in_t[j, g0:g0 + ngate].reshape(2, n_heads, dm)
    w_alpha = w_in_t[j, g0 + ngate:g0 + 2 * ngate].reshape(2, n_heads, dm)
    w_g = jnp.concatenate([w_beta, w_alpha, jnp.zeros((2, LANES - 2 * n_heads, dm), F32)], axis=1)
    w_g = w_g.reshape(1, 2 * LANES, dm)
    w_b = _shift_cast_rows(w_in_t, j, g0 + 2 * ngate, 3 * b_w)

    proj_a = _wmatmul_t(h, w_in_t, w_lead=j, n=a_qkv + a_v, tm=2048, tn=512, out_dtype=BF16)
    gates = _wmatmul_t(h, w_g, w_lead=0, n=2 * LANES, tm=2048, tn=2 * LANES, out_dtype=F32)
    proj_b = _wmatmul_t(h, w_b, w_lead=0, n=3 * b_w, tm=2048, tn=512, out_dtype=BF16)

    qkv = _conv_silu_l2(proj_a.reshape(bsz, seq, -1), conv_w, qk_width=2 * a_qk, q_width=a_qk, n_ch=a_qkv)
    pad = jnp.zeros((2, LANES - 2 * n_heads), F32)
    alog_row = jnp.concatenate([jnp.zeros((2, n_heads), F32), a_log.astype(F32), pad], axis=-1).reshape(2, 1, LANES)
    dtb_row = jnp.concatenate([jnp.zeros((2, n_heads), F32), dt_bias.astype(F32), pad], axis=-1).reshape(2, 1, LANES)
    o2 = _delta_scan(qkv, gates.reshape(bsz, seq, 2 * LANES), alog_row, dtb_row, n_heads=n_heads)
    o_a = _delta_out(o2.reshape(2, t, a_v), proj_a, out_norm, z_col_block=a_qkv // 512)

    n_groups = len(B_PATTERNS)
    os_, ls_ = [], []
    for gi, (_, dil) in enumerate(B_PATTERNS):
        grp = _rope_deint(proj_b, cb, sb, gi, dil, bsz=bsz, seq=seq, n_groups=n_groups)
        o_g, l_g = _dilated_group(grp)
        os_.append(o_g)
        ls_.append(l_g)
    o_b = _merge_groups(os_, ls_, bsz=bsz, seq=seq)

    return _wmatmul([(o_a, 0, a_v), (o_b, 0, o_b.shape[1])], w_out, w_lead=j, n=dm, tm=1024, tn=512,
                    out_dtype=F32, residual=x2d)


def _mixer_c(x2d, norm_w, w_qkv, q_norm, k_norm, w_out, j, cc, sc, *, bsz, seq):
    dm = x2d.shape[1]
    h = _rmsnorm(x2d, norm_w, BF16)
    qkv = _wmatmul([(h, 0, dm)], w_qkv, w_lead=j, n=w_qkv.shape[-1], tm=2048, tn=512, out_dtype=BF16)
    qw = C_Q_HEADS * HEAD_DIM
    kw = C_KV_HEADS * HEAD_DIM
    qk = _qknorm_rope(qkv, q_norm, k_norm, cc, sc, q_width=qw, qk_width=qw + kw, seq=seq)
    o = _flash_attention(qk, qkv, bsz=bsz, seq=seq)
    return _wmatmul([(o, 0, qw)], w_out, w_lead=j, n=dm, tm=1024, tn=512, out_dtype=F32, residual=x2d)


def kernel(x, norm_mix, norm_ffn, norm_final, ab_w_in, ab_conv_w, ab_a_log, ab_dt_bias, ab_out_norm, ab_w_out,
           c_w_qkv, c_q_norm, c_k_norm, c_w_out, ffn_w_gate, ffn_w_up, ffn_w_down):
    bsz, seq, dm = x.shape
    depth = norm_mix.shape[0]
    cb, sb, cc, sc = _tables(seq)
    x2d = x.reshape(bsz * seq, dm)
    for layer in range(depth):
        j = layer // 2
        if layer % 2 == 0:
            x2d = _mixer_ab(x2d, norm_mix[layer], ab_w_in, ab_conv_w[j], ab_a_log[j], ab_dt_bias[j],
                            ab_out_norm[j], ab_w_out, j, cb, sb, bsz=bsz, seq=seq)
        else:
            x2d = _mixer_c(x2d, norm_mix[layer], c_w_qkv, c_q_norm[j], c_k_norm[j], c_w_out, j,
                           cc, sc, bsz=bsz, seq=seq)
        x2d = _ffn(x2d, norm_ffn[layer], ffn_w_gate, ffn_w_up, ffn_w_down, layer)
    out = _rmsnorm(x2d, norm_final, x.dtype)
    return out.reshape(bsz, seq, dm)
```
